```python
import math
import jax
import jax.numpy as jnp
from jax import lax
import numpy as np

D_MODEL = 1024
BATCH = 4
SEQ = 8192
DEPTH = 2

CHUNK = 64
CONV_K = 4
GDN_HEADS = D_MODEL // 128
GDN_DK = 128
GDN_DV = 128
GDN_QK = GDN_HEADS * GDN_DK
GDN_V = GDN_HEADS * GDN_DV
GDN_CONV_CH = 2 * GDN_QK + GDN_V
SSD_P = 64
SSD_HEADS = D_MODEL // SSD_P
SSD_G = 2
SSD_N = 128
SSD_INNER = SSD_HEADS * SSD_P
SSD_BC = SSD_G * SSD_N
SSD_CONV_CH = SSD_INNER + 2 * SSD_BC
MIX_WIDTH = GDN_V + SSD_INNER
IN_WIDTH = GDN_CONV_CH + GDN_V + 2 * GDN_HEADS + SSD_INNER + SSD_CONV_CH + SSD_HEADS
N_EXPERTS = 32
TOP_K = 4
D_FF = D_MODEL
SWIGLU_ALPHA = 1.702
SWIGLU_LIMIT = 7.0
EXPERT_BLOCK = 128
EPS = 1e-6

kernel_name = "hybrid_gdn_ssd_moe_adaln_block"


def rmsnorm(x, g):
    xf = x.astype(jnp.float32)
    y = xf * lax.rsqrt(jnp.mean(xf * xf, axis=-1, keepdims=True) + EPS)
    return (y * g.astype(jnp.float32)).astype(x.dtype)


def l2norm(x):
    return x * lax.rsqrt(jnp.sum(x * x, axis=-1, keepdims=True) + EPS)


def causal_dwconv(x, w):
    k, ch = w.shape
    return lax.conv_general_dilated(x, w[:, None, :], window_strides=(1,), padding=[(k - 1, 0)],
                                    dimension_numbers=("NWC", "WIO", "NWC"), feature_group_count=ch)


def gated_delta_rule_chunked(q, k, v, beta, g):
    bsz, s, h, dk = q.shape
    dv = v.shape[-1]
    nc = s // CHUNK

    def blocks(t):
        return jnp.moveaxis(t.reshape((bsz, nc, CHUNK, h) + t.shape[3:]), 3, 1)

    qc, kc, vc, bc, gc = blocks(q), blocks(k), blocks(v), blocks(beta), blocks(g)
    gam = jnp.cumsum(gc, axis=-1)
    incl = jnp.tril(jnp.ones((CHUNK, CHUNK), bool))
    strict = jnp.tril(jnp.ones((CHUNK, CHUNK), bool), -1)
    decay = jnp.exp(jnp.where(incl, gam[..., :, None] - gam[..., None, :], -jnp.inf))
    kk = jnp.einsum("bhnld,bhnsd->bhnls", kc, kc)
    a_mat = jnp.where(strict, bc[..., :, None] * kk * decay, 0.0) + jnp.eye(CHUNK, dtype=q.dtype)
    rhs = jnp.concatenate([vc * bc[..., None], kc * (bc * jnp.exp(gam))[..., None]], axis=-1)
    sol = lax.linalg.triangular_solve(a_mat, rhs, left_side=True, lower=True, unit_diagonal=True)
    u, w = sol[..., :dv], sol[..., dv:]
    qk = jnp.einsum("bhnld,bhnsd->bhnls", qc, kc) * decay
    q_dec = qc * jnp.exp(gam)[..., None]
    k_dec = kc * jnp.exp(gam[..., -1:] - gam)[..., None]
    chunk_dec = jnp.exp(gam[..., -1])

    def step(state, inp):
        u_i, w_i, qk_i, qd_i, kd_i, cd_i = inp
        v_new = u_i - jnp.einsum("bhld,bhdv->bhlv", w_i, state)
        o_i = jnp.einsum("bhld,bhdv->bhlv", qd_i, state) + jnp.einsum("bhls,bhsv->bhlv", qk_i, v_new)
        state = cd_i[..., None, None] * state + jnp.einsum("bhld,bhlv->bhdv", kd_i, v_new)
        return state, o_i

    xs = (jnp.moveaxis(u, 2, 0), jnp.moveaxis(w, 2, 0), jnp.moveaxis(qk, 2, 0),
          jnp.moveaxis(q_dec, 2, 0), jnp.moveaxis(k_dec, 2, 0), jnp.moveaxis(chunk_dec, 2, 0))
    s0 = jnp.zeros((bsz, h, dk, dv), q.dtype)
    _, o = lax.scan(step, s0, xs)
    return jnp.transpose(o, (1, 0, 3, 2, 4)).reshape(bsz, s, h, dv)


def ssd_chunked(x, dt, a_log, bm, cm):
    bsz, s, h, p = x.shape
    g, n = bm.shape[2], bm.shape[3]
    e = h // g
    nc = s // CHUNK
    a = -jnp.exp(a_log) * dt
    xc = (x * dt[..., None]).reshape(bsz, nc, CHUNK, g, e, p)
    ac = a.reshape(bsz, nc, CHUNK, g, e)
    bc = bm.reshape(bsz, nc, CHUNK, g, n)
    cc = cm.reshape(bsz, nc, CHUNK, g, n)
    a_cum = jnp.cumsum(ac, axis=2)
    incl = jnp.tril(jnp.ones((CHUNK, CHUNK), bool))
    seg = a_cum[:, :, :, None] - a_cum[:, :, None, :]
    decay = jnp.exp(jnp.where(incl[:, :, None, None], seg, -jnp.inf))
    cb = jnp.einsum("bclgn,bcsgn->bclsg", cc, bc)
    y_diag = jnp.einsum("bclsge,bcsgep->bclgep", cb[..., None] * decay, xc)
    states = jnp.einsum("bclgn,bclge,bclgep->bcgepn", bc, jnp.exp(a_cum[:, :, -1:] - a_cum), xc)
    chunk_dec = jnp.exp(a_cum[:, :, -1])

    def step(hs, inp):
        st, cd = inp
        return cd[..., None, None] * hs + st, hs

    h0 = jnp.zeros((bsz, g, e, p, n), x.dtype)
    _, h_prev = lax.scan(step, h0, (jnp.moveaxis(states, 1, 0), jnp.moveaxis(chunk_dec, 1, 0)))
    h_prev = jnp.moveaxis(h_prev, 0, 1)
    y_off = jnp.einsum("bclgn,bcgepn,bclge->bclgep", cc, h_prev, jnp.exp(a_cum))
    return (y_diag + y_off).reshape(bsz, s, h, p)


def hybrid_mixer(h, w_in, gdn_conv_w, gdn_a_log, gdn_dt_bias, gdn_norm_g,
                 ssd_conv_w, ssd_conv_b, ssd_a_log, ssd_dt_bias, ssd_d, ssd_norm_g, w_out):
    bsz, s, _ = h.shape
    f32 = jnp.float32
    proj = h @ w_in
    off = 0
    gdn_qkv = proj[..., off:off + GDN_CONV_CH]; off += GDN_CONV_CH
    gdn_z = proj[..., off:off + GDN_V]; off += GDN_V
    gdn_b = proj[..., off:off + GDN_HEADS]; off += GDN_HEADS
    gdn_a = proj[..., off:off + GDN_HEADS]; off += GDN_HEADS
    ssd_z = proj[..., off:off + SSD_INNER]; off += SSD_INNER
    ssd_xbc = proj[..., off:off + SSD_CONV_CH]; off += SSD_CONV_CH
    ssd_dt = proj[..., off:off + SSD_HEADS]

    qkv = jax.nn.silu(causal_dwconv(gdn_qkv, gdn_conv_w)).astype(f32)
    q = l2norm(qkv[..., :GDN_QK].reshape(bsz, s, GDN_HEADS, GDN_DK)) * (GDN_DK ** -0.5)
    k = l2norm(qkv[..., GDN_QK:2 * GDN_QK].reshape(bsz, s, GDN_HEADS, GDN_DK))
    v = qkv[..., 2 * GDN_QK:].reshape(bsz, s, GDN_HEADS, GDN_DV)
    beta = jax.nn.sigmoid(gdn_b.astype(f32))
    g = -jnp.exp(gdn_a_log.astype(f32)) * jax.nn.softplus(gdn_a.astype(f32) + gdn_dt_bias.astype(f32))
    o = gated_delta_rule_chunked(q, k, v, beta, g)
    o = rmsnorm(o, gdn_norm_g) * jax.nn.silu(gdn_z.astype(f32).reshape(bsz, s, GDN_HEADS, GDN_DV))
    gdn_out = o.reshape(bsz, s, GDN_V).astype(h.dtype)

    xbc = jax.nn.silu(causal_dwconv(ssd_xbc, ssd_conv_w) + ssd_conv_b).astype(f32)
    xs = xbc[..., :SSD_INNER].reshape(bsz, s, SSD_HEADS, SSD_P)
    bm = xbc[..., SSD_INNER:SSD_INNER + SSD_BC].reshape(bsz, s, SSD_G, SSD_N)
    cm = xbc[..., SSD_INNER + SSD_BC:].reshape(bsz, s, SSD_G, SSD_N)
    dt = jax.nn.softplus(ssd_dt.astype(f32) + ssd_dt_bias.astype(f32))
    y = ssd_chunked(xs, dt, ssd_a_log.astype(f32), bm, cm) + ssd_d.astype(f32)[:, None] * xs
    y = y.reshape(bsz, s, SSD_INNER) * jax.nn.silu(ssd_z.astype(f32))
    y = rmsnorm(y.reshape(bsz, s, SSD_G, SSD_INNER // SSD_G),
                ssd_norm_g.reshape(SSD_G, SSD_INNER // SSD_G)).reshape(bsz, s, SSD_INNER)
    ssd_out = y.astype(h.dtype)

    return jnp.concatenate([gdn_out, ssd_out], axis=-1) @ w_out


def clamped_swiglu(gu):
    gate, up = gu[..., :D_FF], gu[..., D_FF:]
    gate = jnp.minimum(gate, SWIGLU_LIMIT)
    up = jnp.clip(up, -SWIGLU_LIMIT, SWIGLU_LIMIT)
    return (up + 1.0) * (gate * jax.nn.sigmoid(gate * SWIGLU_ALPHA))


def moe_ffn(h, router_w, router_b, w_gu, b_gu, w_down, b_down):
    bsz, s, d = h.shape
    n_tok = bsz * s
    hf = h.reshape(n_tok, d)
    logits = (hf @ router_w + router_b).astype(jnp.float32)
    top_logit, top_idx = lax.top_k(logits, TOP_K)
    gates = jax.nn.softmax(top_logit, axis=-1)
    n_assign = n_tok * TOP_K
    eid = top_idx.reshape(n_assign)
    tok = jnp.arange(n_assign, dtype=jnp.int32) // TOP_K
    order = jnp.argsort(eid)
    eid_s = eid[order]
    counts = jnp.bincount(eid, length=N_EXPERTS)
    padded = (counts + EXPERT_BLOCK - 1) // EXPERT_BLOCK * EXPERT_BLOCK
    pad_end = jnp.cumsum(padded)
    pad_start = pad_end - padded
    start = jnp.cumsum(counts) - counts
    dest = pad_start[eid_s] + jnp.arange(n_assign, dtype=jnp.int32) - start[eid_s]
    n_rows = (-(-n_assign // EXPERT_BLOCK) + N_EXPERTS) * EXPERT_BLOCK
    n_blocks = n_rows // EXPERT_BLOCK
    row_tok = jnp.full((n_rows,), n_tok, jnp.int32).at[dest].set(tok[order])
    row_gate = jnp.zeros((n_rows,), jnp.float32).at[dest].set(gates.reshape(n_assign)[order])
    blk_start = jnp.arange(n_blocks, dtype=jnp.int32) * EXPERT_BLOCK
    blk_expert = jnp.minimum(jnp.searchsorted(pad_end, blk_start, side="right"), N_EXPERTS - 1)
    h_pad = jnp.concatenate([hf, jnp.zeros((1, d), hf.dtype)], axis=0)
    x_rows = h_pad[row_tok].reshape(n_blocks, EXPERT_BLOCK, d)

    def expert_block(args):
        xb, e = args
        y = clamped_swiglu(xb @ w_gu[e] + b_gu[e])
        return y @ w_down[e] + b_down[e]

    y_rows = lax.map(expert_block, (x_rows, blk_expert)).reshape(n_rows, d)
    y_rows = y_rows * row_gate[:, None].astype(y_rows.dtype)
    out = jax.ops.segment_sum(y_rows, row_tok, num_segments=n_tok + 1)[:n_tok]
    return out.reshape(bsz, s, d)


def setup_inputs(seed: int = 0) -> dict:
    key = jax.random.key(seed)
    ks = jax.random.split(key, 26)
    f32 = jnp.float32
    L = DEPTH

    def nrm(k, shape, scale):
        return jax.random.normal(k, shape, f32) * scale

    def a_log_init(k, h):
        return jnp.log(jax.random.uniform(k, (L, h), f32, minval=1.0, maxval=16.0))

    def dt_bias_init(k, h):
        dt = jnp.exp(jax.random.uniform(k, (L, h), f32, minval=math.log(1e-3), maxval=math.log(1e-1)))
        return dt + jnp.log(-jnp.expm1(-dt))

    return {
        "x": nrm(ks[0], (BATCH, SEQ, D_MODEL), 1.0),
        "c": nrm(ks[1], (BATCH, D_MODEL), 1.0),
        "ada_w": nrm(ks[2], (L, D_MODEL, 6 * D_MODEL), 0.5 * D_MODEL ** -0.5),
        "ada_b": nrm(ks[3], (L, 6 * D_MODEL), 0.02),
        "norm1_g": 1.0 + nrm(ks[4], (L, D_MODEL), 0.02),
        "norm2_g": 1.0 + nrm(ks[5], (L, D_MODEL), 0.02),
        "w_in": nrm(ks[6], (L, D_MODEL, IN_WIDTH), D_MODEL ** -0.5),
        "gdn_conv_w": nrm(ks[7], (L, CONV_K, GDN_CONV_CH), CONV_K ** -0.5),
        "gdn_a_log": a_log_init(ks[8], GDN_HEADS),
        "gdn_dt_bias": dt_bias_init(ks[9], GDN_HEADS),
        "gdn_norm_g": 1.0 + nrm(ks[10], (L, GDN_DV), 0.02),
        "ssd_conv_w": nrm(ks[11], (L, CONV_K, SSD_CONV_CH), CONV_K ** -0.5),
        "ssd_conv_b": nrm(ks[12], (L, SSD_CONV_CH), 0.02),
        "ssd_a_log": a_log_init(ks[13], SSD_HEADS),
        "ssd_dt_bias": dt_bias_init(ks[14], SSD_HEADS),
        "ssd_d": 1.0 + nrm(ks[15], (L, SSD_HEADS), 0.02),
        "ssd_norm_g": 1.0 + nrm(ks[16], (L, SSD_INNER), 0.02),
        "w_out": nrm(ks[17], (L, MIX_WIDTH, D_MODEL), MIX_WIDTH ** -0.5),
        "router_w": nrm(ks[18], (L, D_MODEL, N_EXPERTS), D_MODEL ** -0.5),
        "router_b": nrm(ks[19], (L, N_EXPERTS), 0.01),
        "moe_w_gu": nrm(ks[20], (L, N_EXPERTS, D_MODEL, 2 * D_FF), D_MODEL ** -0.5),
        "moe_b_gu": nrm(ks[21], (L, N_EXPERTS, 2 * D_FF), 0.01),
        "moe_w_down": nrm(ks[22], (L, N_EXPERTS, D_FF, D_MODEL), D_FF ** -0.5),
        "moe_b_down": nrm(ks[23], (L, N_EXPERTS, D_MODEL), 0.01),
        "final_g": 1.0 + nrm(ks[24], (D_MODEL,), 0.02),
    }


def reference(x, c, ada_w, ada_b, norm1_g, norm2_g, w_in, gdn_conv_w, gdn_a_log, gdn_dt_bias,
              gdn_norm_g, ssd_conv_w, ssd_conv_b, ssd_a_log, ssd_dt_bias, ssd_d, ssd_norm_g, w_out,
              router_w, router_b, moe_w_gu, moe_b_gu, moe_w_down, moe_b_down, final_g):
    c_act = jax.nn.silu(c)
    for l in range(DEPTH):
        mod = (c_act @ ada_w[l] + ada_b[l])[:, None, :]
        sh1, sc1, g1, sh2, sc2, g2 = jnp.split(mod, 6, axis=-1)
        h = rmsnorm(x, norm1_g[l]) * (1.0 + sc1) + sh1
        x = x + g1 * hybrid_mixer(h, w_in[l], gdn_conv_w[l], gdn_a_log[l], gdn_dt_bias[l], gdn_norm_g[l],
                                  ssd_conv_w[l], ssd_conv_b[l], ssd_a_log[l], ssd_dt_bias[l], ssd_d[l],
                                  ssd_norm_g[l], w_out[l])
        h = rmsnorm(x, norm2_g[l]) * (1.0 + sc2) + sh2
        x = x + g2 * moe_ffn(h, router_w[l], router_b[l], moe_w_gu[l], moe_b_gu[l],
                             moe_w_down[l], moe_b_down[l])
    return rmsnorm(x, final_g)
```

```python
import functools
import math

import jax
import jax.numpy as jnp
from jax import lax
from jax.experimental import pallas as pl
from jax.experimental.pallas import tpu as pltpu
from jax.experimental.pallas import tpu_sc as plsc

F32 = jnp.float32
BF16 = jnp.bfloat16
HIGHEST = lax.Precision.HIGHEST

D_MODEL = 1024
CHUNK = 64
CONV_K = 4
GDN_HEADS = 8
GDN_DK = 128
GDN_DV = 128
GDN_QK = GDN_HEADS * GDN_DK
GDN_V = GDN_HEADS * GDN_DV
GDN_CONV_CH = 2 * GDN_QK + GDN_V
SSD_P = 64
SSD_HEADS = 16
SSD_G = 2
SSD_N = 128
SSD_INNER = SSD_HEADS * SSD_P
SSD_BC = SSD_G * SSD_N
SSD_CONV_CH = SSD_INNER + 2 * SSD_BC
N_EXPERTS = 32
TOP_K = 4
D_FF = D_MODEL
SWIGLU_ALPHA = 1.702
SWIGLU_LIMIT = 7.0
EPS = 1e-6

LANES = 128
PG_W = GDN_CONV_CH + GDN_V
PS_W = SSD_INNER + SSD_CONV_CH
LANE_BETA = 0
LANE_ALPHA = GDN_HEADS
LANE_DT = 2 * GDN_HEADS
VMEM_LIMIT = 56 * 1024 * 1024

TM_PROJ = 256
TT_SCAN = 512
ROW_BLOCK = 256
FF_CHUNK = 512
SC_WINDOW = 64


def _dot(a, b):
    return jnp.dot(a, b, preferred_element_type=F32)


def _dot_nt(a, b):
    return lax.dot_general(a, b, (((1,), (1,)), ((), ())), preferred_element_type=F32)


def _dot_tn(a, b):
    return lax.dot_general(a, b, (((0,), (0,)), ((), ())), preferred_element_type=F32)


def _sigmoid(x):
    return 1.0 / (1.0 + jnp.exp(-x))


def _silu(x):
    return x * _sigmoid(x)


def _softplus(x):
    return jnp.maximum(x, 0.0) + jnp.log(1.0 + jnp.exp(-jnp.abs(x)))


def _ada_kernel(c_ref, w_ref, b_ref, o_ref):
    c = c_ref[...]
    o_ref[0] = jnp.dot(_silu(c), w_ref[0], precision=HIGHEST, preferred_element_type=F32) + b_ref[0]


def _ada_mod(c, ada_w, ada_b):
    depth = ada_w.shape[0]
    bsz = c.shape[0]
    rows = 8
    c8 = jnp.zeros((rows, D_MODEL), F32).at[:bsz].set(c)
    out = pl.pallas_call(
        _ada_kernel,
        grid=(depth, 6),
        in_specs=[
            pl.BlockSpec((rows, D_MODEL), lambda l, j: (0, 0)),
            pl.BlockSpec((1, D_MODEL, D_MODEL), lambda l, j: (l, 0, j)),
            pl.BlockSpec((1, 1, D_MODEL), lambda l, j: (l, 0, j)),
        ],
        out_specs=pl.BlockSpec((1, rows, D_MODEL), lambda l, j: (l, 0, j)),
        out_shape=jax.ShapeDtypeStruct((depth, rows, 6 * D_MODEL), F32),
        name="ada_mod",
    )(c8, ada_w, ada_b.reshape(depth, 1, 6 * D_MODEL))
    mod = out[:, :bsz].reshape(depth, bsz, 6, D_MODEL)
    return jnp.concatenate([mod, jnp.zeros((depth, bsz, 2, D_MODEL), F32)], axis=2)


def _modulated_norm(x, g, shift, scale):
    ms = jnp.mean(x * x, axis=-1, keepdims=True)
    return (x * lax.rsqrt(ms + EPS) * g) * (1.0 + scale) + shift


def _inproj_kernel(x_ref, mod_ref, g_ref, w_ref, og_ref, os_ref, om_ref):
    h = _modulated_norm(x_ref[...], g_ref[...], mod_ref[0:1, :], mod_ref[1:2, :]).astype(BF16)
    og_ref[...] = _dot(h, w_ref[:, 0:PG_W]).astype(BF16)
    os_ref[...] = _dot(h, w_ref[:, PG_W:PG_W + PS_W]).astype(BF16)
    om_ref[...] = _dot(h, w_ref[:, PG_W + PS_W:])


def _inproj(x2, mod_l, g, w_perm, seq):
    n_tok = x2.shape[0]
    tm = min(TM_PROJ, seq)
    wtot = w_perm.shape[1]
    return pl.pallas_call(
        _inproj_kernel,
        grid=(n_tok // tm,),
        in_specs=[
            pl.BlockSpec((tm, D_MODEL), lambda i: (i, 0)),
            pl.BlockSpec((None, 8, D_MODEL), lambda i: ((i * tm) // seq, 0, 0)),
            pl.BlockSpec((1, D_MODEL), lambda i: (0, 0)),
            pl.BlockSpec((D_MODEL, wtot), lambda i: (0, 0)),
        ],
        out_specs=[
            pl.BlockSpec((tm, PG_W), lambda i: (i, 0)),
            pl.BlockSpec((tm, PS_W), lambda i: (i, 0)),
            pl.BlockSpec((tm, LANES), lambda i: (i, 0)),
        ],
        out_shape=[
            jax.ShapeDtypeStruct((n_tok, PG_W), BF16),
            jax.ShapeDtypeStruct((n_tok, PS_W), BF16),
            jax.ShapeDtypeStruct((n_tok, LANES), F32),
        ],
        compiler_params=pltpu.CompilerParams(
            dimension_semantics=("parallel",), vmem_limit_bytes=VMEM_LIMIT),
        name="norm_inproj",
    )(x2, mod_l, g, w_perm)


def _tri_masks():
    row = lax.broadcasted_iota(jnp.int32, (CHUNK, CHUNK), 0)
    col = lax.broadcasted_iota(jnp.int32, (CHUNK, CHUNK), 1)
    return row, col


def _conv_silu(src_ref, r0, halo, col0, cw_ref, bias_ref):
    x = src_ref[pl.ds(r0, CHUNK), col0:col0 + LANES].astype(F32)
    prev = halo[:, col0:col0 + LANES].astype(F32)[8:16, :]
    xe = jnp.concatenate([prev, x], axis=0)
    w = cw_ref[:, col0:col0 + LANES]
    y = x * w[CONV_K - 1:CONV_K, :]
    for s in range(1, CONV_K):
        y = y + xe[8 - s:8 - s + CHUNK, :] * w[CONV_K - 1 - s:CONV_K - s, :]
    if bias_ref is not None:
        y = y + bias_ref[:, col0:col0 + LANES]
    return _silu(y)


def _halo_rows(src_ref, halo_ref, c, r0, width):
    start = jnp.maximum(r0 - 16, 0)
    inner = src_ref[pl.ds(pl.multiple_of(start, 16), 16), 0:width]
    return jnp.where(c == 0, halo_ref[...], inner)


def _unit_lower_inverse(nmat, row, col):
    eye = (row == col).astype(F32)
    x = eye - jnp.where((row >> 1) == (col >> 1), nmat, 0.0)
    for lg in range(2, CHUNK.bit_length()):
        blk = jnp.where(((row >> lg) == (col >> lg)) & ((row >> (lg - 1)) != (col >> (lg - 1))), nmat, 0.0)
        y = _dot(blk.astype(BF16), x.astype(BF16))
        x = x - _dot(x.astype(BF16), y.astype(BF16))
    return x


def _gdn_kernel(pg_ref, pm_ref, cw_ref, nega_ref, dtb_ref, ng_ref, o_ref, s_ref, halo_ref, *, tt):
    t = pl.program_id(1)

    @pl.when(t == 0)
    def _():
        s_ref[...] = jnp.zeros_like(s_ref)
        halo_ref[...] = jnp.zeros_like(halo_ref)

    row, col = _tri_masks()
    incl = row >= col
    strict = row > col
    tril = incl.astype(F32)
    nega = nega_ref[...]
    dtb = dtb_ref[...]
    ng = ng_ref[...]

    def chunk(c, carry):
        r0 = pl.multiple_of(c * CHUNK, CHUNK)
        pmv = pm_ref[pl.ds(r0, CHUNK), :]
        beta_all = _sigmoid(pmv)
        g_all = nega * _softplus(pmv + dtb)
        gam = jnp.dot(tril, g_all, precision=HIGHEST, preferred_element_type=F32)
        gam_t = gam.T
        gam_last = gam[CHUNK - 1:CHUNK, :]
        e_gam = jnp.exp(gam)
        e_rest = jnp.exp(gam_last - gam)
        e_last = jnp.exp(gam_last)
        halo = _halo_rows(pg_ref, halo_ref, c, r0, GDN_CONV_CH)

        for h in range(GDN_HEADS):
            la = LANE_ALPHA + h
            q = _conv_silu(pg_ref, r0, halo, h * GDN_DK, cw_ref, None)
            k = _conv_silu(pg_ref, r0, halo, GDN_QK + h * GDN_DK, cw_ref, None)
            v = _conv_silu(pg_ref, r0, halo, 2 * GDN_QK + h * GDN_DV, cw_ref, None)
            q = q * lax.rsqrt(jnp.sum(q * q, axis=-1, keepdims=True) + EPS) * (GDN_DK ** -0.5)
            k = k * lax.rsqrt(jnp.sum(k * k, axis=-1, keepdims=True) + EPS)
            bcol = beta_all[:, LANE_BETA + h:LANE_BETA + h + 1]
            gcol = gam[:, la:la + 1]
            grow = gam_t[la:la + 1, :]
            decay = jnp.exp(jnp.where(incl, gcol - grow, -jnp.inf))
            kb = k.astype(BF16)
            qkk = _dot_nt(jnp.concatenate([q, k], axis=0).astype(BF16), kb)
            qk = qkk[0:CHUNK, :]
            kk = qkk[CHUNK:2 * CHUNK, :]
            nmat = jnp.where(strict, bcol * kk * decay, 0.0)
            tinv = _unit_lower_inverse(nmat, row, col)
            egc = e_gam[:, la:la + 1]
            rhs = jnp.concatenate([v * bcol, k * (bcol * egc)], axis=1).astype(BF16)
            sol = _dot(tinv.astype(BF16), rhs)
            u = sol[:, 0:GDN_DV]
            w = sol[:, GDN_DV:]
            state = s_ref[h]
            wq = jnp.concatenate([w, q * egc], axis=0).astype(BF16)
            r = _dot(wq, state.astype(BF16))
            v_new = u - r[0:CHUNK, :]
            o = r[CHUNK:, :] + _dot((qk * decay).astype(BF16), v_new.astype(BF16))
            kd = (k * e_rest[:, la:la + 1]).astype(BF16)
            s_ref[h] = e_last[:, la:la + 1] * state + _dot_tn(kd, v_new.astype(BF16))
            z = pg_ref[pl.ds(r0, CHUNK), GDN_CONV_CH + h * GDN_DV:GDN_CONV_CH + (h + 1) * GDN_DV].astype(F32)
            on = o * lax.rsqrt(jnp.mean(o * o, axis=-1, keepdims=True) + EPS) * ng
            o_ref[pl.ds(r0, CHUNK), h * GDN_DV:(h + 1) * GDN_DV] = (on * _silu(z)).astype(BF16)
        return carry

    lax.fori_loop(0, tt // CHUNK, chunk, 0)
    halo_ref[...] = pg_ref[tt - 16:tt, 0:GDN_CONV_CH]


def _gdn(pg, pm, conv_w, a_log, dt_bias, norm_g, bsz, seq):
    tt = min(TT_SCAN, seq)
    nt = seq // tt
    nega = jnp.zeros((1, LANES), F32).at[0, LANE_ALPHA:LANE_ALPHA + GDN_HEADS].set(-jnp.exp(a_log))
    dtb = jnp.zeros((1, LANES), F32).at[0, LANE_ALPHA:LANE_ALPHA + GDN_HEADS].set(dt_bias)
    return pl.pallas_call(
        functools.partial(_gdn_kernel, tt=tt),
        grid=(bsz, nt),
        in_specs=[
            pl.BlockSpec((tt, PG_W), lambda b, t: (b * nt + t, 0)),
            pl.BlockSpec((tt, LANES), lambda b, t: (b * nt + t, 0)),
            pl.BlockSpec((CONV_K, GDN_CONV_CH), lambda b, t: (0, 0)),
            pl.BlockSpec((1, LANES), lambda b, t: (0, 0)),
            pl.BlockSpec((1, LANES), lambda b, t: (0, 0)),
            pl.BlockSpec((1, GDN_DV), lambda b, t: (0, 0)),
        ],
        out_specs=pl.BlockSpec((tt, GDN_V), lambda b, t: (b * nt + t, 0)),
        out_shape=jax.ShapeDtypeStruct((bsz * seq, GDN_V), BF16),
        scratch_shapes=[
            pltpu.VMEM((GDN_HEADS, GDN_DK, GDN_DV), F32),
            pltpu.VMEM((16, GDN_CONV_CH), BF16),
        ],
        compiler_params=pltpu.CompilerParams(
            dimension_semantics=("parallel", "arbitrary"), vmem_limit_bytes=VMEM_LIMIT),
        name="gdn_scan",
    )(pg, pm, conv_w, nega, dtb, norm_g.reshape(1, GDN_DV))


def _ssd_kernel(ps_ref, pm_ref, cw_ref, cb_ref, nega_ref, dtb_ref, dsk_ref, ng_ref, o_ref,
                h_ref, halo_ref, *, tt):
    t = pl.program_id(1)

    @pl.when(t == 0)
    def _():
        h_ref[...] = jnp.zeros_like(h_ref)
        halo_ref[...] = jnp.zeros_like(halo_ref)

    row, col = _tri_masks()
    incl = row >= col
    tril = incl.astype(F32)
    nega = nega_ref[...]
    dtb = dtb_ref[...]
    dsk = dsk_ref[...]
    lane = lax.broadcasted_iota(jnp.int32, (CHUNK, LANES), 1)
    lo_half = lane < SSD_P
    lane1 = lax.broadcasted_iota(jnp.int32, (1, LANES), 1)
    lo_half1 = lane1 < SSD_P
    heads_per_group = SSD_HEADS // SSD_G
    gw = SSD_INNER // SSD_G

    def pair_cols(arr, l0):
        sel = lo_half if arr.shape[0] == CHUNK else lo_half1
        return jnp.where(sel, arr[:, l0:l0 + 1], arr[:, l0 + 1:l0 + 2])

    def chunk(c, carry):
        r0 = pl.multiple_of(c * CHUNK, CHUNK)
        pmv = pm_ref[pl.ds(r0, CHUNK), :]
        dt_all = _softplus(pmv + dtb)
        a_all = nega * dt_all
        acum = jnp.dot(tril, a_all, precision=HIGHEST, preferred_element_type=F32)
        acum_t = acum.T
        a_last = acum[CHUNK - 1:CHUNK, :]
        e_a = jnp.exp(acum)
        e_rest = jnp.exp(a_last - acum)
        e_last = jnp.exp(a_last)
        halo = _halo_rows(ps_ref, halo_ref, c, r0, PS_W)

        def conv(col0):
            return _conv_silu(ps_ref, r0, halo, SSD_INNER + col0, cw_ref, cb_ref)

        for g in range(SSD_G):
            bg = conv(SSD_INNER + g * SSD_N).astype(BF16)
            cg = conv(SSD_INNER + SSD_BC + g * SSD_N).astype(BF16)
            cbm = _dot_nt(cg, bg)
            hstate = h_ref[g]
            y_off = _dot(cg, hstate.astype(BF16))
            xw_parts = []
            scale_parts = []
            y_parts = []
            for p in range(heads_per_group // 2):
                head0 = g * heads_per_group + 2 * p
                l0 = LANE_DT + head0
                x_p = conv(head0 * SSD_P)
                xdt = x_p * pair_cols(dt_all, l0)
                xdt_b = xdt.astype(BF16)
                res = []
                for j in range(2):
                    acol = acum[:, l0 + j:l0 + j + 1]
                    arow = acum_t[l0 + j:l0 + j + 1, :]
                    decay = jnp.exp(jnp.where(incl, acol - arow, -jnp.inf))
                    res.append(_dot((cbm * decay).astype(BF16), xdt_b))
                y_diag = jnp.where(lo_half, res[0], res[1])
                y_p = (y_diag + pair_cols(e_a, l0) * y_off[:, p * LANES:(p + 1) * LANES]
                       + pair_cols(dsk, l0) * x_p)
                y_parts.append(y_p)
                xw_parts.append((xdt * pair_cols(e_rest, l0)).astype(BF16))
                scale_parts.append(pair_cols(e_last, l0))
            xw = jnp.concatenate(xw_parts, axis=1)
            scale = jnp.concatenate(scale_parts, axis=1)
            h_ref[g] = scale * hstate + _dot_tn(bg, xw)
            y = jnp.concatenate(y_parts, axis=1)
            z = ps_ref[pl.ds(r0, CHUNK), g * gw:(g + 1) * gw].astype(F32)
            yz = y * _silu(z)
            yn = yz * lax.rsqrt(jnp.mean(yz * yz, axis=-1, keepdims=True) + EPS)
            o_ref[pl.ds(r0, CHUNK), g * gw:(g + 1) * gw] = (yn * ng_ref[:, g * gw:(g + 1) * gw]).astype(BF16)
        return carry

    lax.fori_loop(0, tt // CHUNK, chunk, 0)
    halo_ref[...] = ps_ref[tt - 16:tt, :]


def _ssd(ps, pm, conv_w, conv_b, a_log, dt_bias, d_skip, norm_g, bsz, seq):
    tt = min(TT_SCAN, seq)
    nt = seq // tt

    def lanes(v):
        return jnp.zeros((1, LANES), F32).at[0, LANE_DT:LANE_DT + SSD_HEADS].set(v)

    cw = jnp.concatenate([jnp.zeros((CONV_K, SSD_INNER), F32), conv_w], axis=1)
    cb = jnp.concatenate([jnp.zeros((1, SSD_INNER), F32), conv_b.reshape(1, -1)], axis=1)
    return pl.pallas_call(
        functools.partial(_ssd_kernel, tt=tt),
        grid=(bsz, nt),
        in_specs=[
            pl.BlockSpec((tt, PS_W), lambda b, t: (b * nt + t, 0)),
            pl.BlockSpec((tt, LANES), lambda b, t: (b * nt + t, 0)),
            pl.BlockSpec((CONV_K, PS_W), lambda b, t: (0, 0)),
            pl.BlockSpec((1, PS_W), lambda b, t: (0, 0)),
            pl.BlockSpec((1, LANES), lambda b, t: (0, 0)),
            pl.BlockSpec((1, LANES), lambda b, t: (0, 0)),
            pl.BlockSpec((1, LANES), lambda b, t: (0, 0)),
            pl.BlockSpec((1, SSD_INNER), lambda b, t: (0, 0)),
        ],
        out_specs=pl.BlockSpec((tt, SSD_INNER), lambda b, t: (b * nt + t, 0)),
        out_shape=jax.ShapeDtypeStruct((bsz * seq, SSD_INNER), BF16),
        scratch_shapes=[
            pltpu.VMEM((SSD_G, SSD_N, SSD_INNER // SSD_G), F32),
            pltpu.VMEM((16, PS_W), BF16),
        ],
        compiler_params=pltpu.CompilerParams(
            dimension_semantics=("parallel", "arbitrary"), vmem_limit_bytes=VMEM_LIMIT),
        name="ssd_scan",
    )(ps, pm, cw, cb, lanes(-jnp.exp(a_log)), lanes(dt_bias), lanes(d_skip), norm_g.reshape(1, SSD_INNER))


RT_IDX = 0
RT_GATE = TOP_K
RT_POS = 2 * TOP_K


def _outproj_router_kernel(go_ref, so_ref, x_ref, mod_ref, g_ref, wo_ref, rw_ref, rb_ref,
                           xn_ref, h_ref, rt_ref, cnt_ref, run_ref, *, tm):
    i = pl.program_id(0)

    @pl.when(i == 0)
    def _():
        run_ref[...] = jnp.zeros_like(run_ref)

    mix = _dot(go_ref[...], wo_ref[0:GDN_V, :]) + _dot(so_ref[...], wo_ref[GDN_V:, :])
    xn = x_ref[...] + mod_ref[2:3, :] * mix
    xn_ref[...] = xn
    h = _modulated_norm(xn, g_ref[...], mod_ref[3:4, :], mod_ref[4:5, :])
    h_ref[...] = h
    logits = jnp.dot(h, rw_ref[...], precision=HIGHEST, preferred_element_type=F32) + rb_ref[...]
    lane = lax.broadcasted_iota(jnp.int32, (tm, LANES), 1).astype(F32)
    work = logits
    tops = []
    idxs = []
    for _ in range(TOP_K):
        m = jnp.max(work, axis=-1, keepdims=True)
        idx = jnp.min(jnp.where(work == m, lane, float(LANES)), axis=-1, keepdims=True)
        work = jnp.where(lane == idx, -jnp.inf, work)
        tops.append(m)
        idxs.append(idx)
    exps = [jnp.exp(m - tops[0]) for m in tops]
    denom = exps[0] + exps[1] + exps[2] + exps[3]
    onehot = jnp.zeros((tm, LANES), F32)
    for idx in idxs:
        onehot = onehot + (lane == idx).astype(F32)
    trow = lax.broadcasted_iota(jnp.int32, (tm, tm), 0)
    tcol = lax.broadcasted_iota(jnp.int32, (tm, tm), 1)
    before = _dot((trow > tcol).astype(BF16), onehot.astype(BF16)) + run_ref[0:1, :]
    rt = jnp.zeros((tm, LANES), F32)
    for k in range(TOP_K):
        pos = jnp.sum(jnp.where(lane == idxs[k], before, 0.0), axis=-1, keepdims=True)
        rt = jnp.where(lane == RT_IDX + k, idxs[k], rt)
        rt = jnp.where(lane == RT_GATE + k, exps[k] / denom, rt)
        rt = jnp.where(lane == RT_POS + k, pos, rt)
    rt_ref[...] = rt
    run = run_ref[0:1, :] + jnp.sum(onehot, axis=0, keepdims=True)
    run_ref[...] = jnp.broadcast_to(run, run_ref.shape)
    cnt_ref[...] = jnp.broadcast_to(run, cnt_ref.shape)


def _outproj_router(go, so, x2, mod_l, g, w_out, rw, rb, seq):
    n_tok = x2.shape[0]
    tm = min(TM_PROJ, seq)
    return pl.pallas_call(
        functools.partial(_outproj_router_kernel, tm=tm),
        grid=(n_tok // tm,),
        in_specs=[
            pl.BlockSpec((tm, GDN_V), lambda i: (i, 0)),
            pl.BlockSpec((tm, SSD_INNER), lambda i: (i, 0)),
            pl.BlockSpec((tm, D_MODEL), lambda i: (i, 0)),
            pl.BlockSpec((None, 8, D_MODEL), lambda i: ((i * tm) // seq, 0, 0)),
            pl.BlockSpec((1, D_MODEL), lambda i: (0, 0)),
            pl.BlockSpec((GDN_V + SSD_INNER, D_MODEL), lambda i: (0, 0)),
            pl.BlockSpec((D_MODEL, LANES), lambda i: (0, 0)),
            pl.BlockSpec((1, LANES), lambda i: (0, 0)),
        ],
        out_specs=[
            pl.BlockSpec((tm, D_MODEL), lambda i: (i, 0)),
            pl.BlockSpec((tm, D_MODEL), lambda i: (i, 0)),
            pl.BlockSpec((tm, LANES), lambda i: (i, 0)),
            pl.BlockSpec((8, LANES), lambda i: (0, 0)),
        ],
        out_shape=[
            jax.ShapeDtypeStruct((n_tok, D_MODEL), F32),
            jax.ShapeDtypeStruct((n_tok, D_MODEL), F32),
            jax.ShapeDtypeStruct((n_tok, LANES), F32),
            jax.ShapeDtypeStruct((8, LANES), F32),
        ],
        scratch_shapes=[pltpu.VMEM((8, LANES), F32)],
        compiler_params=pltpu.CompilerParams(
            dimension_semantics=("arbitrary",), vmem_limit_bytes=VMEM_LIMIT),
        name="outproj_router",
    )(go, so, x2, mod_l, g, w_out, rw, rb)


def _sc_workers():
    info = plsc.get_sparse_core_info()
    return info.num_cores, info.num_subcores


def _scatter_rows(src, dest_t, n_rows):
    n_tok, width = src.shape
    n_k = dest_t.shape[0]
    nc, ns = _sc_workers()
    per_w = n_tok // (nc * ns)
    win = min(SC_WINDOW, per_w)
    mesh = plsc.VectorSubcoreMesh(core_axis_name="c", subcore_axis_name="s")

    @functools.partial(
        pl.kernel, mesh=mesh,
        out_type=jax.ShapeDtypeStruct((n_rows, width), src.dtype),
        scratch_types=[pltpu.VMEM((win,), jnp.int32), pltpu.VMEM((win, width), src.dtype)],
    )
    def scatter_kernel(src_hbm, idx_hbm, out_hbm, idx_v, rows_v):
        wid = lax.axis_index("s") * nc + lax.axis_index("c")
        base = wid * per_w

        @pl.loop(0, per_w // win)
        def _(j):
            off = base + j * win
            pltpu.sync_copy(src_hbm.at[pl.ds(off, win)], rows_v)
            for k in range(n_k):
                pltpu.sync_copy(idx_hbm.at[k, pl.ds(off, win)], idx_v)
                pltpu.sync_copy(rows_v, out_hbm.at[idx_v])

    return scatter_kernel(src, dest_t)


def _gather_rows(table, idx):
    n_idx = idx.shape[0]
    width = table.shape[1]
    nc, ns = _sc_workers()
    per_w = n_idx // (nc * ns)
    win = min(SC_WINDOW, per_w)
    mesh = plsc.VectorSubcoreMesh(core_axis_name="c", subcore_axis_name="s")

    @functools.partial(
        pl.kernel, mesh=mesh,
        out_type=jax.ShapeDtypeStruct((n_idx, width), table.dtype),
        scratch_types=[pltpu.VMEM((win,), jnp.int32), pltpu.VMEM((win, width), table.dtype)],
    )
    def gather_kernel(table_hbm, idx_hbm, out_hbm, idx_v, rows_v):
        wid = lax.axis_index("s") * nc + lax.axis_index("c")
        base = wid * per_w

        @pl.loop(0, per_w // win)
        def _(j):
            off = base + j * win
            pltpu.sync_copy(idx_hbm.at[pl.ds(off, win)], idx_v)
            pltpu.sync_copy(table_hbm.at[idx_v], rows_v)
            pltpu.sync_copy(rows_v, out_hbm.at[pl.ds(off, win)])

    return gather_kernel(table, idx)


def _expert_kernel(be_ref, bv_ref, x_ref, wgu_ref, bgu_ref, wd_ref, bd_ref, y_ref, wgu_b, wd_b, *, tr):
    i = pl.program_id(0)
    e = be_ref[i]
    prev = be_ref[jnp.maximum(i - 1, 0)]

    @pl.when((i == 0) | (e != prev))
    def _():
        wgu_b[...] = wgu_ref[0].astype(BF16)
        wd_b[...] = wd_ref[0].astype(BF16)

    valid = bv_ref[i]

    @pl.when(valid == 0)
    def _():
        y_ref[...] = jnp.zeros_like(y_ref)

    @pl.when(valid > 0)
    def _():
        rows = lax.broadcasted_iota(jnp.int32, (tr, D_MODEL), 0)
        x = jnp.where(rows < valid, x_ref[...], 0.0).astype(BF16)
        acc = jnp.zeros((tr, D_MODEL), F32)
        for f in range(0, D_FF, FF_CHUNK):
            gate = _dot(x, wgu_b[:, f:f + FF_CHUNK]) + bgu_ref[0, :, f:f + FF_CHUNK]
            up = _dot(x, wgu_b[:, D_FF + f:D_FF + f + FF_CHUNK]) + bgu_ref[0, :, D_FF + f:D_FF + f + FF_CHUNK]
            gate = jnp.minimum(gate, SWIGLU_LIMIT)
            up = jnp.clip(up, -SWIGLU_LIMIT, SWIGLU_LIMIT)
            act = (up + 1.0) * (gate * _sigmoid(gate * SWIGLU_ALPHA))
            acc = acc + _dot(act.astype(BF16), wd_b[f:f + FF_CHUNK, :])
        y_ref[...] = acc + bd_ref[0]


def _experts(x_rows, blk_expert, blk_valid, w_gu, b_gu, w_down, b_down):
    n_rows = x_rows.shape[0]
    tr = ROW_BLOCK
    nb = n_rows // tr
    grid_spec = pltpu.PrefetchScalarGridSpec(
        num_scalar_prefetch=2,
        grid=(nb,),
        in_specs=[
            pl.BlockSpec((tr, D_MODEL), lambda i, be, bv: (i, 0)),
            pl.BlockSpec((1, D_MODEL, 2 * D_FF), lambda i, be, bv: (be[i], 0, 0)),
            pl.BlockSpec((1, 1, 2 * D_FF), lambda i, be, bv: (be[i], 0, 0)),
            pl.BlockSpec((1, D_FF, D_MODEL), lambda i, be, bv: (be[i], 0, 0)),
            pl.BlockSpec((1, 1, D_MODEL), lambda i, be, bv: (be[i], 0, 0)),
        ],
        out_specs=pl.BlockSpec((tr, D_MODEL), lambda i, be, bv: (i, 0)),
        scratch_shapes=[pltpu.VMEM((D_MODEL, 2 * D_FF), BF16), pltpu.VMEM((D_FF, D_MODEL), BF16)],
    )
    return pl.pallas_call(
        functools.partial(_expert_kernel, tr=tr),
        grid_spec=grid_spec,
        out_shape=jax.ShapeDtypeStruct((n_rows, D_MODEL), F32),
        compiler_params=pltpu.CompilerParams(
            dimension_semantics=("arbitrary",), vmem_limit_bytes=VMEM_LIMIT),
        name="moe_experts",
    )(blk_expert, blk_valid, x_rows, w_gu, b_gu.reshape(N_EXPERTS, 1, 2 * D_FF), w_down,
      b_down.reshape(N_EXPERTS, 1, D_MODEL))


def _combine_kernel(xn_ref, y4_ref, rt_ref, mod_ref, g_ref, o_ref, *, final):
    rt = rt_ref[...]
    acc = rt[:, RT_GATE:RT_GATE + 1] * y4_ref[:, 0:D_MODEL]
    for k in range(1, TOP_K):
        acc = acc + rt[:, RT_GATE + k:RT_GATE + k + 1] * y4_ref[:, k * D_MODEL:(k + 1) * D_MODEL]
    x = xn_ref[...] + mod_ref[5:6, :] * acc
    if final:
        ms = jnp.mean(x * x, axis=-1, keepdims=True)
        x = x * lax.rsqrt(ms + EPS) * g_ref[...]
    o_ref[...] = x


def _combine(xn, y4, rt, mod_l, final_g, seq, final):
    n_tok = xn.shape[0]
    tm = min(TM_PROJ, seq)
    return pl.pallas_call(
        functools.partial(_combine_kernel, final=final),
        grid=(n_tok // tm,),
        in_specs=[
            pl.BlockSpec((tm, D_MODEL), lambda i: (i, 0)),
            pl.BlockSpec((tm, TOP_K * D_MODEL), lambda i: (i, 0)),
            pl.BlockSpec((tm, LANES), lambda i: (i, 0)),
            pl.BlockSpec((None, 8, D_MODEL), lambda i: ((i * tm) // seq, 0, 0)),
            pl.BlockSpec((1, D_MODEL), lambda i: (0, 0)),
        ],
        out_specs=pl.BlockSpec((tm, D_MODEL), lambda i: (i, 0)),
        out_shape=jax.ShapeDtypeStruct((n_tok, D_MODEL), F32),
        compiler_params=pltpu.CompilerParams(
            dimension_semantics=("parallel",), vmem_limit_bytes=VMEM_LIMIT),
        name="moe_combine",
    )(xn, y4, rt, mod_l, final_g)


def _permute_w_in(w_in):
    off = 0
    gdn_qkvz = w_in[:, off:off + PG_W]; off += PG_W
    small_ba = w_in[:, off:off + 2 * GDN_HEADS]; off += 2 * GDN_HEADS
    ssd_zxbc = w_in[:, off:off + PS_W]; off += PS_W
    ssd_dt = w_in[:, off:off + SSD_HEADS]
    pad = jnp.zeros((w_in.shape[0], LANES - 2 * GDN_HEADS - SSD_HEADS), w_in.dtype)
    return jnp.concatenate([gdn_qkvz, ssd_zxbc, small_ba, ssd_dt, pad], axis=1).astype(BF16)


def _moe_schedule(rt, counts_row, n_rows):
    idx = rt[:, RT_IDX:RT_IDX + TOP_K].astype(jnp.int32)
    pos = rt[:, RT_POS:RT_POS + TOP_K].astype(jnp.int32)
    counts = counts_row[0, :N_EXPERTS].astype(jnp.int32)
    padded = (counts + ROW_BLOCK - 1) // ROW_BLOCK * ROW_BLOCK
    pad_end = jnp.cumsum(padded)
    pad_start = pad_end - padded
    dest = pad_start[idx] + pos
    blk_start = jnp.arange(n_rows // ROW_BLOCK, dtype=jnp.int32) * ROW_BLOCK
    blk_expert = jnp.minimum(jnp.searchsorted(pad_end, blk_start, side="right"), N_EXPERTS - 1).astype(jnp.int32)
    blk_valid = jnp.clip(counts[blk_expert] - (blk_start - pad_start[blk_expert]), 0, ROW_BLOCK)
    blk_valid = jnp.where(blk_start < pad_end[-1], blk_valid, 0).astype(jnp.int32)
    return dest, blk_expert, blk_valid


def _forward(x, c, ada_w, ada_b, norm1_g, norm2_g, w_in, gdn_conv_w, gdn_a_log, gdn_dt_bias, gdn_norm_g,
             ssd_conv_w, ssd_conv_b, ssd_a_log, ssd_dt_bias, ssd_d, ssd_norm_g, w_out, router_w, router_b,
             moe_w_gu, moe_b_gu, moe_w_down, moe_b_down, final_g):
    bsz, seq, _ = x.shape
    depth = ada_w.shape[0]
    n_tok = bsz * seq
    n_rows = n_tok * TOP_K + N_EXPERTS * ROW_BLOCK
    mod = _ada_mod(c, ada_w, ada_b)
    x2 = x.reshape(n_tok, D_MODEL)
    fg = final_g.reshape(1, D_MODEL)
    for l in range(depth):
        pg, ps, pm = _inproj(x2, mod[l], norm1_g[l].reshape(1, D_MODEL), _permute_w_in(w_in[l]), seq)
        go = _gdn(pg, pm, gdn_conv_w[l], gdn_a_log[l], gdn_dt_bias[l], gdn_norm_g[l], bsz, seq)
        so = _ssd(ps, pm, ssd_conv_w[l], ssd_conv_b[l], ssd_a_log[l], ssd_dt_bias[l], ssd_d[l],
                  ssd_norm_g[l], bsz, seq)
        rw = jnp.zeros((D_MODEL, LANES), F32).at[:, :N_EXPERTS].set(router_w[l])
        rb = jnp.full((1, LANES), -1e30, F32).at[0, :N_EXPERTS].set(router_b[l])
        xn, h2, rt, cnt = _outproj_router(go, so, x2, mod[l], norm2_g[l].reshape(1, D_MODEL),
                                          w_out[l].astype(BF16), rw, rb, seq)
        dest, blk_expert, blk_valid = _moe_schedule(rt, cnt, n_rows)
        x_rows = _scatter_rows(h2, dest.T, n_rows)
        y_rows = _experts(x_rows, blk_expert, blk_valid, moe_w_gu[l], moe_b_gu[l], moe_w_down[l], moe_b_down[l])
        y4 = _gather_rows(y_rows, dest.reshape(n_tok * TOP_K)).reshape(n_tok, TOP_K * D_MODEL)
        x2 = _combine(xn, y4, rt, mod[l], fg, seq, final=(l == depth - 1))
    return x2.reshape(bsz, seq, D_MODEL)


def kernel(x, c, ada_w, ada_b, norm1_g, norm2_g, w_in, gdn_conv_w, gdn_a_log, gdn_dt_bias, gdn_norm_g, ssd_conv_w, ssd_conv_b, ssd_a_log, ssd_dt_bias, ssd_d, ssd_norm_g, w_out, router_w, router_b, moe_w_gu, moe_b_gu, moe_w_down, moe_b_down, final_g):
    return _forward(x, c, ada_w, ada_b, norm1_g, norm2_g, w_in, gdn_conv_w, gdn_a_log, gdn_dt_bias, gdn_norm_g,
                    ssd_conv_w, ssd_conv_b, ssd_a_log, ssd_dt_bias, ssd_d, ssd_norm_g, w_out, router_w, router_b,
                    moe_w_gu, moe_b_gu, moe_w_down, moe_b_down, final_g)
```

```python
import functools
import math

import jax
import jax.numpy as jnp
from jax import lax
from jax.experimental import pallas as pl
from jax.experimental.pallas import tpu as pltpu
from jax.experimental.pallas import tpu_sc as plsc

F32 = jnp.float32
BF16 = jnp.bfloat16
HIGHEST = lax.Precision.HIGHEST

D_MODEL = 1024
CHUNK = 64
CONV_K = 4
GDN_HEADS = 8
GDN_DK = 128
GDN_DV = 128
GDN_QK = GDN_HEADS * GDN_DK
GDN_V = GDN_HEADS * GDN_DV
GDN_CONV_CH = 2 * GDN_QK + GDN_V
SSD_P = 64
SSD_HEADS = 16
SSD_G = 2
SSD_N = 128
SSD_INNER = SSD_HEADS * SSD_P
SSD_BC = SSD_G * SSD_N
SSD_CONV_CH = SSD_INNER + 2 * SSD_BC
N_EXPERTS = 32
TOP_K = 4
D_FF = D_MODEL
SWIGLU_ALPHA = 1.702
SWIGLU_LIMIT = 7.0
EPS = 1e-6

LANES = 128
PG_W = GDN_CONV_CH + GDN_V
PS_W = SSD_INNER + SSD_CONV_CH
LANE_BETA = 0
LANE_ALPHA = GDN_HEADS
LANE_DT = 2 * GDN_HEADS
VMEM_LIMIT = 56 * 1024 * 1024

TM_PROJ = 256
TT_SCAN = 512
ROW_BLOCK = 256
FF_CHUNK = 512
SC_WINDOW = 64


def _dot(a, b):
    return jnp.dot(a, b, preferred_element_type=F32)


def _dot_nt(a, b):
    return lax.dot_general(a, b, (((1,), (1,)), ((), ())), preferred_element_type=F32)


def _dot_tn(a, b):
    return lax.dot_general(a, b, (((0,), (0,)), ((), ())), preferred_element_type=F32)


def _sigmoid(x):
    return 1.0 / (1.0 + jnp.exp(-x))


def _silu(x):
    return x * _sigmoid(x)


def _softplus(x):
    return jnp.maximum(x, 0.0) + jnp.log(1.0 + jnp.exp(-jnp.abs(x)))


def _ada_kernel(c_ref, w_ref, b_ref, o_ref):
    c = c_ref[...]
    o_ref[0] = jnp.dot(_silu(c), w_ref[0], precision=HIGHEST, preferred_element_type=F32) + b_ref[0]


def _ada_mod(c, ada_w, ada_b):
    depth = ada_w.shape[0]
    bsz = c.shape[0]
    rows = 8
    c8 = jnp.zeros((rows, D_MODEL), F32).at[:bsz].set(c)
    out = pl.pallas_call(
        _ada_kernel,
        grid=(depth, 6),
        in_specs=[
            pl.BlockSpec((rows, D_MODEL), lambda l, j: (0, 0)),
            pl.BlockSpec((1, D_MODEL, D_MODEL), lambda l, j: (l, 0, j)),
            pl.BlockSpec((1, 1, D_MODEL), lambda l, j: (l, 0, j)),
        ],
        out_specs=pl.BlockSpec((1, rows, D_MODEL), lambda l, j: (l, 0, j)),
        out_shape=jax.ShapeDtypeStruct((depth, rows, 6 * D_MODEL), F32),
        name="ada_mod",
    )(c8, ada_w, ada_b.reshape(depth, 1, 6 * D_MODEL))
    mod = out[:, :bsz].reshape(depth, bsz, 6, D_MODEL)
    return jnp.concatenate([mod, jnp.zeros((depth, bsz, 2, D_MODEL), F32)], axis=2)


def _modulated_norm(x, g, shift, scale):
    ms = jnp.mean(x * x, axis=-1, keepdims=True)
    return (x * lax.rsqrt(ms + EPS) * g) * (1.0 + scale) + shift


def _inproj_kernel(x_ref, mod_ref, g_ref, w_ref, og_ref, os_ref, om_ref):
    h = _modulated_norm(x_ref[...], g_ref[...], mod_ref[0:1, :], mod_ref[1:2, :]).astype(BF16)
    og_ref[...] = _dot(h, w_ref[:, 0:PG_W]).astype(BF16)
    os_ref[...] = _dot(h, w_ref[:, PG_W:PG_W + PS_W]).astype(BF16)
    om_ref[...] = _dot(h, w_ref[:, PG_W + PS_W:])


def _inproj(x2, mod_l, g, w_perm, seq):
    n_tok = x2.shape[0]
    tm = min(TM_PROJ, seq)
    wtot = w_perm.shape[1]
    return pl.pallas_call(
        _inproj_kernel,
        grid=(n_tok // tm,),
        in_specs=[
            pl.BlockSpec((tm, D_MODEL), lambda i: (i, 0)),
            pl.BlockSpec((None, 8, D_MODEL), lambda i: ((i * tm) // seq, 0, 0)),
            pl.BlockSpec((1, D_MODEL), lambda i: (0, 0)),
            pl.BlockSpec((D_MODEL, wtot), lambda i: (0, 0)),
        ],
        out_specs=[
            pl.BlockSpec((tm, PG_W), lambda i: (i, 0)),
            pl.BlockSpec((tm, PS_W), lambda i: (i, 0)),
            pl.BlockSpec((tm, LANES), lambda i: (i, 0)),
        ],
        out_shape=[
            jax.ShapeDtypeStruct((n_tok, PG_W), BF16),
            jax.ShapeDtypeStruct((n_tok, PS_W), BF16),
            jax.ShapeDtypeStruct((n_tok, LANES), F32),
        ],
        compiler_params=pltpu.CompilerParams(
            dimension_semantics=("parallel",), vmem_limit_bytes=VMEM_LIMIT),
        name="norm_inproj",
    )(x2, mod_l, g, w_perm)


def _tri_masks():
    row = lax.broadcasted_iota(jnp.int32, (CHUNK, CHUNK), 0)
    col = lax.broadcasted_iota(jnp.int32, (CHUNK, CHUNK), 1)
    return row, col


def _conv_silu(src_ref, r0, halo, col0, cw_ref, bias_ref):
    x = src_ref[pl.ds(r0, CHUNK), col0:col0 + LANES].astype(F32)
    prev = halo[:, col0:col0 + LANES].astype(F32)[8:16, :]
    xe = jnp.concatenate([prev, x], axis=0)
    w = cw_ref[:, col0:col0 + LANES]
    y = x * w[CONV_K - 1:CONV_K, :]
    for s in range(1, CONV_K):
        y = y + xe[8 - s:8 - s + CHUNK, :] * w[CONV_K - 1 - s:CONV_K - s, :]
    if bias_ref is not None:
        y = y + bias_ref[:, col0:col0 + LANES]
    return _silu(y)


def _halo_rows(src_ref, halo_ref, c, r0, width):
    start = jnp.maximum(r0 - 16, 0)
    inner = src_ref[pl.ds(pl.multiple_of(start, 16), 16), 0:width]
    return jnp.where(c == 0, halo_ref[...], inner)


def _unit_lower_inverses(nmats, row, col):
    eye = (row == col).astype(F32)
    pair = (row >> 1) == (col >> 1)
    xs = [eye - jnp.where(pair, n, 0.0) for n in nmats]
    for lg in range(2, CHUNK.bit_length()):
        mask = ((row >> lg) == (col >> lg)) & ((row >> (lg - 1)) != (col >> (lg - 1)))
        xb = [x.astype(BF16) for x in xs]
        ys = [_dot(jnp.where(mask, n, 0.0).astype(BF16), b) for n, b in zip(nmats, xb)]
        xs = [x - _dot(b, y.astype(BF16)) for x, b, y in zip(xs, xb, ys)]
    return xs


def _gdn_kernel(pg_ref, pm_ref, cw_ref, nega_ref, dtb_ref, ng_ref, o_ref, s_ref, halo_ref,
                u_scr, wq_scr, a_scr, kd_scr, el_scr, *, tt):
    t = pl.program_id(1)
    heads = range(GDN_HEADS)

    @pl.when(t == 0)
    def _():
        s_ref[...] = jnp.zeros_like(s_ref)
        halo_ref[...] = jnp.zeros_like(halo_ref)

    row, col = _tri_masks()
    incl = row >= col
    strict = row > col
    tril = incl.astype(F32)
    nega = nega_ref[...]
    dtb = dtb_ref[...]
    ng = ng_ref[...]

    def prepare(c, carry):
        r0 = pl.multiple_of(c * CHUNK, CHUNK)
        pmv = pm_ref[pl.ds(r0, CHUNK), :]
        beta_all = _sigmoid(pmv)
        g_all = nega * _softplus(pmv + dtb)
        gam = jnp.dot(tril, g_all, precision=HIGHEST, preferred_element_type=F32)
        gam_t = gam.T
        gam_last = gam[CHUNK - 1:CHUNK, :]
        e_gam = jnp.exp(gam)
        e_rest = jnp.exp(gam_last - gam)
        el_scr[pl.ds(pl.multiple_of(c * 8, 8), 8), :] = jnp.broadcast_to(jnp.exp(gam_last), (8, LANES))
        halo = _halo_rows(pg_ref, halo_ref, c, r0, GDN_CONV_CH)

        qs = [_conv_silu(pg_ref, r0, halo, h * GDN_DK, cw_ref, None) for h in heads]
        ks = [_conv_silu(pg_ref, r0, halo, GDN_QK + h * GDN_DK, cw_ref, None) for h in heads]
        qs = [q * (lax.rsqrt(jnp.sum(q * q, axis=-1, keepdims=True) + EPS) * (GDN_DK ** -0.5)) for q in qs]
        ks = [k * lax.rsqrt(jnp.sum(k * k, axis=-1, keepdims=True) + EPS) for k in ks]
        qkks = [_dot_nt(jnp.concatenate([q, k], axis=0).astype(BF16), k.astype(BF16))
                for q, k in zip(qs, ks)]
        bcols = [beta_all[:, LANE_BETA + h:LANE_BETA + h + 1] for h in heads]
        egcs = [e_gam[:, LANE_ALPHA + h:LANE_ALPHA + h + 1] for h in heads]
        decays = [jnp.exp(jnp.where(incl, gam[:, LANE_ALPHA + h:LANE_ALPHA + h + 1]
                                    - gam_t[LANE_ALPHA + h:LANE_ALPHA + h + 1, :], -jnp.inf)) for h in heads]
        nmats = [jnp.where(strict, b * qkk[CHUNK:2 * CHUNK, :] * d, 0.0)
                 for b, qkk, d in zip(bcols, qkks, decays)]
        for h in heads:
            a_scr[c * GDN_HEADS + h] = (qkks[h][0:CHUNK, :] * decays[h]).astype(BF16)
            kd_scr[c * GDN_HEADS + h] = (ks[h] * e_rest[:, LANE_ALPHA + h:LANE_ALPHA + h + 1]).astype(BF16)
        tinvs = _unit_lower_inverses(nmats, row, col)
        vs = [_conv_silu(pg_ref, r0, halo, 2 * GDN_QK + h * GDN_DV, cw_ref, None) for h in heads]
        rhss = [jnp.concatenate([v * b, k * (b * e)], axis=1).astype(BF16)
                for v, k, b, e in zip(vs, ks, bcols, egcs)]
        sols = [_dot(x.astype(BF16), r) for x, r in zip(tinvs, rhss)]
        for h in heads:
            u_scr[c * GDN_HEADS + h] = sols[h][:, 0:GDN_DV]
            wq_scr[c * GDN_HEADS + h] = jnp.concatenate(
                [sols[h][:, GDN_DV:], qs[h] * egcs[h]], axis=0).astype(BF16)
        return carry

    def scan(c, carry):
        r0 = pl.multiple_of(c * CHUNK, CHUNK)
        e_last = el_scr[pl.ds(pl.multiple_of(c * 8, 8), 8), :][0:1, :]
        states = [s_ref[h] for h in heads]
        rs = [_dot(wq_scr[c * GDN_HEADS + h], states[h].astype(BF16)) for h in heads]
        vns = [(u_scr[c * GDN_HEADS + h] - rs[h][0:CHUNK, :]).astype(BF16) for h in heads]
        os_ = [rs[h][CHUNK:, :] + _dot(a_scr[c * GDN_HEADS + h], vns[h]) for h in heads]
        for h in heads:
            s_ref[h] = (e_last[:, LANE_ALPHA + h:LANE_ALPHA + h + 1] * states[h]
                        + _dot_tn(kd_scr[c * GDN_HEADS + h], vns[h]))
        for h in heads:
            o = os_[h]
            z = pg_ref[pl.ds(r0, CHUNK), GDN_CONV_CH + h * GDN_DV:GDN_CONV_CH + (h + 1) * GDN_DV].astype(F32)
            on = o * lax.rsqrt(jnp.mean(o * o, axis=-1, keepdims=True) + EPS) * ng
            o_ref[pl.ds(r0, CHUNK), h * GDN_DV:(h + 1) * GDN_DV] = (on * _silu(z)).astype(BF16)
        return carry

    lax.fori_loop(0, tt // CHUNK, prepare, 0)
    lax.fori_loop(0, tt // CHUNK, scan, 0)
    halo_ref[...] = pg_ref[tt - 16:tt, 0:GDN_CONV_CH]


def _gdn(pg, pm, conv_w, a_log, dt_bias, norm_g, bsz, seq):
    tt = min(TT_SCAN, seq)
    nt = seq // tt
    nega = jnp.zeros((1, LANES), F32).at[0, LANE_ALPHA:LANE_ALPHA + GDN_HEADS].set(-jnp.exp(a_log))
    dtb = jnp.zeros((1, LANES), F32).at[0, LANE_ALPHA:LANE_ALPHA + GDN_HEADS].set(dt_bias)
    return pl.pallas_call(
        functools.partial(_gdn_kernel, tt=tt),
        grid=(bsz, nt),
        in_specs=[
            pl.BlockSpec((tt, PG_W), lambda b, t: (b * nt + t, 0)),
            pl.BlockSpec((tt, LANES), lambda b, t: (b * nt + t, 0)),
            pl.BlockSpec((CONV_K, GDN_CONV_CH), lambda b, t: (0, 0)),
            pl.BlockSpec((1, LANES), lambda b, t: (0, 0)),
            pl.BlockSpec((1, LANES), lambda b, t: (0, 0)),
            pl.BlockSpec((1, GDN_DV), lambda b, t: (0, 0)),
        ],
        out_specs=pl.BlockSpec((tt, GDN_V), lambda b, t: (b * nt + t, 0)),
        out_shape=jax.ShapeDtypeStruct((bsz * seq, GDN_V), BF16),
        scratch_shapes=[
            pltpu.VMEM((GDN_HEADS, GDN_DK, GDN_DV), F32),
            pltpu.VMEM((16, GDN_CONV_CH), BF16),
            pltpu.VMEM((tt // CHUNK * GDN_HEADS, CHUNK, GDN_DV), F32),
            pltpu.VMEM((tt // CHUNK * GDN_HEADS, 2 * CHUNK, GDN_DK), BF16),
            pltpu.VMEM((tt // CHUNK * GDN_HEADS, CHUNK, CHUNK), BF16),
            pltpu.VMEM((tt // CHUNK * GDN_HEADS, CHUNK, GDN_DK), BF16),
            pltpu.VMEM((tt // CHUNK * 8, LANES), F32),
        ],
        compiler_params=pltpu.CompilerParams(
            dimension_semantics=("parallel", "arbitrary"), vmem_limit_bytes=VMEM_LIMIT),
        name="gdn_scan",
    )(pg, pm, conv_w, nega, dtb, norm_g.reshape(1, GDN_DV))


def _ssd_kernel(ps_ref, pm_ref, cw_ref, cb_ref, nega_ref, dtb_ref, dsk_ref, ng_ref, o_ref,
                h_ref, halo_ref, *, tt):
    t = pl.program_id(1)

    @pl.when(t == 0)
    def _():
        h_ref[...] = jnp.zeros_like(h_ref)
        halo_ref[...] = jnp.zeros_like(halo_ref)

    row, col = _tri_masks()
    incl = row >= col
    tril = incl.astype(F32)
    nega = nega_ref[...]
    dtb = dtb_ref[...]
    dsk = dsk_ref[...]
    lane = lax.broadcasted_iota(jnp.int32, (CHUNK, LANES), 1)
    lo_half = lane < SSD_P
    lane1 = lax.broadcasted_iota(jnp.int32, (1, LANES), 1)
    lo_half1 = lane1 < SSD_P
    heads_per_group = SSD_HEADS // SSD_G
    gw = SSD_INNER // SSD_G

    def pair_cols(arr, l0):
        sel = lo_half if arr.shape[0] == CHUNK else lo_half1
        return jnp.where(sel, arr[:, l0:l0 + 1], arr[:, l0 + 1:l0 + 2])

    def chunk(c, carry):
        r0 = pl.multiple_of(c * CHUNK, CHUNK)
        pmv = pm_ref[pl.ds(r0, CHUNK), :]
        dt_all = _softplus(pmv + dtb)
        a_all = nega * dt_all
        acum = jnp.dot(tril, a_all, precision=HIGHEST, preferred_element_type=F32)
        acum_t = acum.T
        a_last = acum[CHUNK - 1:CHUNK, :]
        e_a = jnp.exp(acum)
        e_rest = jnp.exp(a_last - acum)
        e_last = jnp.exp(a_last)
        halo = _halo_rows(ps_ref, halo_ref, c, r0, PS_W)

        def conv(col0):
            return _conv_silu(ps_ref, r0, halo, SSD_INNER + col0, cw_ref, cb_ref)

        for g in range(SSD_G):
            bg = conv(SSD_INNER + g * SSD_N).astype(BF16)
            cg = conv(SSD_INNER + SSD_BC + g * SSD_N).astype(BF16)
            cbm = _dot_nt(cg, bg)
            hstate = h_ref[g]
            y_off = _dot(cg, hstate.astype(BF16))
            xw_parts = []
            scale_parts = []
            y_parts = []
            for p in range(heads_per_group // 2):
                head0 = g * heads_per_group + 2 * p
                l0 = LANE_DT + head0
                x_p = conv(head0 * SSD_P)
                xdt = x_p * pair_cols(dt_all, l0)
                xdt_b = xdt.astype(BF16)
                res = []
                for j in range(2):
                    acol = acum[:, l0 + j:l0 + j + 1]
                    arow = acum_t[l0 + j:l0 + j + 1, :]
                    decay = jnp.exp(jnp.where(incl, acol - arow, -jnp.inf))
                    res.append(_dot((cbm * decay).astype(BF16), xdt_b))
                y_diag = jnp.where(lo_half, res[0], res[1])
                y_p = (y_diag + pair_cols(e_a, l0) * y_off[:, p * LANES:(p + 1) * LANES]
                       + pair_cols(dsk, l0) * x_p)
                y_parts.append(y_p)
                xw_parts.append((xdt * pair_cols(e_rest, l0)).astype(BF16))
                scale_parts.append(pair_cols(e_last, l0))
            xw = jnp.concatenate(xw_parts, axis=1)
            scale = jnp.concatenate(scale_parts, axis=1)
            h_ref[g] = scale * hstate + _dot_tn(bg, xw)
            y = jnp.concatenate(y_parts, axis=1)
            z = ps_ref[pl.ds(r0, CHUNK), g * gw:(g + 1) * gw].astype(F32)
            yz = y * _silu(z)
            yn = yz * lax.rsqrt(jnp.mean(yz * yz, axis=-1, keepdims=True) + EPS)
            o_ref[pl.ds(r0, CHUNK), g * gw:(g + 1) * gw] = (yn * ng_ref[:, g * gw:(g + 1) * gw]).astype(BF16)
        return carry

    lax.fori_loop(0, tt // CHUNK, chunk, 0)
    halo_ref[...] = ps_ref[tt - 16:tt, :]


def _ssd(ps, pm, conv_w, conv_b, a_log, dt_bias, d_skip, norm_g, bsz, seq):
    tt = min(TT_SCAN, seq)
    nt = seq // tt

    def lanes(v):
        return jnp.zeros((1, LANES), F32).at[0, LANE_DT:LANE_DT + SSD_HEADS].set(v)

    cw = jnp.concatenate([jnp.zeros((CONV_K, SSD_INNER), F32), conv_w], axis=1)
    cb = jnp.concatenate([jnp.zeros((1, SSD_INNER), F32), conv_b.reshape(1, -1)], axis=1)
    return pl.pallas_call(
        functools.partial(_ssd_kernel, tt=tt),
        grid=(bsz, nt),
        in_specs=[
            pl.BlockSpec((tt, PS_W), lambda b, t: (b * nt + t, 0)),
            pl.BlockSpec((tt, LANES), lambda b, t: (b * nt + t, 0)),
            pl.BlockSpec((CONV_K, PS_W), lambda b, t: (0, 0)),
            pl.BlockSpec((1, PS_W), lambda b, t: (0, 0)),
            pl.BlockSpec((1, LANES), lambda b, t: (0, 0)),
            pl.BlockSpec((1, LANES), lambda b, t: (0, 0)),
            pl.BlockSpec((1, LANES), lambda b, t: (0, 0)),
            pl.BlockSpec((1, SSD_INNER), lambda b, t: (0, 0)),
        ],
        out_specs=pl.BlockSpec((tt, SSD_INNER), lambda b, t: (b * nt + t, 0)),
        out_shape=jax.ShapeDtypeStruct((bsz * seq, SSD_INNER), BF16),
        scratch_shapes=[
            pltpu.VMEM((SSD_G, SSD_N, SSD_INNER // SSD_G), F32),
            pltpu.VMEM((16, PS_W), BF16),
        ],
        compiler_params=pltpu.CompilerParams(
            dimension_semantics=("parallel", "arbitrary"), vmem_limit_bytes=VMEM_LIMIT),
        name="ssd_scan",
    )(ps, pm, cw, cb, lanes(-jnp.exp(a_log)), lanes(dt_bias), lanes(d_skip), norm_g.reshape(1, SSD_INNER))


RT_IDX = 0
RT_GATE = TOP_K
RT_POS = 2 * TOP_K


def _outproj_router_kernel(go_ref, so_ref, x_ref, mod_ref, g_ref, wo_ref, rw_ref, rb_ref,
                           xn_ref, h_ref, rt_ref, cnt_ref, run_ref, *, tm):
    i = pl.program_id(0)

    @pl.when(i == 0)
    def _():
        run_ref[...] = jnp.zeros_like(run_ref)

    mix = _dot(go_ref[...], wo_ref[0:GDN_V, :]) + _dot(so_ref[...], wo_ref[GDN_V:, :])
    xn = x_ref[...] + mod_ref[2:3, :] * mix
    xn_ref[...] = xn
    h = _modulated_norm(xn, g_ref[...], mod_ref[3:4, :], mod_ref[4:5, :])
    h_ref[...] = h
    logits = jnp.dot(h, rw_ref[...], precision=HIGHEST, preferred_element_type=F32) + rb_ref[...]
    lane = lax.broadcasted_iota(jnp.int32, (tm, LANES), 1).astype(F32)
    work = logits
    tops = []
    idxs = []
    for _ in range(TOP_K):
        m = jnp.max(work, axis=-1, keepdims=True)
        idx = jnp.min(jnp.where(work == m, lane, float(LANES)), axis=-1, keepdims=True)
        work = jnp.where(lane == idx, -jnp.inf, work)
        tops.append(m)
        idxs.append(idx)
    exps = [jnp.exp(m - tops[0]) for m in tops]
    denom = exps[0] + exps[1] + exps[2] + exps[3]
    onehot = jnp.zeros((tm, LANES), F32)
    for idx in idxs:
        onehot = onehot + (lane == idx).astype(F32)
    trow = lax.broadcasted_iota(jnp.int32, (tm, tm), 0)
    tcol = lax.broadcasted_iota(jnp.int32, (tm, tm), 1)
    before = _dot((trow > tcol).astype(BF16), onehot.astype(BF16)) + run_ref[0:1, :]
    rt = jnp.zeros((tm, LANES), F32)
    for k in range(TOP_K):
        pos = jnp.sum(jnp.where(lane == idxs[k], before, 0.0), axis=-1, keepdims=True)
        rt = jnp.where(lane == RT_IDX + k, idxs[k], rt)
        rt = jnp.where(lane == RT_GATE + k, exps[k] / denom, rt)
        rt = jnp.where(lane == RT_POS + k, pos, rt)
    rt_ref[...] = rt
    run = run_ref[0:1, :] + jnp.sum(onehot, axis=0, keepdims=True)
    run_ref[...] = jnp.broadcast_to(run, run_ref.shape)
    cnt_ref[...] = jnp.broadcast_to(run, cnt_ref.shape)


def _outproj_router(go, so, x2, mod_l, g, w_out, rw, rb, seq):
    n_tok = x2.shape[0]
    tm = min(TM_PROJ, seq)
    return pl.pallas_call(
        functools.partial(_outproj_router_kernel, tm=tm),
        grid=(n_tok // tm,),
        in_specs=[
            pl.BlockSpec((tm, GDN_V), lambda i: (i, 0)),
            pl.BlockSpec((tm, SSD_INNER), lambda i: (i, 0)),
            pl.BlockSpec((tm, D_MODEL), lambda i: (i, 0)),
            pl.BlockSpec((None, 8, D_MODEL), lambda i: ((i * tm) // seq, 0, 0)),
            pl.BlockSpec((1, D_MODEL), lambda i: (0, 0)),
            pl.BlockSpec((GDN_V + SSD_INNER, D_MODEL), lambda i: (0, 0)),
            pl.BlockSpec((D_MODEL, LANES), lambda i: (0, 0)),
            pl.BlockSpec((1, LANES), lambda i: (0, 0)),
        ],
        out_specs=[
            pl.BlockSpec((tm, D_MODEL), lambda i: (i, 0)),
            pl.BlockSpec((tm, D_MODEL), lambda i: (i, 0)),
            pl.BlockSpec((tm, LANES), lambda i: (i, 0)),
            pl.BlockSpec((8, LANES), lambda i: (0, 0)),
        ],
        out_shape=[
            jax.ShapeDtypeStruct((n_tok, D_MODEL), F32),
            jax.ShapeDtypeStruct((n_tok, D_MODEL), F32),
            jax.ShapeDtypeStruct((n_tok, LANES), F32),
            jax.ShapeDtypeStruct((8, LANES), F32),
        ],
        scratch_shapes=[pltpu.VMEM((8, LANES), F32)],
        compiler_params=pltpu.CompilerParams(
            dimension_semantics=("arbitrary",), vmem_limit_bytes=VMEM_LIMIT),
        name="outproj_router",
    )(go, so, x2, mod_l, g, w_out, rw, rb)


def _sc_workers():
    info = plsc.get_sparse_core_info()
    return info.num_cores, info.num_subcores


def _scatter_rows(src, dest_t, n_rows):
    n_tok, width = src.shape
    n_k = dest_t.shape[0]
    nc, ns = _sc_workers()
    per_w = n_tok // (nc * ns)
    win = min(SC_WINDOW, per_w)
    mesh = plsc.VectorSubcoreMesh(core_axis_name="c", subcore_axis_name="s")

    @functools.partial(
        pl.kernel, mesh=mesh,
        out_type=jax.ShapeDtypeStruct((n_rows, width), src.dtype),
        scratch_types=[pltpu.VMEM((win,), jnp.int32), pltpu.VMEM((win, width), src.dtype)],
    )
    def scatter_kernel(src_hbm, idx_hbm, out_hbm, idx_v, rows_v):
        wid = lax.axis_index("s") * nc + lax.axis_index("c")
        base = wid * per_w

        @pl.loop(0, per_w // win)
        def _(j):
            off = base + j * win
            pltpu.sync_copy(src_hbm.at[pl.ds(off, win)], rows_v)
            for k in range(n_k):
                pltpu.sync_copy(idx_hbm.at[k, pl.ds(off, win)], idx_v)
                pltpu.sync_copy(rows_v, out_hbm.at[idx_v])

    return scatter_kernel(src, dest_t)


def _gather_rows(table, idx):
    n_idx = idx.shape[0]
    width = table.shape[1]
    nc, ns = _sc_workers()
    per_w = n_idx // (nc * ns)
    win = min(SC_WINDOW, per_w)
    mesh = plsc.VectorSubcoreMesh(core_axis_name="c", subcore_axis_name="s")

    @functools.partial(
        pl.kernel, mesh=mesh,
        out_type=jax.ShapeDtypeStruct((n_idx, width), table.dtype),
        scratch_types=[pltpu.VMEM((win,), jnp.int32), pltpu.VMEM((win, width), table.dtype)],
    )
    def gather_kernel(table_hbm, idx_hbm, out_hbm, idx_v, rows_v):
        wid = lax.axis_index("s") * nc + lax.axis_index("c")
        base = wid * per_w

        @pl.loop(0, per_w // win)
        def _(j):
            off = base + j * win
            pltpu.sync_copy(idx_hbm.at[pl.ds(off, win)], idx_v)
            pltpu.sync_copy(table_hbm.at[idx_v], rows_v)
            pltpu.sync_copy(rows_v, out_hbm.at[pl.ds(off, win)])

    return gather_kernel(table, idx)


def _expert_kernel(be_ref, bv_ref, x_ref, wgu_ref, bgu_ref, wd_ref, bd_ref, y_ref, wgu_b, wd_b, *, tr):
    i = pl.program_id(0)
    e = be_ref[i]
    prev = be_ref[jnp.maximum(i - 1, 0)]

    @pl.when((i == 0) | (e != prev))
    def _():
        wgu_b[...] = wgu_ref[...].astype(BF16)
        wd_b[...] = wd_ref[...].astype(BF16)

    valid = bv_ref[i]

    @pl.when(valid == 0)
    def _():
        y_ref[...] = jnp.zeros_like(y_ref)

    @pl.when(valid > 0)
    def _():
        rows = lax.broadcasted_iota(jnp.int32, (tr, D_MODEL), 0)
        x = jnp.where(rows < valid, x_ref[...], 0.0).astype(BF16)
        acc = jnp.zeros((tr, D_MODEL), F32)
        for f in range(0, D_FF, FF_CHUNK):
            gate = _dot(x, wgu_b[:, f:f + FF_CHUNK]) + bgu_ref[:, f:f + FF_CHUNK]
            up = _dot(x, wgu_b[:, D_FF + f:D_FF + f + FF_CHUNK]) + bgu_ref[:, D_FF + f:D_FF + f + FF_CHUNK]
            gate = jnp.minimum(gate, SWIGLU_LIMIT)
            up = jnp.clip(up, -SWIGLU_LIMIT, SWIGLU_LIMIT)
            act = (up + 1.0) * (gate * _sigmoid(gate * SWIGLU_ALPHA))
            acc = acc + _dot(act.astype(BF16), wd_b[f:f + FF_CHUNK, :])
        y_ref[...] = acc + bd_ref[...]


def _experts(x_rows, blk_expert, blk_valid, w_gu, b_gu, w_down, b_down, layer):
    n_rows = x_rows.shape[0]
    depth = w_gu.shape[0]
    tr = ROW_BLOCK
    nb = n_rows // tr
    grid_spec = pltpu.PrefetchScalarGridSpec(
        num_scalar_prefetch=2,
        grid=(nb,),
        in_specs=[
            pl.BlockSpec((tr, D_MODEL), lambda i, be, bv: (i, 0)),
            pl.BlockSpec((None, None, D_MODEL, 2 * D_FF), lambda i, be, bv: (layer, be[i], 0, 0)),
            pl.BlockSpec((None, None, 1, 2 * D_FF), lambda i, be, bv: (layer, be[i], 0, 0)),
            pl.BlockSpec((None, None, D_FF, D_MODEL), lambda i, be, bv: (layer, be[i], 0, 0)),
            pl.BlockSpec((None, None, 1, D_MODEL), lambda i, be, bv: (layer, be[i], 0, 0)),
        ],
        out_specs=pl.BlockSpec((tr, D_MODEL), lambda i, be, bv: (i, 0)),
        scratch_shapes=[pltpu.VMEM((D_MODEL, 2 * D_FF), BF16), pltpu.VMEM((D_FF, D_MODEL), BF16)],
    )
    return pl.pallas_call(
        functools.partial(_expert_kernel, tr=tr),
        grid_spec=grid_spec,
        out_shape=jax.ShapeDtypeStruct((n_rows, D_MODEL), F32),
        compiler_params=pltpu.CompilerParams(
            dimension_semantics=("arbitrary",), vmem_limit_bytes=VMEM_LIMIT),
        name="moe_experts",
    )(blk_expert, blk_valid, x_rows, w_gu, b_gu.reshape(depth, N_EXPERTS, 1, 2 * D_FF), w_down,
      b_down.reshape(depth, N_EXPERTS, 1, D_MODEL))


def _combine_kernel(xn_ref, y4_ref, rt_ref, mod_ref, g_ref, o_ref, *, final):
    rt = rt_ref[...]
    acc = rt[:, RT_GATE:RT_GATE + 1] * y4_ref[0]
    for k in range(1, TOP_K):
        acc = acc + rt[:, RT_GATE + k:RT_GATE + k + 1] * y4_ref[k]
    x = xn_ref[...] + mod_ref[5:6, :] * acc
    if final:
        ms = jnp.mean(x * x, axis=-1, keepdims=True)
        x = x * lax.rsqrt(ms + EPS) * g_ref[...]
    o_ref[...] = x


def _combine(xn, y4, rt, mod_l, final_g, seq, final):
    n_tok = xn.shape[0]
    tm = min(TM_PROJ, seq)
    return pl.pallas_call(
        functools.partial(_combine_kernel, final=final),
        grid=(n_tok // tm,),
        in_specs=[
            pl.BlockSpec((tm, D_MODEL), lambda i: (i, 0)),
            pl.BlockSpec((TOP_K, tm, D_MODEL), lambda i: (0, i, 0)),
            pl.BlockSpec((tm, LANES), lambda i: (i, 0)),
            pl.BlockSpec((None, 8, D_MODEL), lambda i: ((i * tm) // seq, 0, 0)),
            pl.BlockSpec((1, D_MODEL), lambda i: (0, 0)),
        ],
        out_specs=pl.BlockSpec((tm, D_MODEL), lambda i: (i, 0)),
        out_shape=jax.ShapeDtypeStruct((n_tok, D_MODEL), F32),
        compiler_params=pltpu.CompilerParams(
            dimension_semantics=("parallel",), vmem_limit_bytes=VMEM_LIMIT),
        name="moe_combine",
    )(xn, y4, rt, mod_l, final_g)


def _permute_w_in(w_in):
    off = 0
    gdn_qkvz = w_in[:, off:off + PG_W]; off += PG_W
    small_ba = w_in[:, off:off + 2 * GDN_HEADS]; off += 2 * GDN_HEADS
    ssd_zxbc = w_in[:, off:off + PS_W]; off += PS_W
    ssd_dt = w_in[:, off:off + SSD_HEADS]
    pad = jnp.zeros((w_in.shape[0], LANES - 2 * GDN_HEADS - SSD_HEADS), w_in.dtype)
    return jnp.concatenate([gdn_qkvz, ssd_zxbc, small_ba, ssd_dt, pad], axis=1).astype(BF16)


def _moe_schedule(rt, counts_row, n_rows):
    idx = rt[:, RT_IDX:RT_IDX + TOP_K].astype(jnp.int32)
    pos = rt[:, RT_POS:RT_POS + TOP_K].astype(jnp.int32)
    counts = counts_row[0, :N_EXPERTS].astype(jnp.int32)
    padded = (counts + ROW_BLOCK - 1) // ROW_BLOCK * ROW_BLOCK
    pad_end = jnp.cumsum(padded)
    pad_start = pad_end - padded
    experts = jnp.arange(N_EXPERTS, dtype=jnp.int32)
    dest = pos + jnp.sum(jnp.where(idx[..., None] == experts, pad_start, 0), axis=-1)
    blk_start = jnp.arange(n_rows // ROW_BLOCK, dtype=jnp.int32) * ROW_BLOCK
    blk_expert = jnp.minimum(jnp.sum(blk_start[:, None] >= pad_end[None, :], axis=1), N_EXPERTS - 1)
    blk_onehot = blk_expert[:, None] == experts
    blk_valid = (jnp.sum(jnp.where(blk_onehot, counts + pad_start, 0), axis=1) - blk_start)
    blk_valid = jnp.where(blk_start < pad_end[-1], jnp.clip(blk_valid, 0, ROW_BLOCK), 0)
    return dest, blk_expert.astype(jnp.int32), blk_valid.astype(jnp.int32)


def _forward(x, c, ada_w, ada_b, norm1_g, norm2_g, w_in, gdn_conv_w, gdn_a_log, gdn_dt_bias, gdn_norm_g,
             ssd_conv_w, ssd_conv_b, ssd_a_log, ssd_dt_bias, ssd_d, ssd_norm_g, w_out, router_w, router_b,
             moe_w_gu, moe_b_gu, moe_w_down, moe_b_down, final_g):
    bsz, seq, _ = x.shape
    depth = ada_w.shape[0]
    n_tok = bsz * seq
    n_rows = n_tok * TOP_K + N_EXPERTS * ROW_BLOCK
    mod = _ada_mod(c, ada_w, ada_b)
    x2 = x.reshape(n_tok, D_MODEL)
    fg = final_g.reshape(1, D_MODEL)
    for l in range(depth):
        pg, ps, pm = _inproj(x2, mod[l], norm1_g[l].reshape(1, D_MODEL), _permute_w_in(w_in[l]), seq)
        go = _gdn(pg, pm, gdn_conv_w[l], gdn_a_log[l], gdn_dt_bias[l], gdn_norm_g[l], bsz, seq)
        so = _ssd(ps, pm, ssd_conv_w[l], ssd_conv_b[l], ssd_a_log[l], ssd_dt_bias[l], ssd_d[l],
                  ssd_norm_g[l], bsz, seq)
        rw = jnp.zeros((D_MODEL, LANES), F32).at[:, :N_EXPERTS].set(router_w[l])
        rb = jnp.full((1, LANES), -1e30, F32).at[0, :N_EXPERTS].set(router_b[l])
        xn, h2, rt, cnt = _outproj_router(go, so, x2, mod[l], norm2_g[l].reshape(1, D_MODEL),
                                          w_out[l].astype(BF16), rw, rb, seq)
        dest, blk_expert, blk_valid = _moe_schedule(rt, cnt, n_rows)
        dest_t = dest.T
        x_rows = _scatter_rows(h2, dest_t, n_rows)
        y_rows = _experts(x_rows, blk_expert, blk_valid, moe_w_gu, moe_b_gu, moe_w_down, moe_b_down, l)
        y4 = _gather_rows(y_rows, dest_t.reshape(TOP_K * n_tok)).reshape(TOP_K, n_tok, D_MODEL)
        x2 = _combine(xn, y4, rt, mod[l], fg, seq, final=(l == depth - 1))
    return x2.reshape(bsz, seq, D_MODEL)


def kernel(x, c, ada_w, ada_b, norm1_g, norm2_g, w_in, gdn_conv_w, gdn_a_log, gdn_dt_bias, gdn_norm_g, ssd_conv_w, ssd_conv_b, ssd_a_log, ssd_dt_bias, ssd_d, ssd_norm_g, w_out, router_w, router_b, moe_w_gu, moe_b_gu, moe_w_down, moe_b_down, final_g):
    return _forward(x, c, ada_w, ada_b, norm1_g, norm2_g, w_in, gdn_conv_w, gdn_a_log, gdn_dt_bias, gdn_norm_g,
                    ssd_conv_w, ssd_conv_b, ssd_a_log, ssd_dt_bias, ssd_d, ssd_norm_g, w_out, router_w, router_b,
                    moe_w_gu, moe_b_gu, moe_w_down, moe_b_down, final_g)
```

```python
import functools

import jax
import jax.numpy as jnp
from jax import lax
from jax.experimental import pallas as pl
from jax.experimental.pallas import tpu as pltpu
from jax.experimental.pallas import tpu_sc as plsc

F32 = jnp.float32
BF16 = jnp.bfloat16
U32 = jnp.uint32
HIGHEST = lax.Precision.HIGHEST

D_MODEL = 1024
CHUNK = 64
CONV_K = 4
GDN_HEADS = 8
GDN_DK = 128
GDN_DV = 128
GDN_QK = GDN_HEADS * GDN_DK
GDN_V = GDN_HEADS * GDN_DV
GDN_CONV_CH = 2 * GDN_QK + GDN_V
SSD_P = 64
SSD_HEADS = 16
SSD_G = 2
SSD_N = 128
SSD_INNER = SSD_HEADS * SSD_P
SSD_BC = SSD_G * SSD_N
SSD_CONV_CH = SSD_INNER + 2 * SSD_BC
N_EXPERTS = 32
TOP_K = 4
D_FF = D_MODEL
SWIGLU_ALPHA = 1.702
SWIGLU_LIMIT = 7.0
EPS = 1e-6

LANES = 128
SUBLANES = 8
PG_W = GDN_CONV_CH + GDN_V
PS_W = SSD_INNER + SSD_CONV_CH
PS_X = SSD_INNER
PS_B = 2 * SSD_INNER
PS_C = 2 * SSD_INNER + SSD_BC
N_CONV = GDN_CONV_CH + SSD_CONV_CH
LANE_BETA = 0
LANE_ALPHA = GDN_HEADS
LANE_DT = 2 * GDN_HEADS
VMEM_LIMIT = 56 * 1024 * 1024

TM_PROJ = 256
TM_ROUTER = 512
TT_SCAN = 512
GDN_PREP_CHUNKS = 2
ROW_BLOCK = 512
FF_CHUNK = 512
CONV_SLAB = 512
SC_WINDOW = 128
ROW_WORDS = D_MODEL // 2


def _dot(a, b):
    return jnp.dot(a, b, preferred_element_type=F32)


def _dot_nt(a, b):
    return lax.dot_general(a, b, (((1,), (1,)), ((), ())), preferred_element_type=F32)


def _dot_tn(a, b):
    return lax.dot_general(a, b, (((0,), (0,)), ((), ())), preferred_element_type=F32)


def _sigmoid(x):
    return 1.0 / (1.0 + jnp.exp(-x))


def _silu(x):
    return x * _sigmoid(x)


def _softplus(x):
    return jnp.maximum(x, 0.0) + jnp.log(1.0 + jnp.exp(-jnp.abs(x)))


def _pack_rows(h):
    w = h.shape[1] // 2
    hi = pltpu.bitcast(h[:, :w].astype(BF16).astype(F32), U32)
    lo = pltpu.bitcast(h[:, w:].astype(BF16).astype(F32), U32)
    return hi | (lo >> 16)


def _unpack_rows(p):
    hi = pltpu.bitcast(p & jnp.uint32(0xFFFF0000), F32)
    lo = pltpu.bitcast(p << 16, F32)
    return jnp.concatenate([hi, lo], axis=1)


def _ada_kernel(c_ref, w_ref, b_ref, o_ref):
    c = c_ref[...]
    o_ref[0] = jnp.dot(_silu(c), w_ref[0], precision=HIGHEST, preferred_element_type=F32) + b_ref[0]


def _ada_mod(c, ada_w, ada_b):
    depth = ada_w.shape[0]
    bsz = c.shape[0]
    c8 = jnp.zeros((SUBLANES, D_MODEL), F32).at[:bsz].set(c)
    out = pl.pallas_call(
        _ada_kernel,
        grid=(depth, 6),
        in_specs=[
            pl.BlockSpec((SUBLANES, D_MODEL), lambda l, j: (0, 0)),
            pl.BlockSpec((1, D_MODEL, D_MODEL), lambda l, j: (l, 0, j)),
            pl.BlockSpec((1, 1, D_MODEL), lambda l, j: (l, 0, j)),
        ],
        out_specs=pl.BlockSpec((1, SUBLANES, D_MODEL), lambda l, j: (l, 0, j)),
        out_shape=jax.ShapeDtypeStruct((depth, SUBLANES, 6 * D_MODEL), F32),
        name="ada_mod",
    )(c8, ada_w, ada_b.reshape(depth, 1, 6 * D_MODEL))
    mod = out[:, :bsz].reshape(depth, bsz, 6, D_MODEL)
    return jnp.concatenate([mod, jnp.zeros((depth, bsz, 2, D_MODEL), F32)], axis=2)


def _modulated_norm(x, g, shift, scale):
    ms = jnp.mean(x * x, axis=-1, keepdims=True)
    return (x * lax.rsqrt(ms + EPS) * g) * (1.0 + scale) + shift


def _inproj_kernel(x_ref, mod_ref, g_ref, w_ref, cw_ref, cb_ref, og_ref, os_ref, om_ref,
                   tail_ref, cbuf_ref, *, tm, seq):
    i = pl.program_id(0)

    @pl.when((i * tm) % seq == 0)
    def _():
        tail_ref[...] = jnp.zeros_like(tail_ref)

    h = _modulated_norm(x_ref[...], g_ref[...], mod_ref[0:1, :], mod_ref[1:2, :]).astype(BF16)

    def conv_slab(n, wcol, ccol):
        buf = cbuf_ref.at[n % 2]
        p = _dot(h, w_ref[:, wcol:wcol + CONV_SLAB])
        buf[0:SUBLANES, :] = tail_ref[:, ccol:ccol + CONV_SLAB]
        buf[SUBLANES:SUBLANES + tm, :] = p
        tail_ref[:, ccol:ccol + CONV_SLAB] = p[tm - SUBLANES:tm, :]
        y = p * cw_ref[CONV_K - 1:CONV_K, ccol:ccol + CONV_SLAB] + cb_ref[:, ccol:ccol + CONV_SLAB]
        for s in range(1, CONV_K):
            y = y + (buf[SUBLANES - s:SUBLANES - s + tm, :]
                     * cw_ref[CONV_K - 1 - s:CONV_K - s, ccol:ccol + CONV_SLAB])
        return _silu(y)

    n = 0
    for part, scale in ((0, GDN_DK ** -0.5), (1, 1.0)):
        for j in range(GDN_QK // CONV_SLAB):
            col = part * GDN_QK + j * CONV_SLAB
            y = conv_slab(n, col, col)
            n += 1
            for hh in range(CONV_SLAB // GDN_DK):
                yh = y[:, hh * GDN_DK:(hh + 1) * GDN_DK]
                inv = lax.rsqrt(jnp.sum(yh * yh, axis=-1, keepdims=True) + EPS) * scale
                og_ref[:, col + hh * GDN_DK:col + (hh + 1) * GDN_DK] = (yh * inv).astype(BF16)
    for j in range(GDN_V // CONV_SLAB):
        col = 2 * GDN_QK + j * CONV_SLAB
        og_ref[:, col:col + CONV_SLAB] = conv_slab(n, col, col).astype(BF16)
        n += 1
    og_ref[:, GDN_CONV_CH:PG_W] = _dot(h, w_ref[:, GDN_CONV_CH:PG_W]).astype(BF16)
    os_ref[:, 0:SSD_INNER] = _dot(h, w_ref[:, PG_W:PG_W + SSD_INNER]).astype(BF16)
    for j in range(SSD_CONV_CH // CONV_SLAB):
        col = j * CONV_SLAB
        os_ref[:, SSD_INNER + col:SSD_INNER + col + CONV_SLAB] = conv_slab(
            n, PG_W + SSD_INNER + col, GDN_CONV_CH + col).astype(BF16)
        n += 1
    om_ref[...] = _dot(h, w_ref[:, PG_W + PS_W:])


def _inproj(x2, mod_l, g, w_perm, conv_w, conv_b, seq):
    n_tok = x2.shape[0]
    tm = min(TM_PROJ, seq)
    wtot = w_perm.shape[1]
    return pl.pallas_call(
        functools.partial(_inproj_kernel, tm=tm, seq=seq),
        grid=(n_tok // tm,),
        in_specs=[
            pl.BlockSpec((tm, D_MODEL), lambda i: (i, 0)),
            pl.BlockSpec((None, SUBLANES, D_MODEL), lambda i: ((i * tm) // seq, 0, 0)),
            pl.BlockSpec((1, D_MODEL), lambda i: (0, 0)),
            pl.BlockSpec((D_MODEL, wtot), lambda i: (0, 0)),
            pl.BlockSpec((CONV_K, N_CONV), lambda i: (0, 0)),
            pl.BlockSpec((1, N_CONV), lambda i: (0, 0)),
        ],
        out_specs=[
            pl.BlockSpec((tm, PG_W), lambda i: (i, 0)),
            pl.BlockSpec((tm, PS_W), lambda i: (i, 0)),
            pl.BlockSpec((tm, LANES), lambda i: (i, 0)),
        ],
        out_shape=[
            jax.ShapeDtypeStruct((n_tok, PG_W), BF16),
            jax.ShapeDtypeStruct((n_tok, PS_W), BF16),
            jax.ShapeDtypeStruct((n_tok, LANES), F32),
        ],
        scratch_shapes=[
            pltpu.VMEM((SUBLANES, N_CONV), F32),
            pltpu.VMEM((2, SUBLANES + tm, CONV_SLAB), F32),
        ],
        compiler_params=pltpu.CompilerParams(
            dimension_semantics=("arbitrary",), vmem_limit_bytes=VMEM_LIMIT),
        name="norm_inproj",
    )(x2, mod_l, g, w_perm, conv_w, conv_b)


def _tri_masks():
    row = lax.broadcasted_iota(jnp.int32, (CHUNK, CHUNK), 0)
    col = lax.broadcasted_iota(jnp.int32, (CHUNK, CHUNK), 1)
    return row, col


def _unit_lower_inverses(nmats, row, col):
    eye = (row == col).astype(F32)
    pair = (row >> 1) == (col >> 1)
    xs = [eye - jnp.where(pair, n, 0.0) for n in nmats]
    for lg in range(2, CHUNK.bit_length()):
        mask = ((row >> lg) == (col >> lg)) & ((row >> (lg - 1)) != (col >> (lg - 1)))
        xb = [x.astype(BF16) for x in xs]
        ys = [_dot(jnp.where(mask, n, 0.0).astype(BF16), b) for n, b in zip(nmats, xb)]
        xs = [x - _dot(b, y.astype(BF16)) for x, b, y in zip(xs, xb, ys)]
    return xs


def _gdn_kernel(pg_ref, pm_ref, nega_ref, dtb_ref, ng_ref, o_ref, s_ref,
                gq_scr, b_scr, o0_scr, el_scr, *, tt):
    t = pl.program_id(1)
    heads = range(GDN_HEADS)

    @pl.when(t == 0)
    def _():
        s_ref[...] = jnp.zeros_like(s_ref)

    row, col = _tri_masks()
    incl = row >= col
    strict = row > col
    tril = incl.astype(F32)
    nega = nega_ref[...]
    dtb = dtb_ref[...]
    ng = ng_ref[...]

    def prepare(ci, carry):
        units = []
        for j in range(GDN_PREP_CHUNKS):
            c = ci * GDN_PREP_CHUNKS + j
            rows = pl.ds(pl.multiple_of(c * CHUNK, CHUNK), CHUNK)
            pmv = pm_ref[rows, :]
            beta_all = _sigmoid(pmv)
            g_all = nega * _softplus(pmv + dtb)
            gam = jnp.dot(tril, g_all, precision=HIGHEST, preferred_element_type=F32)
            gam_t = gam.T
            gam_last = gam[CHUNK - 1:CHUNK, :]
            e_gam = jnp.exp(gam)
            e_rest = jnp.exp(gam_last - gam)
            el_scr[pl.ds(pl.multiple_of(c * SUBLANES, SUBLANES), SUBLANES), :] = jnp.broadcast_to(
                jnp.exp(gam_last), (SUBLANES, LANES))
            for h in heads:
                la = LANE_ALPHA + h
                units.append(dict(
                    slot=c * GDN_HEADS + h, rows=rows, h=h,
                    beta=beta_all[:, LANE_BETA + h:LANE_BETA + h + 1],
                    gcol=gam[:, la:la + 1], grow=gam_t[la:la + 1, :],
                    egc=e_gam[:, la:la + 1], erc=e_rest[:, la:la + 1]))

        qs = [pg_ref[u["rows"], u["h"] * GDN_DK:(u["h"] + 1) * GDN_DK] for u in units]
        ks = [pg_ref[u["rows"], GDN_QK + u["h"] * GDN_DK:GDN_QK + (u["h"] + 1) * GDN_DK] for u in units]
        qkks = [_dot_nt(jnp.concatenate([q, k], axis=0), k) for q, k in zip(qs, ks)]
        decays = [jnp.exp(jnp.where(incl, u["gcol"] - u["grow"], -jnp.inf)) for u in units]
        nmats = [jnp.where(strict, u["beta"] * qkk[CHUNK:2 * CHUNK, :] * d, 0.0)
                 for u, qkk, d in zip(units, qkks, decays)]
        amats = [(qkk[0:CHUNK, :] * d).astype(BF16) for qkk, d in zip(qkks, decays)]
        kfs = [k.astype(F32) for k in ks]
        kds = [(kf * u["erc"]).astype(BF16) for kf, u in zip(kfs, units)]
        tinvs = _unit_lower_inverses(nmats, row, col)
        vs = [pg_ref[u["rows"], 2 * GDN_QK + u["h"] * GDN_DV:2 * GDN_QK + (u["h"] + 1) * GDN_DV].astype(F32)
              for u in units]
        rhss = [jnp.concatenate([v * u["beta"], kf * (u["beta"] * u["egc"])], axis=1).astype(BF16)
                for v, kf, u in zip(vs, kfs, units)]
        sols = [_dot(x.astype(BF16), r).astype(BF16) for x, r in zip(tinvs, rhss)]
        a_uw = [_dot(a, s) for a, s in zip(amats, sols)]
        k_uw = [_dot_tn(kd, s) for kd, s in zip(kds, sols)]
        for u, q, au, ku in zip(units, qs, a_uw, k_uw):
            o0_scr[u["slot"]] = au[:, 0:GDN_DV]
            b_scr[u["slot"]] = ku[:, 0:GDN_DV]
            gq_scr[u["slot"]] = jnp.concatenate(
                [ku[:, GDN_DV:], q.astype(F32) * u["egc"] - au[:, GDN_DV:]], axis=0).astype(BF16)
        return carry

    def scan(c, carry):
        r0 = pl.multiple_of(c * CHUNK, CHUNK)
        e_last = el_scr[pl.ds(pl.multiple_of(c * SUBLANES, SUBLANES), SUBLANES), :][0:1, :]
        states = [s_ref[h] for h in heads]
        rs = [_dot(gq_scr[c * GDN_HEADS + h], states[h].astype(BF16)) for h in heads]
        for h in heads:
            s_ref[h] = (e_last[:, LANE_ALPHA + h:LANE_ALPHA + h + 1] * states[h]
                        - rs[h][0:GDN_DK, :] + b_scr[c * GDN_HEADS + h])
        for h in heads:
            o = rs[h][GDN_DK:, :] + o0_scr[c * GDN_HEADS + h]
            z = pg_ref[pl.ds(r0, CHUNK), GDN_CONV_CH + h * GDN_DV:GDN_CONV_CH + (h + 1) * GDN_DV].astype(F32)
            on = o * lax.rsqrt(jnp.mean(o * o, axis=-1, keepdims=True) + EPS) * ng
            o_ref[pl.ds(r0, CHUNK), h * GDN_DV:(h + 1) * GDN_DV] = (on * _silu(z)).astype(BF16)
        return carry

    lax.fori_loop(0, tt // CHUNK // GDN_PREP_CHUNKS, prepare, 0)
    lax.fori_loop(0, tt // CHUNK, scan, 0, unroll=2)


def _gdn(pg, pm, a_log, dt_bias, norm_g, bsz, seq):
    tt = min(TT_SCAN, seq)
    nt = seq // tt
    nega = jnp.zeros((1, LANES), F32).at[0, LANE_ALPHA:LANE_ALPHA + GDN_HEADS].set(-jnp.exp(a_log))
    dtb = jnp.zeros((1, LANES), F32).at[0, LANE_ALPHA:LANE_ALPHA + GDN_HEADS].set(dt_bias)
    units = tt // CHUNK * GDN_HEADS
    return pl.pallas_call(
        functools.partial(_gdn_kernel, tt=tt),
        grid=(bsz, nt),
        in_specs=[
            pl.BlockSpec((tt, PG_W), lambda b, t: (b * nt + t, 0)),
            pl.BlockSpec((tt, LANES), lambda b, t: (b * nt + t, 0)),
            pl.BlockSpec((1, LANES), lambda b, t: (0, 0)),
            pl.BlockSpec((1, LANES), lambda b, t: (0, 0)),
            pl.BlockSpec((1, GDN_DV), lambda b, t: (0, 0)),
        ],
        out_specs=pl.BlockSpec((tt, GDN_V), lambda b, t: (b * nt + t, 0)),
        out_shape=jax.ShapeDtypeStruct((bsz * seq, GDN_V), BF16),
        scratch_shapes=[
            pltpu.VMEM((GDN_HEADS, GDN_DK, GDN_DV), F32),
            pltpu.VMEM((units, GDN_DK + CHUNK, GDN_DK), BF16),
            pltpu.VMEM((units, GDN_DK, GDN_DV), F32),
            pltpu.VMEM((units, CHUNK, GDN_DV), F32),
            pltpu.VMEM((tt // CHUNK * SUBLANES, LANES), F32),
        ],
        compiler_params=pltpu.CompilerParams(
            dimension_semantics=("parallel", "arbitrary"), vmem_limit_bytes=VMEM_LIMIT),
        name="gdn_scan",
    )(pg, pm, nega, dtb, norm_g.reshape(1, GDN_DV))


def _ssd_kernel(ps_ref, pm_ref, nega_ref, dtb_ref, dsk_ref, ng_ref, o_ref, h_ref, *, tt):
    t = pl.program_id(1)

    @pl.when(t == 0)
    def _():
        h_ref[...] = jnp.zeros_like(h_ref)

    row, col = _tri_masks()
    incl = row >= col
    tril = incl.astype(F32)
    nega = nega_ref[...]
    dtb = dtb_ref[...]
    dsk = dsk_ref[...]
    lane = lax.broadcasted_iota(jnp.int32, (CHUNK, LANES), 1)
    lo_half = lane < SSD_P
    incl2 = lax.broadcasted_iota(jnp.int32, (CHUNK, LANES), 0) >= (lane & (SSD_P - 1))
    lane1 = lax.broadcasted_iota(jnp.int32, (1, LANES), 1)
    lo_half1 = lane1 < SSD_P
    heads_per_group = SSD_HEADS // SSD_G
    gw = SSD_INNER // SSD_G

    def pair_cols(arr, l0):
        sel = lo_half if arr.shape[0] == CHUNK else lo_half1
        return jnp.where(sel, arr[:, l0:l0 + 1], arr[:, l0 + 1:l0 + 2])

    def chunk(c, carry):
        r0 = pl.multiple_of(c * CHUNK, CHUNK)
        rows = pl.ds(r0, CHUNK)
        pmv = pm_ref[rows, :]
        dt_all = _softplus(pmv + dtb)
        a_all = nega * dt_all
        acum = jnp.dot(tril, a_all, precision=HIGHEST, preferred_element_type=F32)
        acum_t = acum.T
        a_last = acum[CHUNK - 1:CHUNK, :]

        for g in range(SSD_G):
            bg = ps_ref[rows, PS_B + g * SSD_N:PS_B + (g + 1) * SSD_N]
            cg = ps_ref[rows, PS_C + g * SSD_N:PS_C + (g + 1) * SSD_N]
            cb2 = _dot_nt(cg, jnp.concatenate([bg, bg], axis=0))
            hstate = h_ref[g]
            y_off = _dot(cg, hstate.astype(BF16))
            xw_parts = []
            scale_parts = []
            y_parts = []
            for p in range(heads_per_group // 2):
                head0 = g * heads_per_group + 2 * p
                l0 = LANE_DT + head0
                x_p = ps_ref[rows, PS_X + head0 * SSD_P:PS_X + (head0 + 2) * SSD_P].astype(F32)
                ac_p = pair_cols(acum, l0)
                al_p = pair_cols(a_last, l0)
                r0_t = acum_t[l0:l0 + 1, :]
                r1_t = acum_t[l0 + 1:l0 + 2, :]
                arow_p = jnp.where(lo_half1, jnp.concatenate([r0_t, r0_t], axis=1),
                                   jnp.concatenate([r1_t, r1_t], axis=1))
                decay = jnp.exp(jnp.where(incl2, ac_p - arow_p, -jnp.inf))
                xdt = x_p * pair_cols(dt_all, l0)
                rhs = jnp.concatenate([jnp.where(lo_half, xdt, 0.0), jnp.where(lo_half, 0.0, xdt)],
                                      axis=0).astype(BF16)
                y_diag = _dot((cb2 * decay).astype(BF16), rhs)
                y_p = (y_diag + jnp.exp(ac_p) * y_off[:, p * LANES:(p + 1) * LANES]
                       + pair_cols(dsk, l0) * x_p)
                y_parts.append(y_p)
                xw_parts.append((xdt * jnp.exp(al_p - ac_p)).astype(BF16))
                scale_parts.append(jnp.exp(al_p))
            xw = jnp.concatenate(xw_parts, axis=1)
            scale = jnp.concatenate(scale_parts, axis=1)
            h_ref[g] = scale * hstate + _dot_tn(bg, xw)
            y = jnp.concatenate(y_parts, axis=1)
            z = ps_ref[rows, g * gw:(g + 1) * gw].astype(F32)
            yz = y * _silu(z)
            yn = yz * lax.rsqrt(jnp.mean(yz * yz, axis=-1, keepdims=True) + EPS)
            o_ref[rows, g * gw:(g + 1) * gw] = (yn * ng_ref[:, g * gw:(g + 1) * gw]).astype(BF16)
        return carry

    lax.fori_loop(0, tt // CHUNK, chunk, 0)


def _ssd(ps, pm, a_log, dt_bias, d_skip, norm_g, bsz, seq):
    tt = min(TT_SCAN, seq)
    nt = seq // tt

    def lanes(v):
        return jnp.zeros((1, LANES), F32).at[0, LANE_DT:LANE_DT + SSD_HEADS].set(v)

    return pl.pallas_call(
        functools.partial(_ssd_kernel, tt=tt),
        grid=(bsz, nt),
        in_specs=[
            pl.BlockSpec((tt, PS_W), lambda b, t: (b * nt + t, 0)),
            pl.BlockSpec((tt, LANES), lambda b, t: (b * nt + t, 0)),
            pl.BlockSpec((1, LANES), lambda b, t: (0, 0)),
            pl.BlockSpec((1, LANES), lambda b, t: (0, 0)),
            pl.BlockSpec((1, LANES), lambda b, t: (0, 0)),
            pl.BlockSpec((1, SSD_INNER), lambda b, t: (0, 0)),
        ],
        out_specs=pl.BlockSpec((tt, SSD_INNER), lambda b, t: (b * nt + t, 0)),
        out_shape=jax.ShapeDtypeStruct((bsz * seq, SSD_INNER), BF16),
        scratch_shapes=[pltpu.VMEM((SSD_G, SSD_N, SSD_INNER // SSD_G), F32)],
        compiler_params=pltpu.CompilerParams(
            dimension_semantics=("parallel", "arbitrary"), vmem_limit_bytes=VMEM_LIMIT),
        name="ssd_scan",
    )(ps, pm, lanes(-jnp.exp(a_log)), lanes(dt_bias), lanes(d_skip), norm_g.reshape(1, SSD_INNER))


RT_IDX = 0
RT_GATE = TOP_K
RT_POS = 2 * TOP_K


def _outproj_router_kernel(go_ref, so_ref, x_ref, mod_ref, g_ref, wo_ref, rw_ref, rb_ref,
                           xn_ref, h_ref, rt_ref, cnt_ref, run_ref, *, tm):
    i = pl.program_id(0)

    @pl.when(i == 0)
    def _():
        run_ref[...] = jnp.zeros_like(run_ref)

    mix = _dot(go_ref[...], wo_ref[0:GDN_V, :]) + _dot(so_ref[...], wo_ref[GDN_V:, :])
    xn = x_ref[...] + mod_ref[2:3, :] * mix
    xn_ref[...] = xn
    h = _modulated_norm(xn, g_ref[...], mod_ref[3:4, :], mod_ref[4:5, :])
    h_ref[...] = _pack_rows(h)
    h_hi = h.astype(BF16)
    h_lo = (h - h_hi.astype(F32)).astype(BF16)
    hw = _dot(jnp.concatenate([h_hi, h_lo], axis=0), rw_ref[0:D_MODEL, :])
    logits = hw[0:tm, :] + hw[tm:, :] + _dot(h_hi, rw_ref[D_MODEL:, :]) + rb_ref[...]
    lane = lax.broadcasted_iota(jnp.int32, (tm, LANES), 1).astype(F32)
    work = logits
    tops = []
    idxs = []
    for _ in range(TOP_K):
        m = jnp.max(work, axis=-1, keepdims=True)
        idx = jnp.min(jnp.where(work == m, lane, float(LANES)), axis=-1, keepdims=True)
        work = jnp.where(lane == idx, -jnp.inf, work)
        tops.append(m)
        idxs.append(idx)
    exps = [jnp.exp(m - tops[0]) for m in tops]
    denom = exps[0] + exps[1] + exps[2] + exps[3]
    onehot = jnp.zeros((tm, LANES), F32)
    for idx in idxs:
        onehot = onehot + (lane == idx).astype(F32)
    trow = lax.broadcasted_iota(jnp.int32, (tm, tm), 0)
    tcol = lax.broadcasted_iota(jnp.int32, (tm, tm), 1)
    before = _dot((trow > tcol).astype(BF16), onehot.astype(BF16)) + run_ref[0:1, :]
    rt = jnp.zeros((tm, LANES), F32)
    for k in range(TOP_K):
        pos = jnp.sum(jnp.where(lane == idxs[k], before, 0.0), axis=-1, keepdims=True)
        rt = jnp.where(lane == RT_IDX + k, idxs[k], rt)
        rt = jnp.where(lane == RT_GATE + k, exps[k] / denom, rt)
        rt = jnp.where(lane == RT_POS + k, pos, rt)
    rt_ref[...] = rt
    run = run_ref[0:1, :] + jnp.sum(onehot, axis=0, keepdims=True)
    run_ref[...] = jnp.broadcast_to(run, run_ref.shape)
    cnt_ref[...] = jnp.broadcast_to(run, cnt_ref.shape)


def _outproj_router(go, so, x2, mod_l, g, w_out, rw, rb, seq):
    n_tok = x2.shape[0]
    tm = min(TM_ROUTER, seq)
    return pl.pallas_call(
        functools.partial(_outproj_router_kernel, tm=tm),
        grid=(n_tok // tm,),
        in_specs=[
            pl.BlockSpec((tm, GDN_V), lambda i: (i, 0)),
            pl.BlockSpec((tm, SSD_INNER), lambda i: (i, 0)),
            pl.BlockSpec((tm, D_MODEL), lambda i: (i, 0)),
            pl.BlockSpec((None, SUBLANES, D_MODEL), lambda i: ((i * tm) // seq, 0, 0)),
            pl.BlockSpec((1, D_MODEL), lambda i: (0, 0)),
            pl.BlockSpec((GDN_V + SSD_INNER, D_MODEL), lambda i: (0, 0)),
            pl.BlockSpec((2 * D_MODEL, LANES), lambda i: (0, 0)),
            pl.BlockSpec((1, LANES), lambda i: (0, 0)),
        ],
        out_specs=[
            pl.BlockSpec((tm, D_MODEL), lambda i: (i, 0)),
            pl.BlockSpec((tm, ROW_WORDS), lambda i: (i, 0)),
            pl.BlockSpec((tm, LANES), lambda i: (i, 0)),
            pl.BlockSpec((SUBLANES, LANES), lambda i: (0, 0)),
        ],
        out_shape=[
            jax.ShapeDtypeStruct((n_tok, D_MODEL), F32),
            jax.ShapeDtypeStruct((n_tok, ROW_WORDS), U32),
            jax.ShapeDtypeStruct((n_tok, LANES), F32),
            jax.ShapeDtypeStruct((SUBLANES, LANES), F32),
        ],
        scratch_shapes=[pltpu.VMEM((SUBLANES, LANES), F32)],
        compiler_params=pltpu.CompilerParams(
            dimension_semantics=("arbitrary",), vmem_limit_bytes=VMEM_LIMIT),
        name="outproj_router",
    )(go, so, x2, mod_l, g, w_out, rw, rb)


def _sc_workers():
    info = plsc.get_sparse_core_info()
    return info.num_cores, info.num_subcores


def _scatter_rows(src, dest_t, n_rows):
    n_tok, width = src.shape
    n_k = dest_t.shape[0]
    nc, ns = _sc_workers()
    per_w = n_tok // (nc * ns)
    win = min(SC_WINDOW, per_w)
    mesh = plsc.VectorSubcoreMesh(core_axis_name="c", subcore_axis_name="s")

    @functools.partial(
        pl.kernel, mesh=mesh,
        out_type=jax.ShapeDtypeStruct((n_rows, width), src.dtype),
        scratch_types=[pltpu.VMEM((win,), jnp.int32), pltpu.VMEM((win, width), src.dtype)],
    )
    def scatter_kernel(src_hbm, idx_hbm, out_hbm, idx_v, rows_v):
        wid = lax.axis_index("s") * nc + lax.axis_index("c")
        base = wid * per_w

        @pl.loop(0, per_w // win)
        def _(j):
            off = base + j * win
            pltpu.sync_copy(src_hbm.at[pl.ds(off, win)], rows_v)
            for k in range(n_k):
                pltpu.sync_copy(idx_hbm.at[k, pl.ds(off, win)], idx_v)
                pltpu.sync_copy(rows_v, out_hbm.at[idx_v])

    return scatter_kernel(src, dest_t)


def _gather_rows(table, idx):
    n_idx = idx.shape[0]
    width = table.shape[1]
    nc, ns = _sc_workers()
    per_w = n_idx // (nc * ns)
    win = min(SC_WINDOW, per_w)
    mesh = plsc.VectorSubcoreMesh(core_axis_name="c", subcore_axis_name="s")

    @functools.partial(
        pl.kernel, mesh=mesh,
        out_type=jax.ShapeDtypeStruct((n_idx, width), table.dtype),
        scratch_types=[pltpu.VMEM((win,), jnp.int32), pltpu.VMEM((win, width), table.dtype)],
    )
    def gather_kernel(table_hbm, idx_hbm, out_hbm, idx_v, rows_v):
        wid = lax.axis_index("s") * nc + lax.axis_index("c")
        base = wid * per_w

        @pl.loop(0, per_w // win)
        def _(j):
            off = base + j * win
            pltpu.sync_copy(idx_hbm.at[pl.ds(off, win)], idx_v)
            pltpu.sync_copy(table_hbm.at[idx_v], rows_v)
            pltpu.sync_copy(rows_v, out_hbm.at[pl.ds(off, win)])

    return gather_kernel(table, idx)


def _expert_kernel(be_ref, bv_ref, x_ref, wgu_ref, bgu_ref, wd_ref, bd_ref, y_ref, wgu_b, wd_b, *, tr):
    i = pl.program_id(0)
    e = be_ref[i]
    prev = be_ref[jnp.maximum(i - 1, 0)]

    @pl.when((i == 0) | (e != prev))
    def _():
        wgu_b[...] = wgu_ref[...].astype(BF16)
        wd_b[...] = wd_ref[...].astype(BF16)

    valid = bv_ref[i]

    @pl.when(valid == 0)
    def _():
        y_ref[...] = jnp.zeros_like(y_ref)

    @pl.when(valid > 0)
    def _():
        rows = lax.broadcasted_iota(jnp.int32, (tr, D_MODEL), 0)
        x = jnp.where(rows < valid, _unpack_rows(x_ref[...]), 0.0).astype(BF16)
        acc = jnp.zeros((tr, D_MODEL), F32)
        for f in range(0, D_FF, FF_CHUNK):
            gate = _dot(x, wgu_b[:, f:f + FF_CHUNK]) + bgu_ref[:, f:f + FF_CHUNK]
            up = _dot(x, wgu_b[:, D_FF + f:D_FF + f + FF_CHUNK]) + bgu_ref[:, D_FF + f:D_FF + f + FF_CHUNK]
            gate = jnp.minimum(gate, SWIGLU_LIMIT)
            up = jnp.clip(up, -SWIGLU_LIMIT, SWIGLU_LIMIT)
            act = (up + 1.0) * (gate * _sigmoid(gate * SWIGLU_ALPHA))
            acc = acc + _dot(act.astype(BF16), wd_b[f:f + FF_CHUNK, :])
        y_ref[...] = _pack_rows(acc + bd_ref[...])


def _experts(x_rows, blk_expert, blk_valid, w_gu, b_gu, w_down, b_down, layer):
    n_rows, row_w = x_rows.shape
    depth = w_gu.shape[0]
    tr = ROW_BLOCK
    nb = n_rows // tr
    grid_spec = pltpu.PrefetchScalarGridSpec(
        num_scalar_prefetch=2,
        grid=(nb,),
        in_specs=[
            pl.BlockSpec((tr, row_w), lambda i, be, bv: (i, 0)),
            pl.BlockSpec((None, None, D_MODEL, 2 * D_FF), lambda i, be, bv: (layer, be[i], 0, 0)),
            pl.BlockSpec((None, None, 1, 2 * D_FF), lambda i, be, bv: (layer, be[i], 0, 0)),
            pl.BlockSpec((None, None, D_FF, D_MODEL), lambda i, be, bv: (layer, be[i], 0, 0)),
            pl.BlockSpec((None, None, 1, D_MODEL), lambda i, be, bv: (layer, be[i], 0, 0)),
        ],
        out_specs=pl.BlockSpec((tr, row_w), lambda i, be, bv: (i, 0)),
        scratch_shapes=[pltpu.VMEM((D_MODEL, 2 * D_FF), BF16), pltpu.VMEM((D_FF, D_MODEL), BF16)],
    )
    return pl.pallas_call(
        functools.partial(_expert_kernel, tr=tr),
        grid_spec=grid_spec,
        out_shape=jax.ShapeDtypeStruct((n_rows, row_w), x_rows.dtype),
        compiler_params=pltpu.CompilerParams(
            dimension_semantics=("arbitrary",), vmem_limit_bytes=VMEM_LIMIT),
        name="moe_experts",
    )(blk_expert, blk_valid, x_rows, w_gu, b_gu.reshape(depth, N_EXPERTS, 1, 2 * D_FF), w_down,
      b_down.reshape(depth, N_EXPERTS, 1, D_MODEL))


def _combine_kernel(xn_ref, y4_ref, rt_ref, mod_ref, g_ref, o_ref, *, final):
    rt = rt_ref[...]
    acc = rt[:, RT_GATE:RT_GATE + 1] * _unpack_rows(y4_ref[0])
    for k in range(1, TOP_K):
        acc = acc + rt[:, RT_GATE + k:RT_GATE + k + 1] * _unpack_rows(y4_ref[k])
    x = xn_ref[...] + mod_ref[5:6, :] * acc
    if final:
        ms = jnp.mean(x * x, axis=-1, keepdims=True)
        x = x * lax.rsqrt(ms + EPS) * g_ref[...]
    o_ref[...] = x


def _combine(xn, y4, rt, mod_l, final_g, seq, final):
    n_tok = xn.shape[0]
    row_w = y4.shape[2]
    tm = min(TM_PROJ, seq)
    return pl.pallas_call(
        functools.partial(_combine_kernel, final=final),
        grid=(n_tok // tm,),
        in_specs=[
            pl.BlockSpec((tm, D_MODEL), lambda i: (i, 0)),
            pl.BlockSpec((TOP_K, tm, row_w), lambda i: (0, i, 0)),
            pl.BlockSpec((tm, LANES), lambda i: (i, 0)),
            pl.BlockSpec((None, SUBLANES, D_MODEL), lambda i: ((i * tm) // seq, 0, 0)),
            pl.BlockSpec((1, D_MODEL), lambda i: (0, 0)),
        ],
        out_specs=pl.BlockSpec((tm, D_MODEL), lambda i: (i, 0)),
        out_shape=jax.ShapeDtypeStruct((n_tok, D_MODEL), F32),
        compiler_params=pltpu.CompilerParams(
            dimension_semantics=("parallel",), vmem_limit_bytes=VMEM_LIMIT),
        name="moe_combine",
    )(xn, y4, rt, mod_l, final_g)


def _permute_w_in(w_in):
    off = 0
    gdn_qkvz = w_in[:, off:off + PG_W]; off += PG_W
    small_ba = w_in[:, off:off + 2 * GDN_HEADS]; off += 2 * GDN_HEADS
    ssd_zxbc = w_in[:, off:off + PS_W]; off += PS_W
    ssd_dt = w_in[:, off:off + SSD_HEADS]
    pad = jnp.zeros((w_in.shape[0], LANES - 2 * GDN_HEADS - SSD_HEADS), w_in.dtype)
    return jnp.concatenate([gdn_qkvz, ssd_zxbc, small_ba, ssd_dt, pad], axis=1).astype(BF16)


def _moe_schedule(rt, counts_row, n_rows):
    idx = rt[:, RT_IDX:RT_IDX + TOP_K].astype(jnp.int32)
    pos = rt[:, RT_POS:RT_POS + TOP_K].astype(jnp.int32)
    counts = counts_row[0, :N_EXPERTS].astype(jnp.int32)
    padded = (counts + ROW_BLOCK - 1) // ROW_BLOCK * ROW_BLOCK
    pad_end = jnp.cumsum(padded)
    pad_start = pad_end - padded
    experts = jnp.arange(N_EXPERTS, dtype=jnp.int32)
    dest = pos + jnp.sum(jnp.where(idx[..., None] == experts, pad_start, 0), axis=-1)
    blk_start = jnp.arange(n_rows // ROW_BLOCK, dtype=jnp.int32) * ROW_BLOCK
    blk_expert = jnp.minimum(jnp.sum(blk_start[:, None] >= pad_end[None, :], axis=1), N_EXPERTS - 1)
    blk_onehot = blk_expert[:, None] == experts
    blk_valid = (jnp.sum(jnp.where(blk_onehot, counts + pad_start, 0), axis=1) - blk_start)
    blk_valid = jnp.where(blk_start < pad_end[-1], jnp.clip(blk_valid, 0, ROW_BLOCK), 0)
    return dest, blk_expert.astype(jnp.int32), blk_valid.astype(jnp.int32)


def _forward(x, c, ada_w, ada_b, norm1_g, norm2_g, w_in, gdn_conv_w, gdn_a_log, gdn_dt_bias, gdn_norm_g,
             ssd_conv_w, ssd_conv_b, ssd_a_log, ssd_dt_bias, ssd_d, ssd_norm_g, w_out, router_w, router_b,
             moe_w_gu, moe_b_gu, moe_w_down, moe_b_down, final_g):
    bsz, seq, _ = x.shape
    depth = ada_w.shape[0]
    n_tok = bsz * seq
    n_rows = n_tok * TOP_K + N_EXPERTS * ROW_BLOCK
    mod = _ada_mod(c, ada_w, ada_b)
    x2 = x.reshape(n_tok, D_MODEL)
    fg = final_g.reshape(1, D_MODEL)
    for l in range(depth):
        conv_w = jnp.concatenate([gdn_conv_w[l], ssd_conv_w[l]], axis=1)
        conv_b = jnp.concatenate([jnp.zeros((1, GDN_CONV_CH), F32), ssd_conv_b[l].reshape(1, -1)], axis=1)
        pg, ps, pm = _inproj(x2, mod[l], norm1_g[l].reshape(1, D_MODEL), _permute_w_in(w_in[l]),
                             conv_w, conv_b, seq)
        go = _gdn(pg, pm, gdn_a_log[l], gdn_dt_bias[l], gdn_norm_g[l], bsz, seq)
        so = _ssd(ps, pm, ssd_a_log[l], ssd_dt_bias[l], ssd_d[l], ssd_norm_g[l], bsz, seq)
        rw = jnp.zeros((D_MODEL, LANES), F32).at[:, :N_EXPERTS].set(router_w[l])
        rw_hi = rw.astype(BF16)
        rw = jnp.concatenate([rw_hi, (rw - rw_hi.astype(F32)).astype(BF16)], axis=0)
        rb = jnp.full((1, LANES), -1e30, F32).at[0, :N_EXPERTS].set(router_b[l])
        xn, h2, rt, cnt = _outproj_router(go, so, x2, mod[l], norm2_g[l].reshape(1, D_MODEL),
                                          w_out[l].astype(BF16), rw, rb, seq)
        dest, blk_expert, blk_valid = _moe_schedule(rt, cnt, n_rows)
        dest_t = dest.T
        x_rows = _scatter_rows(h2, dest_t, n_rows)
        y_rows = _experts(x_rows, blk_expert, blk_valid, moe_w_gu, moe_b_gu, moe_w_down, moe_b_down, l)
        y4 = _gather_rows(y_rows, dest_t.reshape(TOP_K * n_tok)).reshape(TOP_K, n_tok, y_rows.shape[1])
        x2 = _combine(xn, y4, rt, mod[l], fg, seq, final=(l == depth - 1))
    return x2.reshape(bsz, seq, D_MODEL)


def kernel(x, c, ada_w, ada_b, norm1_g, norm2_g, w_in, gdn_conv_w, gdn_a_log, gdn_dt_bias, gdn_norm_g, ssd_conv_w, ssd_conv_b, ssd_a_log, ssd_dt_bias, ssd_d, ssd_norm_g, w_out, router_w, router_b, moe_w_gu, moe_b_gu, moe_w_down, moe_b_down, final_g):
    return _forward(x, c, ada_w, ada_b, norm1_g, norm2_g, w_in, gdn_conv_w, gdn_a_log, gdn_dt_bias, gdn_norm_g,
                    ssd_conv_w, ssd_conv_b, ssd_a_log, ssd_dt_bias, ssd_d, ssd_norm_g, w_out, router_w, router_b,
                    moe_w_gu, moe_b_gu, moe_w_down, moe_b_down, final_g)
```

```python
import functools

import jax
import jax.numpy as jnp
from jax import lax
from jax.experimental import pallas as pl
from jax.experimental.pallas import tpu as pltpu
from jax.experimental.pallas import tpu_sc as plsc

F32 = jnp.float32
BF16 = jnp.bfloat16
U32 = jnp.uint32
HIGHEST = lax.Precision.HIGHEST

D_MODEL = 1024
CHUNK = 64
CONV_K = 4
GDN_HEADS = 8
GDN_DK = 128
GDN_DV = 128
GDN_QK = GDN_HEADS * GDN_DK
GDN_V = GDN_HEADS * GDN_DV
GDN_CONV_CH = 2 * GDN_QK + GDN_V
SSD_P = 64
SSD_HEADS = 16
SSD_G = 2
SSD_N = 128
SSD_INNER = SSD_HEADS * SSD_P
SSD_BC = SSD_G * SSD_N
SSD_CONV_CH = SSD_INNER + 2 * SSD_BC
N_EXPERTS = 32
TOP_K = 4
D_FF = D_MODEL
SWIGLU_ALPHA = 1.702
SWIGLU_LIMIT = 7.0
EPS = 1e-6

LANES = 128
SUBLANES = 8
PG_W = GDN_CONV_CH + GDN_V
PS_W = SSD_INNER + SSD_CONV_CH
PS_X = SSD_INNER
PS_B = 2 * SSD_INNER
PS_C = 2 * SSD_INNER + SSD_BC
N_CONV = GDN_CONV_CH + SSD_CONV_CH
LANE_BETA = 0
LANE_ALPHA = GDN_HEADS
LANE_DT = 2 * GDN_HEADS
VMEM_LIMIT = 56 * 1024 * 1024

TM_PROJ = 256
TM_ROUTER = 512
TT_SCAN = 512
GDN_PREP_CHUNKS = 4
SSD_ITER_CHUNKS = 4
ROW_BLOCK = 512
FF_CHUNK = 512
CONV_SLAB = 512
SC_WINDOW = 128
ROW_WORDS = D_MODEL // 2


def _dot(a, b):
    return jnp.dot(a, b, preferred_element_type=F32)


def _dot_nt(a, b):
    return lax.dot_general(a, b, (((1,), (1,)), ((), ())), preferred_element_type=F32)


def _dot_tn(a, b):
    return lax.dot_general(a, b, (((0,), (0,)), ((), ())), preferred_element_type=F32)


def _sigmoid(x):
    return 1.0 / (1.0 + jnp.exp(-x))


def _silu(x):
    h = 0.5 * x
    return h + h * jnp.tanh(h)


def _softplus(x):
    return jnp.maximum(x, 0.0) + jnp.log(1.0 + jnp.exp(-jnp.abs(x)))


def _pack_rows(h):
    w = h.shape[1] // 2
    hi = pltpu.bitcast(h[:, :w].astype(BF16).astype(F32), U32)
    lo = pltpu.bitcast(h[:, w:].astype(BF16).astype(F32), U32)
    return hi | (lo >> 16)


def _unpack_rows(p):
    hi = pltpu.bitcast(p & jnp.uint32(0xFFFF0000), F32)
    lo = pltpu.bitcast(p << 16, F32)
    return jnp.concatenate([hi, lo], axis=1)


def _ada_kernel(c_ref, w_ref, b_ref, o_ref):
    c = c_ref[...]
    o_ref[0] = jnp.dot(_silu(c), w_ref[0], precision=HIGHEST, preferred_element_type=F32) + b_ref[0]


def _ada_mod(c, ada_w, ada_b):
    depth = ada_w.shape[0]
    bsz = c.shape[0]
    c8 = jnp.zeros((SUBLANES, D_MODEL), F32).at[:bsz].set(c)
    out = pl.pallas_call(
        _ada_kernel,
        grid=(depth, 6),
        in_specs=[
            pl.BlockSpec((SUBLANES, D_MODEL), lambda l, j: (0, 0)),
            pl.BlockSpec((1, D_MODEL, D_MODEL), lambda l, j: (l, 0, j)),
            pl.BlockSpec((1, 1, D_MODEL), lambda l, j: (l, 0, j)),
        ],
        out_specs=pl.BlockSpec((1, SUBLANES, D_MODEL), lambda l, j: (l, 0, j)),
        out_shape=jax.ShapeDtypeStruct((depth, SUBLANES, 6 * D_MODEL), F32),
        name="ada_mod",
    )(c8, ada_w, ada_b.reshape(depth, 1, 6 * D_MODEL))
    mod = out[:, :bsz].reshape(depth, bsz, 6, D_MODEL)
    return jnp.concatenate([mod, jnp.zeros((depth, bsz, 2, D_MODEL), F32)], axis=2)


def _modulated_norm(x, g, shift, scale):
    ms = jnp.mean(x * x, axis=-1, keepdims=True)
    return (x * lax.rsqrt(ms + EPS) * g) * (1.0 + scale) + shift


def _inproj_kernel(x_ref, mod_ref, g_ref, w_ref, cw_ref, cb_ref, og_ref, os_ref, om_ref,
                   tail_ref, cbuf_ref, *, tm, seq):
    i = pl.program_id(0)

    @pl.when((i * tm) % seq == 0)
    def _():
        tail_ref[...] = jnp.zeros_like(tail_ref)

    h = _modulated_norm(x_ref[...], g_ref[...], mod_ref[0:1, :], mod_ref[1:2, :]).astype(BF16)

    def conv_slab(n, wcol, ccol, bias):
        cs = slice(ccol, ccol + CONV_SLAB)
        pbuf = cbuf_ref.at[n % 2, 0]
        bbuf = cbuf_ref.at[n % 2, 1]
        p = _dot(h, w_ref[:, wcol:wcol + CONV_SLAB])
        pbuf[0:SUBLANES, :] = tail_ref[0, :, cs]
        pbuf[SUBLANES:SUBLANES + tm, :] = p
        tail_ref[0, :, cs] = p[tm - SUBLANES:tm, :]
        p1 = pbuf[SUBLANES - 1:SUBLANES - 1 + tm, :]
        b = p * cw_ref[1:2, cs] + p1 * cw_ref[0:1, cs]
        bbuf[0:SUBLANES, :] = tail_ref[1, :, cs]
        bbuf[SUBLANES:SUBLANES + tm, :] = b
        tail_ref[1, :, cs] = b[tm - SUBLANES:tm, :]
        y = p * cw_ref[3:4, cs] + p1 * cw_ref[2:3, cs] + bbuf[SUBLANES - 2:SUBLANES - 2 + tm, :]
        if bias:
            y = y + cb_ref[:, cs]
        return _silu(y)

    n = 0
    for part, scale in ((0, GDN_DK ** -0.5), (1, 1.0)):
        for j in range(GDN_QK // CONV_SLAB):
            col = part * GDN_QK + j * CONV_SLAB
            y = conv_slab(n, col, col, False)
            n += 1
            for hh in range(CONV_SLAB // GDN_DK):
                yh = y[:, hh * GDN_DK:(hh + 1) * GDN_DK]
                inv = lax.rsqrt(jnp.sum(yh * yh, axis=-1, keepdims=True) + EPS) * scale
                og_ref[:, col + hh * GDN_DK:col + (hh + 1) * GDN_DK] = (yh * inv).astype(BF16)
    for j in range(GDN_V // CONV_SLAB):
        col = 2 * GDN_QK + j * CONV_SLAB
        og_ref[:, col:col + CONV_SLAB] = conv_slab(n, col, col, False).astype(BF16)
        n += 1
    og_ref[:, GDN_CONV_CH:PG_W] = _dot(h, w_ref[:, GDN_CONV_CH:PG_W]).astype(BF16)
    os_ref[:, 0:SSD_INNER] = _dot(h, w_ref[:, PG_W:PG_W + SSD_INNER]).astype(BF16)
    for j in range(SSD_CONV_CH // CONV_SLAB):
        col = j * CONV_SLAB
        os_ref[:, SSD_INNER + col:SSD_INNER + col + CONV_SLAB] = conv_slab(
            n, PG_W + SSD_INNER + col, GDN_CONV_CH + col, True).astype(BF16)
        n += 1
    om_ref[...] = _dot(h, w_ref[:, PG_W + PS_W:])


def _inproj(x2, mod_l, g, w_perm, conv_w, conv_b, seq):
    n_tok = x2.shape[0]
    tm = min(TM_PROJ, seq)
    wtot = w_perm.shape[1]
    return pl.pallas_call(
        functools.partial(_inproj_kernel, tm=tm, seq=seq),
        grid=(n_tok // tm,),
        in_specs=[
            pl.BlockSpec((tm, D_MODEL), lambda i: (i, 0)),
            pl.BlockSpec((None, SUBLANES, D_MODEL), lambda i: ((i * tm) // seq, 0, 0)),
            pl.BlockSpec((1, D_MODEL), lambda i: (0, 0)),
            pl.BlockSpec((D_MODEL, wtot), lambda i: (0, 0)),
            pl.BlockSpec((CONV_K, N_CONV), lambda i: (0, 0)),
            pl.BlockSpec((1, N_CONV), lambda i: (0, 0)),
        ],
        out_specs=[
            pl.BlockSpec((tm, PG_W), lambda i: (i, 0)),
            pl.BlockSpec((tm, PS_W), lambda i: (i, 0)),
            pl.BlockSpec((tm, LANES), lambda i: (i, 0)),
        ],
        out_shape=[
            jax.ShapeDtypeStruct((n_tok, PG_W), BF16),
            jax.ShapeDtypeStruct((n_tok, PS_W), BF16),
            jax.ShapeDtypeStruct((n_tok, LANES), F32),
        ],
        scratch_shapes=[
            pltpu.VMEM((2, SUBLANES, N_CONV), F32),
            pltpu.VMEM((2, 2, SUBLANES + tm, CONV_SLAB), F32),
        ],
        compiler_params=pltpu.CompilerParams(
            dimension_semantics=("arbitrary",), vmem_limit_bytes=VMEM_LIMIT),
        name="norm_inproj",
    )(x2, mod_l, g, w_perm, conv_w, conv_b)


def _tri_masks():
    row = lax.broadcasted_iota(jnp.int32, (CHUNK, CHUNK), 0)
    col = lax.broadcasted_iota(jnp.int32, (CHUNK, CHUNK), 1)
    return row, col


def _block_diag2(m, lo_half):
    zero = jnp.zeros_like(m)
    return jnp.concatenate([jnp.where(lo_half, m, zero), jnp.where(lo_half, zero, m)], axis=0)


def _unit_lower_inverses(nmats, row, col, lo_half):
    eye = (row == col).astype(F32)
    pair = (row >> 1) == (col >> 1)
    xs = [eye - jnp.where(pair, n, 0.0) for n in nmats]
    for lg in range(2, CHUNK.bit_length()):
        mask = ((row >> lg) == (col >> lg)) & ((row >> (lg - 1)) != (col >> (lg - 1)))
        xb = [x.astype(BF16) for x in xs]
        ys = [_dot(jnp.where(mask, n, 0.0).astype(BF16), _block_diag2(b, lo_half)) for n, b in zip(nmats, xb)]
        xs = [x - _dot(b, _block_diag2(y.astype(BF16), lo_half)) for x, b, y in zip(xs, xb, ys)]
    return xs


def _gdn_kernel(pg_ref, pm_ref, nega_ref, dtb_ref, ng_ref, o_ref, s_ref,
                gq_scr, b_scr, o0_scr, el_scr, *, tt):
    t = pl.program_id(1)
    heads = range(GDN_HEADS)

    @pl.when(t == 0)
    def _():
        s_ref[...] = jnp.zeros_like(s_ref)

    row, col = _tri_masks()
    tril = (row >= col).astype(F32)
    row2 = lax.broadcasted_iota(jnp.int32, (CHUNK, LANES), 0)
    lane2 = lax.broadcasted_iota(jnp.int32, (CHUNK, LANES), 1)
    col2 = lane2 & (CHUNK - 1)
    lo_half = lane2 < CHUNK
    lo_half1 = lax.broadcasted_iota(jnp.int32, (1, LANES), 1) < CHUNK
    incl2 = row2 >= col2
    strict2 = row2 > col2
    zero_k = jnp.zeros((CHUNK, GDN_DK), BF16)
    zero_r = jnp.zeros((CHUNK, 2 * GDN_DV), BF16)
    nega = nega_ref[...]
    dtb = dtb_ref[...]
    ng = ng_ref[...]

    def wide(arr, l0):
        n = arr.shape[0]
        return jnp.concatenate([jnp.broadcast_to(arr[:, l0:l0 + 1], (n, LANES)),
                                jnp.broadcast_to(arr[:, l0 + 1:l0 + 2], (n, LANES))], axis=1)

    def halves(w, sel):
        return jnp.where(sel, w[:, 0:LANES], w[:, LANES:])

    def prepare(ci, carry):
        units = []
        for j in range(GDN_PREP_CHUNKS):
            c = ci * GDN_PREP_CHUNKS + j
            rows = pl.ds(pl.multiple_of(c * CHUNK, CHUNK), CHUNK)
            pmv = pm_ref[rows, :]
            beta_all = _sigmoid(pmv)
            g_all = nega * _softplus(pmv + dtb)
            gam = jnp.dot(tril, g_all, precision=HIGHEST, preferred_element_type=F32)
            gam_t = gam.T
            gam_last = gam[CHUNK - 1:CHUNK, :]
            el_scr[pl.ds(pl.multiple_of(c * SUBLANES, SUBLANES), SUBLANES), :] = jnp.broadcast_to(
                jnp.exp(gam_last), (SUBLANES, LANES))
            for h0 in range(0, GDN_HEADS, 2):
                la = LANE_ALPHA + h0
                gam_w = wide(gam, la)
                r0_t = gam_t[la:la + 1, :]
                r1_t = gam_t[la + 1:la + 2, :]
                units.append(dict(
                    slot=c * GDN_HEADS + h0, rows=rows, h0=h0,
                    beta_w=wide(beta_all, LANE_BETA + h0), gam_w=gam_w, gl_w=wide(gam_last, la),
                    grow=jnp.where(lo_half1, jnp.concatenate([r0_t, r0_t], axis=1),
                                   jnp.concatenate([r1_t, r1_t], axis=1))))

        def cols(u, base):
            return pg_ref[u["rows"], base + u["h0"] * GDN_DK:base + (u["h0"] + 2) * GDN_DK]

        q2s = [cols(u, 0) for u in units]
        k2s = [cols(u, GDN_QK) for u in units]
        kbd = [jnp.concatenate([jnp.concatenate([k2[:, 0:GDN_DK], zero_k], axis=1),
                                jnp.concatenate([zero_k, k2[:, GDN_DK:]], axis=1)], axis=0) for k2 in k2s]
        qkks = [_dot_nt(jnp.concatenate([q2, k2], axis=0), r) for q2, k2, r in zip(q2s, k2s, kbd)]
        decays = [jnp.exp(jnp.where(incl2, halves(u["gam_w"], lo_half) - u["grow"], -jnp.inf)) for u in units]
        nmats = [jnp.where(strict2, halves(u["beta_w"], lo_half) * qkk[CHUNK:2 * CHUNK, :] * d, 0.0)
                 for u, qkk, d in zip(units, qkks, decays)]
        amats = [(qkk[0:CHUNK, :] * d).astype(BF16) for qkk, d in zip(qkks, decays)]
        egws = [jnp.exp(u["gam_w"]) for u in units]
        k2f = [k2.astype(F32) for k2 in k2s]
        kds = [(kf * jnp.exp(u["gl_w"] - u["gam_w"])).astype(BF16) for kf, u in zip(k2f, units)]
        tinvs = _unit_lower_inverses(nmats, row2, col2, lo_half)
        vbs = [(cols(u, 2 * GDN_QK).astype(F32) * u["beta_w"]).astype(BF16) for u in units]
        kbs = [(kf * (u["beta_w"] * e)).astype(BF16) for kf, u, e in zip(k2f, units, egws)]
        rhss = [jnp.concatenate(
            [jnp.concatenate([vb[:, 0:GDN_DV], kb[:, 0:GDN_DK], zero_r], axis=1),
             jnp.concatenate([zero_r, vb[:, GDN_DV:], kb[:, GDN_DK:]], axis=1)], axis=0)
            for vb, kb in zip(vbs, kbs)]
        sols = [_dot(x.astype(BF16), r).astype(BF16) for x, r in zip(tinvs, rhss)]
        sbd = [jnp.concatenate([jnp.concatenate([s[:, 0:2 * GDN_DV], zero_r], axis=1),
                                jnp.concatenate([zero_r, s[:, 2 * GDN_DV:]], axis=1)], axis=0) for s in sols]
        a_uw = [_dot(a, r) for a, r in zip(amats, sbd)]
        for hh in range(2):
            k_uw = [_dot_tn(kd[:, hh * GDN_DK:(hh + 1) * GDN_DK], s[:, hh * 2 * GDN_DV:(hh + 1) * 2 * GDN_DV])
                    for kd, s in zip(kds, sols)]
            for u, q2, e, au, ku in zip(units, q2s, egws, a_uw, k_uw):
                qe = (q2[:, hh * GDN_DK:(hh + 1) * GDN_DK].astype(F32) * e[:, hh * LANES:(hh + 1) * LANES])
                base = hh * 2 * GDN_DV
                o0_scr[u["slot"] + hh] = au[:, base:base + GDN_DV]
                b_scr[u["slot"] + hh] = ku[:, 0:GDN_DV]
                gq_scr[u["slot"] + hh] = jnp.concatenate(
                    [ku[:, GDN_DV:], qe - au[:, base + GDN_DV:base + 2 * GDN_DV]], axis=0).astype(BF16)
        return carry

    def scan(c, carry):
        r0 = pl.multiple_of(c * CHUNK, CHUNK)
        e_last = el_scr[pl.ds(pl.multiple_of(c * SUBLANES, SUBLANES), SUBLANES), :][0:1, :]
        states = [s_ref[h] for h in heads]
        rs = [_dot(gq_scr[c * GDN_HEADS + h], states[h].astype(BF16)) for h in heads]
        for h in heads:
            s_ref[h] = (e_last[:, LANE_ALPHA + h:LANE_ALPHA + h + 1] * states[h]
                        - rs[h][0:GDN_DK, :] + b_scr[c * GDN_HEADS + h])
        for h in heads:
            o = rs[h][GDN_DK:, :] + o0_scr[c * GDN_HEADS + h]
            z = pg_ref[pl.ds(r0, CHUNK), GDN_CONV_CH + h * GDN_DV:GDN_CONV_CH + (h + 1) * GDN_DV].astype(F32)
            on = o * lax.rsqrt(jnp.mean(o * o, axis=-1, keepdims=True) + EPS) * ng
            o_ref[pl.ds(r0, CHUNK), h * GDN_DV:(h + 1) * GDN_DV] = (on * _silu(z)).astype(BF16)
        return carry

    lax.fori_loop(0, tt // CHUNK // GDN_PREP_CHUNKS, prepare, 0)
    lax.fori_loop(0, tt // CHUNK, scan, 0, unroll=2)


def _gdn(pg, pm, a_log, dt_bias, norm_g, bsz, seq):
    tt = min(TT_SCAN, seq)
    nt = seq // tt
    nega = jnp.zeros((1, LANES), F32).at[0, LANE_ALPHA:LANE_ALPHA + GDN_HEADS].set(-jnp.exp(a_log))
    dtb = jnp.zeros((1, LANES), F32).at[0, LANE_ALPHA:LANE_ALPHA + GDN_HEADS].set(dt_bias)
    units = tt // CHUNK * GDN_HEADS
    return pl.pallas_call(
        functools.partial(_gdn_kernel, tt=tt),
        grid=(bsz, nt),
        in_specs=[
            pl.BlockSpec((tt, PG_W), lambda b, t: (b * nt + t, 0)),
            pl.BlockSpec((tt, LANES), lambda b, t: (b * nt + t, 0)),
            pl.BlockSpec((1, LANES), lambda b, t: (0, 0)),
            pl.BlockSpec((1, LANES), lambda b, t: (0, 0)),
            pl.BlockSpec((1, GDN_DV), lambda b, t: (0, 0)),
        ],
        out_specs=pl.BlockSpec((tt, GDN_V), lambda b, t: (b * nt + t, 0)),
        out_shape=jax.ShapeDtypeStruct((bsz * seq, GDN_V), BF16),
        scratch_shapes=[
            pltpu.VMEM((GDN_HEADS, GDN_DK, GDN_DV), F32),
            pltpu.VMEM((units, GDN_DK + CHUNK, GDN_DK), BF16),
            pltpu.VMEM((units, GDN_DK, GDN_DV), F32),
            pltpu.VMEM((units, CHUNK, GDN_DV), F32),
            pltpu.VMEM((tt // CHUNK * SUBLANES, LANES), F32),
        ],
        compiler_params=pltpu.CompilerParams(
            dimension_semantics=("parallel", "arbitrary"), vmem_limit_bytes=VMEM_LIMIT),
        name="gdn_scan",
    )(pg, pm, nega, dtb, norm_g.reshape(1, GDN_DV))


def _ssd_kernel(ps_ref, pm_ref, nega_ref, dtb_ref, dsk_ref, ng_ref, o_ref, h_ref, *, tt):
    t = pl.program_id(1)

    @pl.when(t == 0)
    def _():
        h_ref[...] = jnp.zeros_like(h_ref)

    row, col = _tri_masks()
    incl = row >= col
    tril = incl.astype(F32)
    nega = nega_ref[...]
    dtb = dtb_ref[...]
    dsk = dsk_ref[...]
    lane = lax.broadcasted_iota(jnp.int32, (CHUNK, LANES), 1)
    lo_half = lane < SSD_P
    incl2 = lax.broadcasted_iota(jnp.int32, (CHUNK, LANES), 0) >= (lane & (SSD_P - 1))
    lane1 = lax.broadcasted_iota(jnp.int32, (1, LANES), 1)
    lo_half1 = lane1 < SSD_P
    heads_per_group = SSD_HEADS // SSD_G
    gw = SSD_INNER // SSD_G

    def pair_cols(arr, l0):
        sel = lo_half if arr.shape[0] == CHUNK else lo_half1
        return jnp.where(sel, arr[:, l0:l0 + 1], arr[:, l0 + 1:l0 + 2])

    pairs_per_group = heads_per_group // 2

    def chunk_group(ci, carry):
        groups = []
        for j in range(SSD_ITER_CHUNKS):
            c = ci * SSD_ITER_CHUNKS + j
            rows = pl.ds(pl.multiple_of(c * CHUNK, CHUNK), CHUNK)
            pmv = pm_ref[rows, :]
            dt_all = _softplus(pmv + dtb)
            acum = jnp.dot(tril, nega * dt_all, precision=HIGHEST, preferred_element_type=F32)
            info = dict(rows=rows, dt=dt_all, acum=acum, acum_t=acum.T, a_last=acum[CHUNK - 1:CHUNK, :])
            for g in range(SSD_G):
                groups.append(dict(info, g=g))
        for gr in groups:
            g = gr["g"]
            gr["bg"] = ps_ref[gr["rows"], PS_B + g * SSD_N:PS_B + (g + 1) * SSD_N]
            gr["cg"] = ps_ref[gr["rows"], PS_C + g * SSD_N:PS_C + (g + 1) * SSD_N]
        for gr in groups:
            gr["cb2"] = _dot_nt(gr["cg"], jnp.concatenate([gr["bg"], gr["bg"]], axis=0))

        units = [dict(gr=gr, p=p, l0=LANE_DT + gr["g"] * heads_per_group + 2 * p)
                 for gr in groups for p in range(pairs_per_group)]
        for u in units:
            gr, l0 = u["gr"], u["l0"]
            head0 = l0 - LANE_DT
            u["x"] = ps_ref[gr["rows"], PS_X + head0 * SSD_P:PS_X + (head0 + 2) * SSD_P].astype(F32)
            u["ac"] = pair_cols(gr["acum"], l0)
            u["al"] = pair_cols(gr["a_last"], l0)
            r0_t = gr["acum_t"][l0:l0 + 1, :]
            r1_t = gr["acum_t"][l0 + 1:l0 + 2, :]
            u["arow"] = jnp.where(lo_half1, jnp.concatenate([r0_t, r0_t], axis=1),
                                  jnp.concatenate([r1_t, r1_t], axis=1))
            u["xdt"] = u["x"] * pair_cols(gr["dt"], l0)
        for u in units:
            decay = jnp.exp(jnp.where(incl2, u["ac"] - u["arow"], -jnp.inf))
            u["m"] = (u["gr"]["cb2"] * decay).astype(BF16)
            u["rhs"] = _block_diag2(u["xdt"], lo_half).astype(BF16)
        for u in units:
            u["y_diag"] = _dot(u["m"], u["rhs"])
            u["xw"] = (u["xdt"] * jnp.exp(u["al"] - u["ac"])).astype(BF16)
        for gr in groups:
            mine = [u for u in units if u["gr"] is gr]
            gr["upd"] = _dot_tn(gr["bg"], jnp.concatenate([u["xw"] for u in mine], axis=1))
            gr["scale"] = jnp.concatenate([jnp.exp(u["al"]) for u in mine], axis=1)

        states = [h_ref[g] for g in range(SSD_G)]
        for gr in groups:
            gr["y_off"] = _dot(gr["cg"], states[gr["g"]].astype(BF16))
            states[gr["g"]] = gr["scale"] * states[gr["g"]] + gr["upd"]
        for g in range(SSD_G):
            h_ref[g] = states[g]

        for gr in groups:
            g = gr["g"]
            mine = [u for u in units if u["gr"] is gr]
            y = jnp.concatenate(
                [u["y_diag"] + jnp.exp(u["ac"]) * gr["y_off"][:, u["p"] * LANES:(u["p"] + 1) * LANES]
                 + pair_cols(dsk, u["l0"]) * u["x"] for u in mine], axis=1)
            z = ps_ref[gr["rows"], g * gw:(g + 1) * gw].astype(F32)
            yz = y * _silu(z)
            yn = yz * lax.rsqrt(jnp.mean(yz * yz, axis=-1, keepdims=True) + EPS)
            o_ref[gr["rows"], g * gw:(g + 1) * gw] = (yn * ng_ref[:, g * gw:(g + 1) * gw]).astype(BF16)
        return carry

    lax.fori_loop(0, tt // CHUNK // SSD_ITER_CHUNKS, chunk_group, 0)


def _ssd(ps, pm, a_log, dt_bias, d_skip, norm_g, bsz, seq):
    tt = min(TT_SCAN, seq)
    nt = seq // tt

    def lanes(v):
        return jnp.zeros((1, LANES), F32).at[0, LANE_DT:LANE_DT + SSD_HEADS].set(v)

    return pl.pallas_call(
        functools.partial(_ssd_kernel, tt=tt),
        grid=(bsz, nt),
        in_specs=[
            pl.BlockSpec((tt, PS_W), lambda b, t: (b * nt + t, 0)),
            pl.BlockSpec((tt, LANES), lambda b, t: (b * nt + t, 0)),
            pl.BlockSpec((1, LANES), lambda b, t: (0, 0)),
            pl.BlockSpec((1, LANES), lambda b, t: (0, 0)),
            pl.BlockSpec((1, LANES), lambda b, t: (0, 0)),
            pl.BlockSpec((1, SSD_INNER), lambda b, t: (0, 0)),
        ],
        out_specs=pl.BlockSpec((tt, SSD_INNER), lambda b, t: (b * nt + t, 0)),
        out_shape=jax.ShapeDtypeStruct((bsz * seq, SSD_INNER), BF16),
        scratch_shapes=[pltpu.VMEM((SSD_G, SSD_N, SSD_INNER // SSD_G), F32)],
        compiler_params=pltpu.CompilerParams(
            dimension_semantics=("parallel", "arbitrary"), vmem_limit_bytes=VMEM_LIMIT),
        name="ssd_scan",
    )(ps, pm, lanes(-jnp.exp(a_log)), lanes(dt_bias), lanes(d_skip), norm_g.reshape(1, SSD_INNER))


RT_IDX = 0
RT_GATE = TOP_K
RT_POS = 2 * TOP_K


def _outproj_router_kernel(go_ref, so_ref, x_ref, mod_ref, g_ref, wo_ref, rw_ref, rb_ref,
                           xn_ref, h_ref, rt_ref, cnt_ref, run_ref, *, tm):
    i = pl.program_id(0)

    @pl.when(i == 0)
    def _():
        run_ref[...] = jnp.zeros_like(run_ref)

    mix = _dot(go_ref[...], wo_ref[0:GDN_V, :]) + _dot(so_ref[...], wo_ref[GDN_V:, :])
    xn = x_ref[...] + mod_ref[2:3, :] * mix
    xn_ref[...] = xn
    h = _modulated_norm(xn, g_ref[...], mod_ref[3:4, :], mod_ref[4:5, :])
    h_ref[...] = _pack_rows(h)
    h_hi = h.astype(BF16)
    h_lo = (h - h_hi.astype(F32)).astype(BF16)
    hw = _dot(jnp.concatenate([h_hi, h_lo], axis=0), rw_ref[0:D_MODEL, :])
    logits = hw[0:tm, :] + hw[tm:, :] + _dot(h_hi, rw_ref[D_MODEL:, :]) + rb_ref[...]
    lane = lax.broadcasted_iota(jnp.int32, (tm, LANES), 1).astype(F32)
    work = logits
    tops = []
    idxs = []
    for _ in range(TOP_K):
        m = jnp.max(work, axis=-1, keepdims=True)
        idx = jnp.min(jnp.where(work == m, lane, float(LANES)), axis=-1, keepdims=True)
        work = jnp.where(lane == idx, -jnp.inf, work)
        tops.append(m)
        idxs.append(idx)
    exps = [jnp.exp(m - tops[0]) for m in tops]
    denom = exps[0] + exps[1] + exps[2] + exps[3]
    onehot = jnp.zeros((tm, LANES), F32)
    for idx in idxs:
        onehot = onehot + (lane == idx).astype(F32)
    trow = lax.broadcasted_iota(jnp.int32, (tm, tm), 0)
    tcol = lax.broadcasted_iota(jnp.int32, (tm, tm), 1)
    before = _dot((trow > tcol).astype(BF16), onehot.astype(BF16)) + run_ref[0:1, :]
    rt = jnp.zeros((tm, LANES), F32)
    for k in range(TOP_K):
        pos = jnp.sum(jnp.where(lane == idxs[k], before, 0.0), axis=-1, keepdims=True)
        rt = jnp.where(lane == RT_IDX + k, idxs[k], rt)
        rt = jnp.where(lane == RT_GATE + k, exps[k] / denom, rt)
        rt = jnp.where(lane == RT_POS + k, pos, rt)
    rt_ref[...] = rt
    run = run_ref[0:1, :] + jnp.sum(onehot, axis=0, keepdims=True)
    run_ref[...] = jnp.broadcast_to(run, run_ref.shape)
    cnt_ref[...] = jnp.broadcast_to(run, cnt_ref.shape)


def _outproj_router(go, so, x2, mod_l, g, w_out, rw, rb, seq):
    n_tok = x2.shape[0]
    tm = min(TM_ROUTER, seq)
    return pl.pallas_call(
        functools.partial(_outproj_router_kernel, tm=tm),
        grid=(n_tok // tm,),
        in_specs=[
            pl.BlockSpec((tm, GDN_V), lambda i: (i, 0)),
            pl.BlockSpec((tm, SSD_INNER), lambda i: (i, 0)),
            pl.BlockSpec((tm, D_MODEL), lambda i: (i, 0)),
            pl.BlockSpec((None, SUBLANES, D_MODEL), lambda i: ((i * tm) // seq, 0, 0)),
            pl.BlockSpec((1, D_MODEL), lambda i: (0, 0)),
            pl.BlockSpec((GDN_V + SSD_INNER, D_MODEL), lambda i: (0, 0)),
            pl.BlockSpec((2 * D_MODEL, LANES), lambda i: (0, 0)),
            pl.BlockSpec((1, LANES), lambda i: (0, 0)),
        ],
        out_specs=[
            pl.BlockSpec((tm, D_MODEL), lambda i: (i, 0)),
            pl.BlockSpec((tm, ROW_WORDS), lambda i: (i, 0)),
            pl.BlockSpec((tm, LANES), lambda i: (i, 0)),
            pl.BlockSpec((SUBLANES, LANES), lambda i: (0, 0)),
        ],
        out_shape=[
            jax.ShapeDtypeStruct((n_tok, D_MODEL), F32),
            jax.ShapeDtypeStruct((n_tok, ROW_WORDS), U32),
            jax.ShapeDtypeStruct((n_tok, LANES), F32),
            jax.ShapeDtypeStruct((SUBLANES, LANES), F32),
        ],
        scratch_shapes=[pltpu.VMEM((SUBLANES, LANES), F32)],
        compiler_params=pltpu.CompilerParams(
            dimension_semantics=("arbitrary",), vmem_limit_bytes=VMEM_LIMIT),
        name="outproj_router",
    )(go, so, x2, mod_l, g, w_out, rw, rb)


def _sc_workers():
    info = plsc.get_sparse_core_info()
    return info.num_cores, info.num_subcores


def _scatter_rows(src, dest_t, n_rows):
    n_tok, width = src.shape
    n_k = dest_t.shape[0]
    nc, ns = _sc_workers()
    per_w = n_tok // (nc * ns)
    win = min(SC_WINDOW, per_w)
    mesh = plsc.VectorSubcoreMesh(core_axis_name="c", subcore_axis_name="s")

    @functools.partial(
        pl.kernel, mesh=mesh,
        out_type=jax.ShapeDtypeStruct((n_rows, width), src.dtype),
        scratch_types=[pltpu.VMEM((win,), jnp.int32), pltpu.VMEM((win, width), src.dtype)],
    )
    def scatter_kernel(src_hbm, idx_hbm, out_hbm, idx_v, rows_v):
        wid = lax.axis_index("s") * nc + lax.axis_index("c")
        base = wid * per_w

        @pl.loop(0, per_w // win)
        def _(j):
            off = base + j * win
            pltpu.sync_copy(src_hbm.at[pl.ds(off, win)], rows_v)
            for k in range(n_k):
                pltpu.sync_copy(idx_hbm.at[k, pl.ds(off, win)], idx_v)
                pltpu.sync_copy(rows_v, out_hbm.at[idx_v])

    return scatter_kernel(src, dest_t)


def _gather_rows(table, idx):
    n_idx = idx.shape[0]
    width = table.shape[1]
    nc, ns = _sc_workers()
    per_w = n_idx // (nc * ns)
    win = min(SC_WINDOW, per_w)
    mesh = plsc.VectorSubcoreMesh(core_axis_name="c", subcore_axis_name="s")

    @functools.partial(
        pl.kernel, mesh=mesh,
        out_type=jax.ShapeDtypeStruct((n_idx, width), table.dtype),
        scratch_types=[pltpu.VMEM((win,), jnp.int32), pltpu.VMEM((win, width), table.dtype)],
    )
    def gather_kernel(table_hbm, idx_hbm, out_hbm, idx_v, rows_v):
        wid = lax.axis_index("s") * nc + lax.axis_index("c")
        base = wid * per_w

        @pl.loop(0, per_w // win)
        def _(j):
            off = base + j * win
            pltpu.sync_copy(idx_hbm.at[pl.ds(off, win)], idx_v)
            pltpu.sync_copy(table_hbm.at[idx_v], rows_v)
            pltpu.sync_copy(rows_v, out_hbm.at[pl.ds(off, win)])

    return gather_kernel(table, idx)


def _expert_kernel(be_ref, bv_ref, x_ref, wgu_ref, bgu_ref, wd_ref, bd_ref, y_ref, wgu_b, wd_b, *, tr):
    i = pl.program_id(0)
    e = be_ref[i]
    prev = be_ref[jnp.maximum(i - 1, 0)]

    @pl.when((i == 0) | (e != prev))
    def _():
        wgu_b[...] = wgu_ref[...].astype(BF16)
        wd_b[...] = wd_ref[...].astype(BF16)

    valid = bv_ref[i]

    @pl.when(valid == 0)
    def _():
        y_ref[...] = jnp.zeros_like(y_ref)

    @pl.when(valid > 0)
    def _():
        rows = lax.broadcasted_iota(jnp.int32, (tr, D_MODEL), 0)
        x = jnp.where(rows < valid, _unpack_rows(x_ref[...]), 0.0).astype(BF16)
        acc = jnp.zeros((tr, D_MODEL), F32)
        for f in range(0, D_FF, FF_CHUNK):
            gate = _dot(x, wgu_b[:, f:f + FF_CHUNK]) + bgu_ref[:, f:f + FF_CHUNK]
            up = _dot(x, wgu_b[:, D_FF + f:D_FF + f + FF_CHUNK]) + bgu_ref[:, D_FF + f:D_FF + f + FF_CHUNK]
            gate = jnp.minimum(gate, SWIGLU_LIMIT)
            up = jnp.clip(up, -SWIGLU_LIMIT, SWIGLU_LIMIT)
            act = (up + 1.0) * (gate * _sigmoid(gate * SWIGLU_ALPHA))
            acc = acc + _dot(act.astype(BF16), wd_b[f:f + FF_CHUNK, :])
        y_ref[...] = _pack_rows(acc + bd_ref[...])


def _experts(x_rows, blk_expert, blk_valid, w_gu, b_gu, w_down, b_down, layer):
    n_rows, row_w = x_rows.shape
    depth = w_gu.shape[0]
    tr = ROW_BLOCK
    nb = n_rows // tr
    grid_spec = pltpu.PrefetchScalarGridSpec(
        num_scalar_prefetch=2,
        grid=(nb,),
        in_specs=[
            pl.BlockSpec((tr, row_w), lambda i, be, bv: (i, 0)),
            pl.BlockSpec((None, None, D_MODEL, 2 * D_FF), lambda i, be, bv: (layer, be[i], 0, 0)),
            pl.BlockSpec((None, None, 1, 2 * D_FF), lambda i, be, bv: (layer, be[i], 0, 0)),
            pl.BlockSpec((None, None, D_FF, D_MODEL), lambda i, be, bv: (layer, be[i], 0, 0)),
            pl.BlockSpec((None, None, 1, D_MODEL), lambda i, be, bv: (layer, be[i], 0, 0)),
        ],
        out_specs=pl.BlockSpec((tr, row_w), lambda i, be, bv: (i, 0)),
        scratch_shapes=[pltpu.VMEM((D_MODEL, 2 * D_FF), BF16), pltpu.VMEM((D_FF, D_MODEL), BF16)],
    )
    return pl.pallas_call(
        functools.partial(_expert_kernel, tr=tr),
        grid_spec=grid_spec,
        out_shape=jax.ShapeDtypeStruct((n_rows, row_w), x_rows.dtype),
        compiler_params=pltpu.CompilerParams(
            dimension_semantics=("arbitrary",), vmem_limit_bytes=VMEM_LIMIT),
        name="moe_experts",
    )(blk_expert, blk_valid, x_rows, w_gu, b_gu.reshape(depth, N_EXPERTS, 1, 2 * D_FF), w_down,
      b_down.reshape(depth, N_EXPERTS, 1, D_MODEL))


def _combine_kernel(xn_ref, y4_ref, rt_ref, mod_ref, g_ref, o_ref, *, final):
    rt = rt_ref[...]
    acc = rt[:, RT_GATE:RT_GATE + 1] * _unpack_rows(y4_ref[0])
    for k in range(1, TOP_K):
        acc = acc + rt[:, RT_GATE + k:RT_GATE + k + 1] * _unpack_rows(y4_ref[k])
    x = xn_ref[...] + mod_ref[5:6, :] * acc
    if final:
        ms = jnp.mean(x * x, axis=-1, keepdims=True)
        x = x * lax.rsqrt(ms + EPS) * g_ref[...]
    o_ref[...] = x


def _combine(xn, y4, rt, mod_l, final_g, seq, final):
    n_tok = xn.shape[0]
    row_w = y4.shape[2]
    tm = min(TM_PROJ, seq)
    return pl.pallas_call(
        functools.partial(_combine_kernel, final=final),
        grid=(n_tok // tm,),
        in_specs=[
            pl.BlockSpec((tm, D_MODEL), lambda i: (i, 0)),
            pl.BlockSpec((TOP_K, tm, row_w), lambda i: (0, i, 0)),
            pl.BlockSpec((tm, LANES), lambda i: (i, 0)),
            pl.BlockSpec((None, SUBLANES, D_MODEL), lambda i: ((i * tm) // seq, 0, 0)),
            pl.BlockSpec((1, D_MODEL), lambda i: (0, 0)),
        ],
        out_specs=pl.BlockSpec((tm, D_MODEL), lambda i: (i, 0)),
        out_shape=jax.ShapeDtypeStruct((n_tok, D_MODEL), F32),
        compiler_params=pltpu.CompilerParams(
            dimension_semantics=("parallel",), vmem_limit_bytes=VMEM_LIMIT),
        name="moe_combine",
    )(xn, y4, rt, mod_l, final_g)


def _permute_w_in(w_in):
    off = 0
    gdn_qkvz = w_in[:, off:off + PG_W]; off += PG_W
    small_ba = w_in[:, off:off + 2 * GDN_HEADS]; off += 2 * GDN_HEADS
    ssd_zxbc = w_in[:, off:off + PS_W]; off += PS_W
    ssd_dt = w_in[:, off:off + SSD_HEADS]
    pad = jnp.zeros((w_in.shape[0], LANES - 2 * GDN_HEADS - SSD_HEADS), w_in.dtype)
    return jnp.concatenate([gdn_qkvz, ssd_zxbc, small_ba, ssd_dt, pad], axis=1).astype(BF16)


def _moe_schedule(rt, counts_row, n_rows):
    idx = rt[:, RT_IDX:RT_IDX + TOP_K].astype(jnp.int32)
    pos = rt[:, RT_POS:RT_POS + TOP_K].astype(jnp.int32)
    counts = counts_row[0, :N_EXPERTS].astype(jnp.int32)
    padded = (counts + ROW_BLOCK - 1) // ROW_BLOCK * ROW_BLOCK
    pad_end = jnp.cumsum(padded)
    pad_start = pad_end - padded
    experts = jnp.arange(N_EXPERTS, dtype=jnp.int32)
    dest = pos + jnp.sum(jnp.where(idx[..., None] == experts, pad_start, 0), axis=-1)
    blk_start = jnp.arange(n_rows // ROW_BLOCK, dtype=jnp.int32) * ROW_BLOCK
    blk_expert = jnp.minimum(jnp.sum(blk_start[:, None] >= pad_end[None, :], axis=1), N_EXPERTS - 1)
    blk_onehot = blk_expert[:, None] == experts
    blk_valid = (jnp.sum(jnp.where(blk_onehot, counts + pad_start, 0), axis=1) - blk_start)
    blk_valid = jnp.where(blk_start < pad_end[-1], jnp.clip(blk_valid, 0, ROW_BLOCK), 0)
    return dest, blk_expert.astype(jnp.int32), blk_valid.astype(jnp.int32)


def _forward(x, c, ada_w, ada_b, norm1_g, norm2_g, w_in, gdn_conv_w, gdn_a_log, gdn_dt_bias, gdn_norm_g,
             ssd_conv_w, ssd_conv_b, ssd_a_log, ssd_dt_bias, ssd_d, ssd_norm_g, w_out, router_w, router_b,
             moe_w_gu, moe_b_gu, moe_w_down, moe_b_down, final_g):
    bsz, seq, _ = x.shape
    depth = ada_w.shape[0]
    n_tok = bsz * seq
    n_rows = n_tok * TOP_K + N_EXPERTS * ROW_BLOCK
    mod = _ada_mod(c, ada_w, ada_b)
    x2 = x.reshape(n_tok, D_MODEL)
    fg = final_g.reshape(1, D_MODEL)
    for l in range(depth):
        conv_w = jnp.concatenate([gdn_conv_w[l], ssd_conv_w[l]], axis=1)
        conv_b = jnp.concatenate([jnp.zeros((1, GDN_CONV_CH), F32), ssd_conv_b[l].reshape(1, -1)], axis=1)
        pg, ps, pm = _inproj(x2, mod[l], norm1_g[l].reshape(1, D_MODEL), _permute_w_in(w_in[l]),
                             conv_w, conv_b, seq)
        go = _gdn(pg, pm, gdn_a_log[l], gdn_dt_bias[l], gdn_norm_g[l], bsz, seq)
        so = _ssd(ps, pm, ssd_a_log[l], ssd_dt_bias[l], ssd_d[l], ssd_norm_g[l], bsz, seq)
        rw = jnp.zeros((D_MODEL, LANES), F32).at[:, :N_EXPERTS].set(router_w[l])
        rw_hi = rw.astype(BF16)
        rw = jnp.concatenate([rw_hi, (rw - rw_hi.astype(F32)).astype(BF16)], axis=0)
        rb = jnp.full((1, LANES), -1e30, F32).at[0, :N_EXPERTS].set(router_b[l])
        xn, h2, rt, cnt = _outproj_router(go, so, x2, mod[l], norm2_g[l].reshape(1, D_MODEL),
                                          w_out[l].astype(BF16), rw, rb, seq)
        dest, blk_expert, blk_valid = _moe_schedule(rt, cnt, n_rows)
        dest_t = dest.T
        x_rows = _scatter_rows(h2, dest_t, n_rows)
        y_rows = _experts(x_rows, blk_expert, blk_valid, moe_w_gu, moe_b_gu, moe_w_down, moe_b_down, l)
        y4 = _gather_rows(y_rows, dest_t.reshape(TOP_K * n_tok)).reshape(TOP_K, n_tok, y_rows.shape[1])
        x2 = _combine(xn, y4, rt, mod[l], fg, seq, final=(l == depth - 1))
    return x2.reshape(bsz, seq, D_MODEL)


def kernel(x, c, ada_w, ada_b, norm1_g, norm2_g, w_in, gdn_conv_w, gdn_a_log, gdn_dt_bias, gdn_norm_g, ssd_conv_w, ssd_conv_b, ssd_a_log, ssd_dt_bias, ssd_d, ssd_norm_g, w_out, router_w, router_b, moe_w_gu, moe_b_gu, moe_w_down, moe_b_down, final_g):
    return _forward(x, c, ada_w, ada_b, norm1_g, norm2_g, w_in, gdn_conv_w, gdn_a_log, gdn_dt_bias, gdn_norm_g,
                    ssd_conv_w, ssd_conv_b, ssd_a_log, ssd_dt_bias, ssd_d, ssd_norm_g, w_out, router_w, router_b,
                    moe_w_gu, moe_b_gu, moe_w_down, moe_b_down, final_g)
```

```python
import functools

import jax
import jax.numpy as jnp
from jax import lax
from jax.experimental import pallas as pl
from jax.experimental.pallas import tpu as pltpu
from jax.experimental.pallas import tpu_sc as plsc

F32 = jnp.float32
BF16 = jnp.bfloat16
U32 = jnp.uint32
HIGHEST = lax.Precision.HIGHEST

D_MODEL = 1024
CHUNK = 64
CONV_K = 4
GDN_HEADS = 8
GDN_DK = 128
GDN_DV = 128
GDN_QK = GDN_HEADS * GDN_DK
GDN_V = GDN_HEADS * GDN_DV
GDN_CONV_CH = 2 * GDN_QK + GDN_V
SSD_P = 64
SSD_HEADS = 16
SSD_G = 2
SSD_N = 128
SSD_INNER = SSD_HEADS * SSD_P
SSD_BC = SSD_G * SSD_N
SSD_CONV_CH = SSD_INNER + 2 * SSD_BC
N_EXPERTS = 32
TOP_K = 4
D_FF = D_MODEL
SWIGLU_ALPHA = 1.702
SWIGLU_LIMIT = 7.0
EPS = 1e-6
LOG2_E = 1.4426950408889634

LANES = 128
SUBLANES = 8
PG_W = GDN_CONV_CH + GDN_V
PS_W = SSD_INNER + SSD_CONV_CH
PS_X = SSD_INNER
PS_B = 2 * SSD_INNER
PS_C = 2 * SSD_INNER + SSD_BC
N_CONV = GDN_CONV_CH + SSD_CONV_CH
LANE_BETA = 0
LANE_ALPHA = GDN_HEADS
LANE_DT = 2 * GDN_HEADS
VMEM_LIMIT = 56 * 1024 * 1024

TM_PROJ = 512
TM_ROUTER = 512
TT_SCAN = 512
MIX_CHUNKS = 4
SSD_STAGE_SPLIT = 4
ROW_BLOCK = 512
FF_CHUNK = 512
CONV_SLAB = 512
SC_WINDOW = 128
ROW_WORDS = D_MODEL // 2


def _dot(a, b):
    return jnp.dot(a, b, preferred_element_type=F32)


def _dot_nt(a, b):
    return lax.dot_general(a, b, (((1,), (1,)), ((), ())), preferred_element_type=F32)


def _dot_tn(a, b):
    return lax.dot_general(a, b, (((0,), (0,)), ((), ())), preferred_element_type=F32)


def _sigmoid(x):
    return 1.0 / (1.0 + jnp.exp(-x))


def _silu(x):
    h = 0.5 * x
    return h + h * jnp.tanh(h)


def _softplus(x):
    return jnp.maximum(x, 0.0) + jnp.log(1.0 + jnp.exp(-jnp.abs(x)))


def _pack_rows(h):
    w = h.shape[1] // 2
    hi = pltpu.bitcast(h[:, :w].astype(BF16).astype(F32), U32)
    lo = pltpu.bitcast(h[:, w:].astype(BF16).astype(F32), U32)
    return hi | (lo >> 16)


def _unpack_rows(p):
    hi = pltpu.bitcast(p & jnp.uint32(0xFFFF0000), F32)
    lo = pltpu.bitcast(p << 16, F32)
    return jnp.concatenate([hi, lo], axis=1)


def _ada_kernel(c_ref, w_ref, b_ref, o_ref):
    c = c_ref[...]
    o_ref[0] = jnp.dot(_silu(c), w_ref[0], precision=HIGHEST, preferred_element_type=F32) + b_ref[0]


def _ada_mod(c, ada_w, ada_b):
    depth = ada_w.shape[0]
    bsz = c.shape[0]
    c8 = jnp.zeros((SUBLANES, D_MODEL), F32).at[:bsz].set(c)
    out = pl.pallas_call(
        _ada_kernel,
        grid=(depth, 6),
        in_specs=[
            pl.BlockSpec((SUBLANES, D_MODEL), lambda l, j: (0, 0)),
            pl.BlockSpec((1, D_MODEL, D_MODEL), lambda l, j: (l, 0, j)),
            pl.BlockSpec((1, 1, D_MODEL), lambda l, j: (l, 0, j)),
        ],
        out_specs=pl.BlockSpec((1, SUBLANES, D_MODEL), lambda l, j: (l, 0, j)),
        out_shape=jax.ShapeDtypeStruct((depth, SUBLANES, 6 * D_MODEL), F32),
        name="ada_mod",
    )(c8, ada_w, ada_b.reshape(depth, 1, 6 * D_MODEL))
    mod = out[:, :bsz].reshape(depth, bsz, 6, D_MODEL)
    return jnp.concatenate([mod, jnp.zeros((depth, bsz, 2, D_MODEL), F32)], axis=2)


def _modulated_norm(x, g, shift, scale):
    ms = jnp.mean(x * x, axis=-1, keepdims=True)
    return (x * lax.rsqrt(ms + EPS) * g) * (1.0 + scale) + shift


def _inproj_kernel(x_ref, mod_ref, g_ref, w_ref, cw_ref, cb_ref, og_ref, os_ref, om_ref,
                   tail_ref, cbuf_ref, *, tm, seq):
    i = pl.program_id(0)

    @pl.when((i * tm) % seq == 0)
    def _():
        tail_ref[...] = jnp.zeros_like(tail_ref)

    h = _modulated_norm(x_ref[...], g_ref[...], mod_ref[0:1, :], mod_ref[1:2, :]).astype(BF16)

    def conv_slab(n, wcol, ccol, bias):
        cs = slice(ccol, ccol + CONV_SLAB)
        pbuf = cbuf_ref.at[n % 2, 0]
        bbuf = cbuf_ref.at[n % 2, 1]
        p = _dot(h, w_ref[:, wcol:wcol + CONV_SLAB])
        pbuf[0:SUBLANES, :] = tail_ref[0, :, cs]
        pbuf[SUBLANES:SUBLANES + tm, :] = p
        tail_ref[0, :, cs] = p[tm - SUBLANES:tm, :]
        p1 = pbuf[SUBLANES - 1:SUBLANES - 1 + tm, :]
        b = p * cw_ref[1:2, cs] + p1 * cw_ref[0:1, cs]
        bbuf[0:SUBLANES, :] = tail_ref[1, :, cs]
        bbuf[SUBLANES:SUBLANES + tm, :] = b
        tail_ref[1, :, cs] = b[tm - SUBLANES:tm, :]
        y = p * cw_ref[3:4, cs] + p1 * cw_ref[2:3, cs] + bbuf[SUBLANES - 2:SUBLANES - 2 + tm, :]
        if bias:
            y = y + cb_ref[:, cs]
        return _silu(y)

    n = 0
    for part, scale in ((0, GDN_DK ** -0.5), (1, 1.0)):
        for j in range(GDN_QK // CONV_SLAB):
            col = part * GDN_QK + j * CONV_SLAB
            y = conv_slab(n, col, col, False)
            n += 1
            for hh in range(CONV_SLAB // GDN_DK):
                yh = y[:, hh * GDN_DK:(hh + 1) * GDN_DK]
                inv = lax.rsqrt(jnp.sum(yh * yh, axis=-1, keepdims=True) + EPS) * scale
                og_ref[:, col + hh * GDN_DK:col + (hh + 1) * GDN_DK] = (yh * inv).astype(BF16)
    for j in range(GDN_V // CONV_SLAB):
        col = 2 * GDN_QK + j * CONV_SLAB
        og_ref[:, col:col + CONV_SLAB] = conv_slab(n, col, col, False).astype(BF16)
        n += 1
    og_ref[:, GDN_CONV_CH:PG_W] = _dot(h, w_ref[:, GDN_CONV_CH:PG_W]).astype(BF16)
    os_ref[:, 0:SSD_INNER] = _dot(h, w_ref[:, PG_W:PG_W + SSD_INNER]).astype(BF16)
    for j in range(SSD_CONV_CH // CONV_SLAB):
        col = j * CONV_SLAB
        os_ref[:, SSD_INNER + col:SSD_INNER + col + CONV_SLAB] = conv_slab(
            n, PG_W + SSD_INNER + col, GDN_CONV_CH + col, True).astype(BF16)
        n += 1
    om_ref[...] = _dot(h, w_ref[:, PG_W + PS_W:])


def _inproj(x2, mod_l, g, w_perm, conv_w, conv_b, seq):
    n_tok = x2.shape[0]
    tm = min(TM_PROJ, seq)
    wtot = w_perm.shape[1]
    return pl.pallas_call(
        functools.partial(_inproj_kernel, tm=tm, seq=seq),
        grid=(n_tok // tm,),
        in_specs=[
            pl.BlockSpec((tm, D_MODEL), lambda i: (i, 0)),
            pl.BlockSpec((None, SUBLANES, D_MODEL), lambda i: ((i * tm) // seq, 0, 0)),
            pl.BlockSpec((1, D_MODEL), lambda i: (0, 0)),
            pl.BlockSpec((D_MODEL, wtot), lambda i: (0, 0)),
            pl.BlockSpec((CONV_K, N_CONV), lambda i: (0, 0)),
            pl.BlockSpec((1, N_CONV), lambda i: (0, 0)),
        ],
        out_specs=[
            pl.BlockSpec((tm, PG_W), lambda i: (i, 0)),
            pl.BlockSpec((tm, PS_W), lambda i: (i, 0)),
            pl.BlockSpec((tm, LANES), lambda i: (i, 0)),
        ],
        out_shape=[
            jax.ShapeDtypeStruct((n_tok, PG_W), BF16),
            jax.ShapeDtypeStruct((n_tok, PS_W), BF16),
            jax.ShapeDtypeStruct((n_tok, LANES), F32),
        ],
        scratch_shapes=[
            pltpu.VMEM((2, SUBLANES, N_CONV), F32),
            pltpu.VMEM((2, 2, SUBLANES + tm, CONV_SLAB), F32),
        ],
        compiler_params=pltpu.CompilerParams(
            dimension_semantics=("arbitrary",), vmem_limit_bytes=VMEM_LIMIT),
        name="norm_inproj",
    )(x2, mod_l, g, w_perm, conv_w, conv_b)


def _tri_masks():
    row = lax.broadcasted_iota(jnp.int32, (CHUNK, CHUNK), 0)
    col = lax.broadcasted_iota(jnp.int32, (CHUNK, CHUNK), 1)
    return row, col


def _block_diag2(m, lo_half):
    zero = jnp.zeros_like(m)
    return jnp.concatenate([jnp.where(lo_half, m, zero), jnp.where(lo_half, zero, m)], axis=0)


def _unit_lower_inverses(nmats, row, col, lo_half):
    eye = (row == col).astype(F32)
    pair = (row >> 1) == (col >> 1)
    xs = [eye - jnp.where(pair, n, 0.0) for n in nmats]
    for lg in range(2, CHUNK.bit_length()):
        mask = ((row >> lg) == (col >> lg)) & ((row >> (lg - 1)) != (col >> (lg - 1)))
        xb = [x.astype(BF16) for x in xs]
        ys = [_dot(jnp.where(mask, n, 0.0).astype(BF16), _block_diag2(b, lo_half)) for n, b in zip(nmats, xb)]
        yield
        xs = [x - _dot(b, _block_diag2(y.astype(BF16), lo_half)) for x, b, y in zip(xs, xb, ys)]
        yield
    return xs


def _interleave(*stage_generators):
    live = list(stage_generators)
    while live:
        for gen in list(live):
            try:
                next(gen)
            except StopIteration:
                live.remove(gen)


def _gdn_program(pg_ref, pm_ref, nega_ref, dtb_ref, ng_ref, o_ref, s_ref, gq_scr, b_scr, o0_scr, el_scr):
    heads = range(GDN_HEADS)

    def init():
        s_ref[...] = jnp.zeros_like(s_ref)

    row, col = _tri_masks()
    tril = (row >= col).astype(F32)
    row2 = lax.broadcasted_iota(jnp.int32, (CHUNK, LANES), 0)
    lane2 = lax.broadcasted_iota(jnp.int32, (CHUNK, LANES), 1)
    col2 = lane2 & (CHUNK - 1)
    lo_half = lane2 < CHUNK
    lo_half1 = lax.broadcasted_iota(jnp.int32, (1, LANES), 1) < CHUNK
    incl2 = row2 >= col2
    strict2 = row2 > col2
    zero_k = jnp.zeros((CHUNK, GDN_DK), BF16)
    zero_r = jnp.zeros((CHUNK, 2 * GDN_DV), BF16)
    nega = nega_ref[...]
    dtb = dtb_ref[...]
    ng = ng_ref[...]

    def wide(arr, l0):
        n = arr.shape[0]
        return jnp.concatenate([jnp.broadcast_to(arr[:, l0:l0 + 1], (n, LANES)),
                                jnp.broadcast_to(arr[:, l0 + 1:l0 + 2], (n, LANES))], axis=1)

    def halves(w, sel):
        return jnp.where(sel, w[:, 0:LANES], w[:, LANES:])

    def prepare(ci):
        units = []
        for j in range(MIX_CHUNKS):
            c = ci * MIX_CHUNKS + j
            rows = pl.ds(pl.multiple_of(c * CHUNK, CHUNK), CHUNK)
            pmv = pm_ref[rows, :]
            beta_all = _sigmoid(pmv)
            g_all = nega * _softplus(pmv + dtb)
            gam = jnp.dot(tril, g_all, precision=HIGHEST, preferred_element_type=F32)
            gam_t = gam.T
            gam_last = gam[CHUNK - 1:CHUNK, :]
            el_scr[pl.ds(pl.multiple_of(c * SUBLANES, SUBLANES), SUBLANES), :] = jnp.broadcast_to(
                jnp.exp2(gam_last), (SUBLANES, LANES))
            for h0 in range(0, GDN_HEADS, 2):
                la = LANE_ALPHA + h0
                gam_w = wide(gam, la)
                r0_t = gam_t[la:la + 1, :]
                r1_t = gam_t[la + 1:la + 2, :]
                units.append(dict(
                    slot=c * GDN_HEADS + h0, rows=rows, h0=h0,
                    beta_w=wide(beta_all, LANE_BETA + h0), gam_w=gam_w, gl_w=wide(gam_last, la),
                    grow=jnp.where(lo_half1, jnp.concatenate([r0_t, r0_t], axis=1),
                                   jnp.concatenate([r1_t, r1_t], axis=1))))

        def cols(u, base):
            return pg_ref[u["rows"], base + u["h0"] * GDN_DK:base + (u["h0"] + 2) * GDN_DK]

        yield
        q2s = [cols(u, 0) for u in units]
        k2s = [cols(u, GDN_QK) for u in units]
        kbd = [jnp.concatenate([jnp.concatenate([k2[:, 0:GDN_DK], zero_k], axis=1),
                                jnp.concatenate([zero_k, k2[:, GDN_DK:]], axis=1)], axis=0) for k2 in k2s]
        qkks = [_dot_nt(jnp.concatenate([q2, k2], axis=0), r) for q2, k2, r in zip(q2s, k2s, kbd)]
        yield
        decays = [jnp.exp2(jnp.where(incl2, halves(u["gam_w"], lo_half) - u["grow"], -jnp.inf)) for u in units]
        nmats = [jnp.where(strict2, halves(u["beta_w"], lo_half) * qkk[CHUNK:2 * CHUNK, :] * d, 0.0)
                 for u, qkk, d in zip(units, qkks, decays)]
        amats = [(qkk[0:CHUNK, :] * d).astype(BF16) for qkk, d in zip(qkks, decays)]
        yield
        egws = [jnp.exp2(u["gam_w"]) for u in units]
        k2f = [k2.astype(F32) for k2 in k2s]
        kds = [(kf * jnp.exp2(u["gl_w"] - u["gam_w"])).astype(BF16) for kf, u in zip(k2f, units)]
        yield
        tinvs = yield from _unit_lower_inverses(nmats, row2, col2, lo_half)
        vbs = [(cols(u, 2 * GDN_QK).astype(F32) * u["beta_w"]).astype(BF16) for u in units]
        kbs = [(kf * (u["beta_w"] * e)).astype(BF16) for kf, u, e in zip(k2f, units, egws)]
        rhss = [jnp.concatenate(
            [jnp.concatenate([vb[:, 0:GDN_DV], kb[:, 0:GDN_DK], zero_r], axis=1),
             jnp.concatenate([zero_r, vb[:, GDN_DV:], kb[:, GDN_DK:]], axis=1)], axis=0)
            for vb, kb in zip(vbs, kbs)]
        yield
        sols = [_dot(x.astype(BF16), r).astype(BF16) for x, r in zip(tinvs, rhss)]
        yield
        sbd = [jnp.concatenate([jnp.concatenate([s[:, 0:2 * GDN_DV], zero_r], axis=1),
                                jnp.concatenate([zero_r, s[:, 2 * GDN_DV:]], axis=1)], axis=0) for s in sols]
        a_uw = [_dot(a, r) for a, r in zip(amats, sbd)]
        yield
        for hh in range(2):
            k_uw = [_dot_tn(kd[:, hh * GDN_DK:(hh + 1) * GDN_DK], s[:, hh * 2 * GDN_DV:(hh + 1) * 2 * GDN_DV])
                    for kd, s in zip(kds, sols)]
            yield
            for u, q2, e, au, ku in zip(units, q2s, egws, a_uw, k_uw):
                qe = (q2[:, hh * GDN_DK:(hh + 1) * GDN_DK].astype(F32) * e[:, hh * LANES:(hh + 1) * LANES])
                base = hh * 2 * GDN_DV
                o0_scr[u["slot"] + hh] = au[:, base:base + GDN_DV]
                b_scr[u["slot"] + hh] = ku[:, 0:GDN_DV]
                gq_scr[u["slot"] + hh] = jnp.concatenate(
                    [ku[:, GDN_DV:], qe - au[:, base + GDN_DV:base + 2 * GDN_DV]], axis=0).astype(BF16)
            yield

    def scan(c, carry):
        r0 = pl.multiple_of(c * CHUNK, CHUNK)
        e_last = el_scr[pl.ds(pl.multiple_of(c * SUBLANES, SUBLANES), SUBLANES), :][0:1, :]
        states = [s_ref[h] for h in heads]
        rs = [_dot(gq_scr[c * GDN_HEADS + h], states[h].astype(BF16)) for h in heads]
        for h in heads:
            s_ref[h] = (e_last[:, LANE_ALPHA + h:LANE_ALPHA + h + 1] * states[h]
                        - rs[h][0:GDN_DK, :] + b_scr[c * GDN_HEADS + h])
        for h in heads:
            o = rs[h][GDN_DK:, :] + o0_scr[c * GDN_HEADS + h]
            z = pg_ref[pl.ds(r0, CHUNK), GDN_CONV_CH + h * GDN_DV:GDN_CONV_CH + (h + 1) * GDN_DV].astype(F32)
            on = o * lax.rsqrt(jnp.mean(o * o, axis=-1, keepdims=True) + EPS) * ng
            o_ref[pl.ds(r0, CHUNK), h * GDN_DV:(h + 1) * GDN_DV] = (on * _silu(z)).astype(BF16)
        return carry

    return init, prepare, scan


def _ssd_program(ps_ref, pm_ref, nega_ref, dtb_ref, dsk_ref, ng_ref, o_ref, h_ref):
    def init():
        h_ref[...] = jnp.zeros_like(h_ref)

    row, col = _tri_masks()
    incl = row >= col
    tril = incl.astype(F32)
    nega = nega_ref[...]
    dtb = dtb_ref[...]
    dsk = dsk_ref[...]
    lane = lax.broadcasted_iota(jnp.int32, (CHUNK, LANES), 1)
    lo_half = lane < SSD_P
    incl2 = lax.broadcasted_iota(jnp.int32, (CHUNK, LANES), 0) >= (lane & (SSD_P - 1))
    lane1 = lax.broadcasted_iota(jnp.int32, (1, LANES), 1)
    lo_half1 = lane1 < SSD_P
    heads_per_group = SSD_HEADS // SSD_G
    gw = SSD_INNER // SSD_G

    def pair_cols(arr, l0):
        sel = lo_half if arr.shape[0] == CHUNK else lo_half1
        return jnp.where(sel, arr[:, l0:l0 + 1], arr[:, l0 + 1:l0 + 2])

    pairs_per_group = heads_per_group // 2

    def chunk_group(ci):
        groups = []
        for j in range(MIX_CHUNKS):
            c = ci * MIX_CHUNKS + j
            rows = pl.ds(pl.multiple_of(c * CHUNK, CHUNK), CHUNK)
            pmv = pm_ref[rows, :]
            dt_all = _softplus(pmv + dtb)
            acum = jnp.dot(tril, nega * dt_all, precision=HIGHEST, preferred_element_type=F32)
            info = dict(rows=rows, dt=dt_all, acum=acum, acum_t=acum.T, a_last=acum[CHUNK - 1:CHUNK, :])
            for g in range(SSD_G):
                groups.append(dict(info, g=g))
        yield
        for gr in groups:
            g = gr["g"]
            gr["bg"] = ps_ref[gr["rows"], PS_B + g * SSD_N:PS_B + (g + 1) * SSD_N]
            gr["cg"] = ps_ref[gr["rows"], PS_C + g * SSD_N:PS_C + (g + 1) * SSD_N]
        for gr in groups:
            gr["cb2"] = _dot_nt(gr["cg"], jnp.concatenate([gr["bg"], gr["bg"]], axis=0))
        yield

        units = [dict(gr=gr, p=p, l0=LANE_DT + gr["g"] * heads_per_group + 2 * p)
                 for gr in groups for p in range(pairs_per_group)]
        batches = [units[i:i + len(units) // SSD_STAGE_SPLIT] for i in range(0, len(units), len(units) // SSD_STAGE_SPLIT)]
        for batch in batches:
            for u in batch:
                gr, l0 = u["gr"], u["l0"]
                head0 = l0 - LANE_DT
                u["x"] = ps_ref[gr["rows"], PS_X + head0 * SSD_P:PS_X + (head0 + 2) * SSD_P].astype(F32)
                u["ac"] = pair_cols(gr["acum"], l0)
                u["al"] = pair_cols(gr["a_last"], l0)
                r0_t = gr["acum_t"][l0:l0 + 1, :]
                r1_t = gr["acum_t"][l0 + 1:l0 + 2, :]
                u["arow"] = jnp.where(lo_half1, jnp.concatenate([r0_t, r0_t], axis=1),
                                      jnp.concatenate([r1_t, r1_t], axis=1))
                u["xdt"] = u["x"] * pair_cols(gr["dt"], l0)
            yield
        for batch in batches:
            for u in batch:
                decay = jnp.exp2(jnp.where(incl2, u["ac"] - u["arow"], -jnp.inf))
                u["m"] = (u["gr"]["cb2"] * decay).astype(BF16)
                u["rhs"] = _block_diag2(u["xdt"], lo_half).astype(BF16)
            yield
        for batch in batches:
            for u in batch:
                u["y_diag"] = _dot(u["m"], u["rhs"])
                u["xw"] = (u["xdt"] * jnp.exp2(u["al"] - u["ac"])).astype(BF16)
            yield
        for gr in groups:
            mine = [u for u in units if u["gr"] is gr]
            gr["upd"] = _dot_tn(gr["bg"], jnp.concatenate([u["xw"] for u in mine], axis=1))
            gr["scale"] = jnp.concatenate([jnp.exp2(u["al"]) for u in mine], axis=1)
        yield

        states = [h_ref[g] for g in range(SSD_G)]
        for gr in groups:
            gr["y_off"] = _dot(gr["cg"], states[gr["g"]].astype(BF16))
            states[gr["g"]] = gr["scale"] * states[gr["g"]] + gr["upd"]
        for g in range(SSD_G):
            h_ref[g] = states[g]
        yield

        for gr in groups:
            g = gr["g"]
            mine = [u for u in units if u["gr"] is gr]
            y = jnp.concatenate(
                [u["y_diag"] + jnp.exp2(u["ac"]) * gr["y_off"][:, u["p"] * LANES:(u["p"] + 1) * LANES]
                 + pair_cols(dsk, u["l0"]) * u["x"] for u in mine], axis=1)
            z = ps_ref[gr["rows"], g * gw:(g + 1) * gw].astype(F32)
            yz = y * _silu(z)
            yn = yz * lax.rsqrt(jnp.mean(yz * yz, axis=-1, keepdims=True) + EPS)
            o_ref[gr["rows"], g * gw:(g + 1) * gw] = (yn * ng_ref[:, g * gw:(g + 1) * gw]).astype(BF16)
            yield

    return init, chunk_group


def _mixer_kernel(pg_ref, ps_ref, pm_ref, g_nega, g_dtb, g_ng, s_nega, s_dtb, s_dsk, s_ng, go_ref, so_ref,
                  s_ref, gq_scr, b_scr, o0_scr, el_scr, h_ref, *, tt):
    g_init, g_prepare, g_scan = _gdn_program(pg_ref, pm_ref, g_nega, g_dtb, g_ng, go_ref, s_ref,
                                             gq_scr, b_scr, o0_scr, el_scr)
    s_init, s_chunks = _ssd_program(ps_ref, pm_ref, s_nega, s_dtb, s_dsk, s_ng, so_ref, h_ref)

    @pl.when(pl.program_id(1) == 0)
    def _():
        g_init()
        s_init()

    def body(ci, carry):
        _interleave(g_prepare(ci), s_chunks(ci))
        return carry

    lax.fori_loop(0, tt // CHUNK // MIX_CHUNKS, body, 0)
    lax.fori_loop(0, tt // CHUNK, g_scan, 0, unroll=2)


def _mixer(pg, ps, pm, gdn_a_log, gdn_dt_bias, gdn_norm_g, ssd_a_log, ssd_dt_bias, ssd_d, ssd_norm_g, bsz, seq):
    tt = min(TT_SCAN, seq)
    nt = seq // tt
    units = tt // CHUNK * GDN_HEADS

    def lanes(v, lane0):
        return jnp.zeros((1, LANES), F32).at[0, lane0:lane0 + v.shape[0]].set(v)

    def tile(width):
        return pl.BlockSpec((tt, width), lambda b, t: (b * nt + t, 0))

    def const(width):
        return pl.BlockSpec((1, width), lambda b, t: (0, 0))

    return pl.pallas_call(
        functools.partial(_mixer_kernel, tt=tt),
        grid=(bsz, nt),
        in_specs=[tile(PG_W), tile(PS_W), tile(LANES), const(LANES), const(LANES), const(GDN_DV),
                  const(LANES), const(LANES), const(LANES), const(SSD_INNER)],
        out_specs=[tile(GDN_V), tile(SSD_INNER)],
        out_shape=[jax.ShapeDtypeStruct((bsz * seq, GDN_V), BF16),
                   jax.ShapeDtypeStruct((bsz * seq, SSD_INNER), BF16)],
        scratch_shapes=[
            pltpu.VMEM((GDN_HEADS, GDN_DK, GDN_DV), F32),
            pltpu.VMEM((units, GDN_DK + CHUNK, GDN_DK), BF16),
            pltpu.VMEM((units, GDN_DK, GDN_DV), F32),
            pltpu.VMEM((units, CHUNK, GDN_DV), F32),
            pltpu.VMEM((tt // CHUNK * SUBLANES, LANES), F32),
            pltpu.VMEM((SSD_G, SSD_N, SSD_INNER // SSD_G), F32),
        ],
        compiler_params=pltpu.CompilerParams(
            dimension_semantics=("parallel", "arbitrary"), vmem_limit_bytes=VMEM_LIMIT),
        name="mixer_scan",
    )(pg, ps, pm, lanes(-LOG2_E * jnp.exp(gdn_a_log), LANE_ALPHA), lanes(gdn_dt_bias, LANE_ALPHA),
      gdn_norm_g.reshape(1, GDN_DV), lanes(-LOG2_E * jnp.exp(ssd_a_log), LANE_DT), lanes(ssd_dt_bias, LANE_DT),
      lanes(ssd_d, LANE_DT), ssd_norm_g.reshape(1, SSD_INNER))


RT_IDX = 0
RT_GATE = TOP_K
RT_POS = 2 * TOP_K


def _outproj_router_kernel(go_ref, so_ref, x_ref, mod_ref, g_ref, wo_ref, rw_ref, rb_ref,
                           xn_ref, h_ref, rt_ref, cnt_ref, run_ref, *, tm):
    i = pl.program_id(0)

    @pl.when(i == 0)
    def _():
        run_ref[...] = jnp.zeros_like(run_ref)

    mix = _dot(go_ref[...], wo_ref[0:GDN_V, :]) + _dot(so_ref[...], wo_ref[GDN_V:, :])
    xn = x_ref[...] + mod_ref[2:3, :] * mix
    xn_ref[...] = xn
    h = _modulated_norm(xn, g_ref[...], mod_ref[3:4, :], mod_ref[4:5, :])
    h_ref[...] = _pack_rows(h)
    h_hi = h.astype(BF16)
    h_lo = (h - h_hi.astype(F32)).astype(BF16)
    hw = _dot(jnp.concatenate([h_hi, h_lo], axis=0), rw_ref[0:D_MODEL, :])
    logits = hw[0:tm, :] + hw[tm:, :] + _dot(h_hi, rw_ref[D_MODEL:, :]) + rb_ref[...]
    lane = lax.broadcasted_iota(jnp.int32, (tm, LANES), 1).astype(F32)
    work = logits
    tops = []
    idxs = []
    for _ in range(TOP_K):
        m = jnp.max(work, axis=-1, keepdims=True)
        idx = jnp.min(jnp.where(work == m, lane, float(LANES)), axis=-1, keepdims=True)
        work = jnp.where(lane == idx, -jnp.inf, work)
        tops.append(m)
        idxs.append(idx)
    exps = [jnp.exp(m - tops[0]) for m in tops]
    denom = exps[0] + exps[1] + exps[2] + exps[3]
    onehot = jnp.zeros((tm, LANES), F32)
    for idx in idxs:
        onehot = onehot + (lane == idx).astype(F32)
    trow = lax.broadcasted_iota(jnp.int32, (tm, tm), 0)
    tcol = lax.broadcasted_iota(jnp.int32, (tm, tm), 1)
    before = _dot((trow > tcol).astype(BF16), onehot.astype(BF16)) + run_ref[0:1, :]
    rt = jnp.zeros((tm, LANES), F32)
    for k in range(TOP_K):
        pos = jnp.sum(jnp.where(lane == idxs[k], before, 0.0), axis=-1, keepdims=True)
        rt = jnp.where(lane == RT_IDX + k, idxs[k], rt)
        rt = jnp.where(lane == RT_GATE + k, exps[k] / denom, rt)
        rt = jnp.where(lane == RT_POS + k, pos, rt)
    rt_ref[...] = rt
    run = run_ref[0:1, :] + jnp.sum(onehot, axis=0, keepdims=True)
    run_ref[...] = jnp.broadcast_to(run, run_ref.shape)
    cnt_ref[...] = jnp.broadcast_to(run, cnt_ref.shape)


def _outproj_router(go, so, x2, mod_l, g, w_out, rw, rb, seq):
    n_tok = x2.shape[0]
    tm = min(TM_ROUTER, seq)
    return pl.pallas_call(
        functools.partial(_outproj_router_kernel, tm=tm),
        grid=(n_tok // tm,),
        in_specs=[
            pl.BlockSpec((tm, GDN_V), lambda i: (i, 0)),
            pl.BlockSpec((tm, SSD_INNER), lambda i: (i, 0)),
            pl.BlockSpec((tm, D_MODEL), lambda i: (i, 0)),
            pl.BlockSpec((None, SUBLANES, D_MODEL), lambda i: ((i * tm) // seq, 0, 0)),
            pl.BlockSpec((1, D_MODEL), lambda i: (0, 0)),
            pl.BlockSpec((GDN_V + SSD_INNER, D_MODEL), lambda i: (0, 0)),
            pl.BlockSpec((2 * D_MODEL, LANES), lambda i: (0, 0)),
            pl.BlockSpec((1, LANES), lambda i: (0, 0)),
        ],
        out_specs=[
            pl.BlockSpec((tm, D_MODEL), lambda i: (i, 0)),
            pl.BlockSpec((tm, ROW_WORDS), lambda i: (i, 0)),
            pl.BlockSpec((tm, LANES), lambda i: (i, 0)),
            pl.BlockSpec((SUBLANES, LANES), lambda i: (0, 0)),
        ],
        out_shape=[
            jax.ShapeDtypeStruct((n_tok, D_MODEL), F32),
            jax.ShapeDtypeStruct((n_tok, ROW_WORDS), U32),
            jax.ShapeDtypeStruct((n_tok, LANES), F32),
            jax.ShapeDtypeStruct((SUBLANES, LANES), F32),
        ],
        scratch_shapes=[pltpu.VMEM((SUBLANES, LANES), F32)],
        compiler_params=pltpu.CompilerParams(
            dimension_semantics=("arbitrary",), vmem_limit_bytes=VMEM_LIMIT),
        name="outproj_router",
    )(go, so, x2, mod_l, g, w_out, rw, rb)


def _sc_workers():
    info = plsc.get_sparse_core_info()
    return info.num_cores, info.num_subcores


def _scatter_rows(src, dest_t, n_rows):
    n_tok, width = src.shape
    n_k = dest_t.shape[0]
    nc, ns = _sc_workers()
    per_w = n_tok // (nc * ns)
    win = min(SC_WINDOW, per_w)
    mesh = plsc.VectorSubcoreMesh(core_axis_name="c", subcore_axis_name="s")

    @functools.partial(
        pl.kernel, mesh=mesh,
        out_type=jax.ShapeDtypeStruct((n_rows, width), src.dtype),
        scratch_types=[pltpu.VMEM((win,), jnp.int32), pltpu.VMEM((win, width), src.dtype)],
    )
    def scatter_kernel(src_hbm, idx_hbm, out_hbm, idx_v, rows_v):
        wid = lax.axis_index("s") * nc + lax.axis_index("c")
        base = wid * per_w

        @pl.loop(0, per_w // win)
        def _(j):
            off = base + j * win
            pltpu.sync_copy(src_hbm.at[pl.ds(off, win)], rows_v)
            for k in range(n_k):
                pltpu.sync_copy(idx_hbm.at[k, pl.ds(off, win)], idx_v)
                pltpu.sync_copy(rows_v, out_hbm.at[idx_v])

    return scatter_kernel(src, dest_t)


def _gather_rows(table, idx):
    n_idx = idx.shape[0]
    width = table.shape[1]
    nc, ns = _sc_workers()
    per_w = n_idx // (nc * ns)
    win = min(SC_WINDOW, per_w)
    mesh = plsc.VectorSubcoreMesh(core_axis_name="c", subcore_axis_name="s")

    @functools.partial(
        pl.kernel, mesh=mesh,
        out_type=jax.ShapeDtypeStruct((n_idx, width), table.dtype),
        scratch_types=[pltpu.VMEM((win,), jnp.int32), pltpu.VMEM((win, width), table.dtype)],
    )
    def gather_kernel(table_hbm, idx_hbm, out_hbm, idx_v, rows_v):
        wid = lax.axis_index("s") * nc + lax.axis_index("c")
        base = wid * per_w

        @pl.loop(0, per_w // win)
        def _(j):
            off = base + j * win
            pltpu.sync_copy(idx_hbm.at[pl.ds(off, win)], idx_v)
            pltpu.sync_copy(table_hbm.at[idx_v], rows_v)
            pltpu.sync_copy(rows_v, out_hbm.at[pl.ds(off, win)])

    return gather_kernel(table, idx)


def _expert_kernel(be_ref, bv_ref, x_ref, wgu_ref, bgu_ref, wd_ref, bd_ref, y_ref, wgu_b, wd_b, *, tr):
    i = pl.program_id(0)
    e = be_ref[i]
    prev = be_ref[jnp.maximum(i - 1, 0)]

    @pl.when((i == 0) | (e != prev))
    def _():
        wgu_b[...] = wgu_ref[...].astype(BF16)
        wd_b[...] = wd_ref[...].astype(BF16)

    valid = bv_ref[i]

    @pl.when(valid == 0)
    def _():
        y_ref[...] = jnp.zeros_like(y_ref)

    @pl.when(valid > 0)
    def _():
        rows = lax.broadcasted_iota(jnp.int32, (tr, D_MODEL), 0)
        x = jnp.where(rows < valid, _unpack_rows(x_ref[...]), 0.0).astype(BF16)
        acc = jnp.zeros((tr, D_MODEL), F32)
        for f in range(0, D_FF, FF_CHUNK):
            gate = _dot(x, wgu_b[:, f:f + FF_CHUNK]) + bgu_ref[:, f:f + FF_CHUNK]
            up = _dot(x, wgu_b[:, D_FF + f:D_FF + f + FF_CHUNK]) + bgu_ref[:, D_FF + f:D_FF + f + FF_CHUNK]
            gate = jnp.minimum(gate, SWIGLU_LIMIT)
            up = jnp.clip(up, -SWIGLU_LIMIT, SWIGLU_LIMIT)
            act = (up + 1.0) * (gate * _sigmoid(gate * SWIGLU_ALPHA))
            acc = acc + _dot(act.astype(BF16), wd_b[f:f + FF_CHUNK, :])
        y_ref[...] = _pack_rows(acc + bd_ref[...])


def _experts(x_rows, blk_expert, blk_valid, w_gu, b_gu, w_down, b_down, layer):
    n_rows, row_w = x_rows.shape
    depth = w_gu.shape[0]
    tr = ROW_BLOCK
    nb = n_rows // tr
    grid_spec = pltpu.PrefetchScalarGridSpec(
        num_scalar_prefetch=2,
        grid=(nb,),
        in_specs=[
            pl.BlockSpec((tr, row_w), lambda i, be, bv: (i, 0)),
            pl.BlockSpec((None, None, D_MODEL, 2 * D_FF), lambda i, be, bv: (layer, be[i], 0, 0)),
            pl.BlockSpec((None, None, 1, 2 * D_FF), lambda i, be, bv: (layer, be[i], 0, 0)),
            pl.BlockSpec((None, None, D_FF, D_MODEL), lambda i, be, bv: (layer, be[i], 0, 0)),
            pl.BlockSpec((None, None, 1, D_MODEL), lambda i, be, bv: (layer, be[i], 0, 0)),
        ],
        out_specs=pl.BlockSpec((tr, row_w), lambda i, be, bv: (i, 0)),
        scratch_shapes=[pltpu.VMEM((D_MODEL, 2 * D_FF), BF16), pltpu.VMEM((D_FF, D_MODEL), BF16)],
    )
    return pl.pallas_call(
        functools.partial(_expert_kernel, tr=tr),
        grid_spec=grid_spec,
        out_shape=jax.ShapeDtypeStruct((n_rows, row_w), x_rows.dtype),
        compiler_params=pltpu.CompilerParams(
            dimension_semantics=("arbitrary",), vmem_limit_bytes=VMEM_LIMIT),
        name="moe_experts",
    )(blk_expert, blk_valid, x_rows, w_gu, b_gu.reshape(depth, N_EXPERTS, 1, 2 * D_FF), w_down,
      b_down.reshape(depth, N_EXPERTS, 1, D_MODEL))


def _combine_kernel(xn_ref, y4_ref, rt_ref, mod_ref, g_ref, o_ref, *, final):
    rt = rt_ref[...]
    acc = rt[:, RT_GATE:RT_GATE + 1] * _unpack_rows(y4_ref[0])
    for k in range(1, TOP_K):
        acc = acc + rt[:, RT_GATE + k:RT_GATE + k + 1] * _unpack_rows(y4_ref[k])
    x = xn_ref[...] + mod_ref[5:6, :] * acc
    if final:
        ms = jnp.mean(x * x, axis=-1, keepdims=True)
        x = x * lax.rsqrt(ms + EPS) * g_ref[...]
    o_ref[...] = x


def _combine(xn, y4, rt, mod_l, final_g, seq, final):
    n_tok = xn.shape[0]
    row_w = y4.shape[2]
    tm = min(TM_PROJ, seq)
    return pl.pallas_call(
        functools.partial(_combine_kernel, final=final),
        grid=(n_tok // tm,),
        in_specs=[
            pl.BlockSpec((tm, D_MODEL), lambda i: (i, 0)),
            pl.BlockSpec((TOP_K, tm, row_w), lambda i: (0, i, 0)),
            pl.BlockSpec((tm, LANES), lambda i: (i, 0)),
            pl.BlockSpec((None, SUBLANES, D_MODEL), lambda i: ((i * tm) // seq, 0, 0)),
            pl.BlockSpec((1, D_MODEL), lambda i: (0, 0)),
        ],
        out_specs=pl.BlockSpec((tm, D_MODEL), lambda i: (i, 0)),
        out_shape=jax.ShapeDtypeStruct((n_tok, D_MODEL), F32),
        compiler_params=pltpu.CompilerParams(
            dimension_semantics=("parallel",), vmem_limit_bytes=VMEM_LIMIT),
        name="moe_combine",
    )(xn, y4, rt, mod_l, final_g)


def _permute_w_in(w_in):
    off = 0
    gdn_qkvz = w_in[:, off:off + PG_W]; off += PG_W
    small_ba = w_in[:, off:off + 2 * GDN_HEADS]; off += 2 * GDN_HEADS
    ssd_zxbc = w_in[:, off:off + PS_W]; off += PS_W
    ssd_dt = w_in[:, off:off + SSD_HEADS]
    pad = jnp.zeros((w_in.shape[0], LANES - 2 * GDN_HEADS - SSD_HEADS), w_in.dtype)
    return jnp.concatenate([gdn_qkvz, ssd_zxbc, small_ba, ssd_dt, pad], axis=1).astype(BF16)


def _moe_schedule(rt, counts_row, n_rows):
    idx = rt[:, RT_IDX:RT_IDX + TOP_K].astype(jnp.int32)
    pos = rt[:, RT_POS:RT_POS + TOP_K].astype(jnp.int32)
    counts = counts_row[0, :N_EXPERTS].astype(jnp.int32)
    padded = (counts + ROW_BLOCK - 1) // ROW_BLOCK * ROW_BLOCK
    pad_end = jnp.cumsum(padded)
    pad_start = pad_end - padded
    experts = jnp.arange(N_EXPERTS, dtype=jnp.int32)
    dest = pos + jnp.sum(jnp.where(idx[..., None] == experts, pad_start, 0), axis=-1)
    blk_start = jnp.arange(n_rows // ROW_BLOCK, dtype=jnp.int32) * ROW_BLOCK
    blk_expert = jnp.minimum(jnp.sum(blk_start[:, None] >= pad_end[None, :], axis=1), N_EXPERTS - 1)
    blk_onehot = blk_expert[:, None] == experts
    blk_valid = (jnp.sum(jnp.where(blk_onehot, counts + pad_start, 0), axis=1) - blk_start)
    blk_valid = jnp.where(blk_start < pad_end[-1], jnp.clip(blk_valid, 0, ROW_BLOCK), 0)
    return dest, blk_expert.astype(jnp.int32), blk_valid.astype(jnp.int32)


def _forward(x, c, ada_w, ada_b, norm1_g, norm2_g, w_in, gdn_conv_w, gdn_a_log, gdn_dt_bias, gdn_norm_g,
             ssd_conv_w, ssd_conv_b, ssd_a_log, ssd_dt_bias, ssd_d, ssd_norm_g, w_out, router_w, router_b,
             moe_w_gu, moe_b_gu, moe_w_down, moe_b_down, final_g):
    bsz, seq, _ = x.shape
    depth = ada_w.shape[0]
    n_tok = bsz * seq
    n_rows = n_tok * TOP_K + N_EXPERTS * ROW_BLOCK
    mod = _ada_mod(c, ada_w, ada_b)
    x2 = x.reshape(n_tok, D_MODEL)
    fg = final_g.reshape(1, D_MODEL)
    for l in range(depth):
        conv_w = jnp.concatenate([gdn_conv_w[l], ssd_conv_w[l]], axis=1)
        conv_b = jnp.concatenate([jnp.zeros((1, GDN_CONV_CH), F32), ssd_conv_b[l].reshape(1, -1)], axis=1)
        pg, ps, pm = _inproj(x2, mod[l], norm1_g[l].reshape(1, D_MODEL), _permute_w_in(w_in[l]),
                             conv_w, conv_b, seq)
        go, so = _mixer(pg, ps, pm, gdn_a_log[l], gdn_dt_bias[l], gdn_norm_g[l],
                        ssd_a_log[l], ssd_dt_bias[l], ssd_d[l], ssd_norm_g[l], bsz, seq)
        rw = jnp.zeros((D_MODEL, LANES), F32).at[:, :N_EXPERTS].set(router_w[l])
        rw_hi = rw.astype(BF16)
        rw = jnp.concatenate([rw_hi, (rw - rw_hi.astype(F32)).astype(BF16)], axis=0)
        rb = jnp.full((1, LANES), -1e30, F32).at[0, :N_EXPERTS].set(router_b[l])
        xn, h2, rt, cnt = _outproj_router(go, so, x2, mod[l], norm2_g[l].reshape(1, D_MODEL),
                                          w_out[l].astype(BF16), rw, rb, seq)
        dest, blk_expert, blk_valid = _moe_schedule(rt, cnt, n_rows)
        dest_t = dest.T
        x_rows = _scatter_rows(h2, dest_t, n_rows)
        y_rows = _experts(x_rows, blk_expert, blk_valid, moe_w_gu, moe_b_gu, moe_w_down, moe_b_down, l)
        y4 = _gather_rows(y_rows, dest_t.reshape(TOP_K * n_tok)).reshape(TOP_K, n_tok, y_rows.shape[1])
        x2 = _combine(xn, y4, rt, mod[l], fg, seq, final=(l == depth - 1))
    return x2.reshape(bsz, seq, D_MODEL)


def kernel(x, c, ada_w, ada_b, norm1_g, norm2_g, w_in, gdn_conv_w, gdn_a_log, gdn_dt_bias, gdn_norm_g, ssd_conv_w, ssd_conv_b, ssd_a_log, ssd_dt_bias, ssd_d, ssd_norm_g, w_out, router_w, router_b, moe_w_gu, moe_b_gu, moe_w_down, moe_b_down, final_g):
    return _forward(x, c, ada_w, ada_b, norm1_g, norm2_g, w_in, gdn_conv_w, gdn_a_log, gdn_dt_bias, gdn_norm_g,
                    ssd_conv_w, ssd_conv_b, ssd_a_log, ssd_dt_bias, ssd_d, ssd_norm_g, w_out, router_w, router_b,
                    moe_w_gu, moe_b_gu, moe_w_down, moe_b_down, final_g)
```

```python
import functools

import jax
import jax.numpy as jnp
from jax import lax
from jax.experimental import pallas as pl
from jax.experimental.pallas import tpu as pltpu
from jax.experimental.pallas import tpu_sc as plsc

F32 = jnp.float32
BF16 = jnp.bfloat16
U32 = jnp.uint32
HIGHEST = lax.Precision.HIGHEST

D_MODEL = 1024
CHUNK = 64
CONV_K = 4
GDN_HEADS = 8
GDN_DK = 128
GDN_DV = 128
GDN_QK = GDN_HEADS * GDN_DK
GDN_V = GDN_HEADS * GDN_DV
GDN_CONV_CH = 2 * GDN_QK + GDN_V
SSD_P = 64
SSD_HEADS = 16
SSD_G = 2
SSD_N = 128
SSD_INNER = SSD_HEADS * SSD_P
SSD_BC = SSD_G * SSD_N
SSD_CONV_CH = SSD_INNER + 2 * SSD_BC
N_EXPERTS = 32
TOP_K = 4
D_FF = D_MODEL
SWIGLU_ALPHA = 1.702
SWIGLU_LIMIT = 7.0
EPS = 1e-6
LOG2_E = 1.4426950408889634

LANES = 128
SUBLANES = 8
PG_W = GDN_CONV_CH + GDN_V
PS_W = SSD_INNER + SSD_CONV_CH
PS_X = SSD_INNER
PS_B = 2 * SSD_INNER
PS_C = 2 * SSD_INNER + SSD_BC
N_CONV = GDN_CONV_CH + SSD_CONV_CH
LANE_BETA = 0
LANE_ALPHA = GDN_HEADS
LANE_DT = 2 * GDN_HEADS
VMEM_LIMIT = 56 * 1024 * 1024

TM_PROJ = 512
TM_ROUTER = 512
ROUTER_SPLIT = 2
TT_SCAN = 512
MIX_CHUNKS = 4
SSD_STAGE_SPLIT = 4
ROW_BLOCK = 512
FF_CHUNK = 512
CONV_SLAB = 512
SC_WINDOW = 128
ROW_WORDS = D_MODEL // 2


def _dot(a, b):
    return jnp.dot(a, b, preferred_element_type=F32)


def _dot_nt(a, b):
    return lax.dot_general(a, b, (((1,), (1,)), ((), ())), preferred_element_type=F32)


def _dot_tn(a, b):
    return lax.dot_general(a, b, (((0,), (0,)), ((), ())), preferred_element_type=F32)


def _sigmoid(x):
    return 1.0 / (1.0 + jnp.exp(-x))


def _silu(x):
    h = 0.5 * x
    return h + h * jnp.tanh(h)


def _softplus(x):
    return jnp.maximum(x, 0.0) + jnp.log(1.0 + jnp.exp(-jnp.abs(x)))


def _pack_rows(h):
    w = h.shape[1] // 2
    hi = pltpu.bitcast(h[:, :w].astype(BF16).astype(F32), U32)
    lo = pltpu.bitcast(h[:, w:].astype(BF16).astype(F32), U32)
    return hi | (lo >> 16)


def _unpack_rows(p):
    hi = pltpu.bitcast(p & jnp.uint32(0xFFFF0000), F32)
    lo = pltpu.bitcast(p << 16, F32)
    return jnp.concatenate([hi, lo], axis=1)


def _ada_kernel(c_ref, w_ref, b_ref, o_ref):
    c = c_ref[...]
    o_ref[0] = jnp.dot(_silu(c), w_ref[0], precision=HIGHEST, preferred_element_type=F32) + b_ref[0]


def _ada_mod(c, ada_w, ada_b):
    depth = ada_w.shape[0]
    bsz = c.shape[0]
    c8 = jnp.zeros((SUBLANES, D_MODEL), F32).at[:bsz].set(c)
    out = pl.pallas_call(
        _ada_kernel,
        grid=(depth, 6),
        in_specs=[
            pl.BlockSpec((SUBLANES, D_MODEL), lambda l, j: (0, 0)),
            pl.BlockSpec((1, D_MODEL, D_MODEL), lambda l, j: (l, 0, j)),
            pl.BlockSpec((1, 1, D_MODEL), lambda l, j: (l, 0, j)),
        ],
        out_specs=pl.BlockSpec((1, SUBLANES, D_MODEL), lambda l, j: (l, 0, j)),
        out_shape=jax.ShapeDtypeStruct((depth, SUBLANES, 6 * D_MODEL), F32),
        name="ada_mod",
    )(c8, ada_w, ada_b.reshape(depth, 1, 6 * D_MODEL))
    mod = out[:, :bsz].reshape(depth, bsz, 6, D_MODEL)
    return jnp.concatenate([mod, jnp.zeros((depth, bsz, 2, D_MODEL), F32)], axis=2)


def _modulated_norm(x, g, shift, scale):
    ms = jnp.mean(x * x, axis=-1, keepdims=True)
    return (x * lax.rsqrt(ms + EPS) * g) * (1.0 + scale) + shift


def _inproj_kernel(x_ref, mod_ref, g_ref, w_ref, cw_ref, cb_ref, og_ref, os_ref, om_ref,
                   tail_ref, cbuf_ref, *, tm, seq):
    i = pl.program_id(0)

    @pl.when((i * tm) % seq == 0)
    def _():
        tail_ref[...] = jnp.zeros_like(tail_ref)

    h = _modulated_norm(x_ref[...], g_ref[...], mod_ref[0:1, :], mod_ref[1:2, :]).astype(BF16)

    def conv_slab(n, wcol, ccol, bias):
        cs = slice(ccol, ccol + CONV_SLAB)
        pbuf = cbuf_ref.at[n % 2, 0]
        bbuf = cbuf_ref.at[n % 2, 1]
        p = _dot(h, w_ref[:, wcol:wcol + CONV_SLAB])
        pbuf[0:SUBLANES, :] = tail_ref[0, :, cs]
        pbuf[SUBLANES:SUBLANES + tm, :] = p
        tail_ref[0, :, cs] = p[tm - SUBLANES:tm, :]
        p1 = pbuf[SUBLANES - 1:SUBLANES - 1 + tm, :]
        b = p * cw_ref[1:2, cs] + p1 * cw_ref[0:1, cs]
        bbuf[0:SUBLANES, :] = tail_ref[1, :, cs]
        bbuf[SUBLANES:SUBLANES + tm, :] = b
        tail_ref[1, :, cs] = b[tm - SUBLANES:tm, :]
        y = p * cw_ref[3:4, cs] + p1 * cw_ref[2:3, cs] + bbuf[SUBLANES - 2:SUBLANES - 2 + tm, :]
        if bias:
            y = y + cb_ref[:, cs]
        return _silu(y)

    n = 0
    for part, scale in ((0, GDN_DK ** -0.5), (1, 1.0)):
        for j in range(GDN_QK // CONV_SLAB):
            col = part * GDN_QK + j * CONV_SLAB
            y = conv_slab(n, col, col, False)
            n += 1
            for hh in range(CONV_SLAB // GDN_DK):
                yh = y[:, hh * GDN_DK:(hh + 1) * GDN_DK]
                inv = lax.rsqrt(jnp.sum(yh * yh, axis=-1, keepdims=True) + EPS) * scale
                og_ref[:, col + hh * GDN_DK:col + (hh + 1) * GDN_DK] = (yh * inv).astype(BF16)
    for j in range(GDN_V // CONV_SLAB):
        col = 2 * GDN_QK + j * CONV_SLAB
        og_ref[:, col:col + CONV_SLAB] = conv_slab(n, col, col, False).astype(BF16)
        n += 1
    og_ref[:, GDN_CONV_CH:PG_W] = _dot(h, w_ref[:, GDN_CONV_CH:PG_W]).astype(BF16)
    os_ref[:, 0:SSD_INNER] = _dot(h, w_ref[:, PG_W:PG_W + SSD_INNER]).astype(BF16)
    for j in range(SSD_CONV_CH // CONV_SLAB):
        col = j * CONV_SLAB
        os_ref[:, SSD_INNER + col:SSD_INNER + col + CONV_SLAB] = conv_slab(
            n, PG_W + SSD_INNER + col, GDN_CONV_CH + col, True).astype(BF16)
        n += 1
    om_ref[...] = _dot(h, w_ref[:, PG_W + PS_W:])


def _inproj(x2, mod_l, g, w_perm, conv_w, conv_b, seq):
    n_tok = x2.shape[0]
    tm = min(TM_PROJ, seq)
    wtot = w_perm.shape[1]
    return pl.pallas_call(
        functools.partial(_inproj_kernel, tm=tm, seq=seq),
        grid=(n_tok // tm,),
        in_specs=[
            pl.BlockSpec((tm, D_MODEL), lambda i: (i, 0)),
            pl.BlockSpec((None, SUBLANES, D_MODEL), lambda i: ((i * tm) // seq, 0, 0)),
            pl.BlockSpec((1, D_MODEL), lambda i: (0, 0)),
            pl.BlockSpec((D_MODEL, wtot), lambda i: (0, 0)),
            pl.BlockSpec((CONV_K, N_CONV), lambda i: (0, 0)),
            pl.BlockSpec((1, N_CONV), lambda i: (0, 0)),
        ],
        out_specs=[
            pl.BlockSpec((tm, PG_W), lambda i: (i, 0)),
            pl.BlockSpec((tm, PS_W), lambda i: (i, 0)),
            pl.BlockSpec((tm, LANES), lambda i: (i, 0)),
        ],
        out_shape=[
            jax.ShapeDtypeStruct((n_tok, PG_W), BF16),
            jax.ShapeDtypeStruct((n_tok, PS_W), BF16),
            jax.ShapeDtypeStruct((n_tok, LANES), F32),
        ],
        scratch_shapes=[
            pltpu.VMEM((2, SUBLANES, N_CONV), F32),
            pltpu.VMEM((2, 2, SUBLANES + tm, CONV_SLAB), F32),
        ],
        compiler_params=pltpu.CompilerParams(
            dimension_semantics=("arbitrary",), vmem_limit_bytes=VMEM_LIMIT),
        name="norm_inproj",
    )(x2, mod_l, g, w_perm, conv_w, conv_b)


def _tri_masks():
    row = lax.broadcasted_iota(jnp.int32, (CHUNK, CHUNK), 0)
    col = lax.broadcasted_iota(jnp.int32, (CHUNK, CHUNK), 1)
    return row, col


def _block_diag2(m, lo_half):
    zero = jnp.zeros_like(m)
    return jnp.concatenate([jnp.where(lo_half, m, zero), jnp.where(lo_half, zero, m)], axis=0)


def _unit_lower_inverses(nmats, row, col, lo_half):
    eye = (row == col).astype(F32)
    pair = (row >> 1) == (col >> 1)
    xs = [eye - jnp.where(pair, n, 0.0) for n in nmats]
    for lg in range(2, CHUNK.bit_length()):
        mask = ((row >> lg) == (col >> lg)) & ((row >> (lg - 1)) != (col >> (lg - 1)))
        xb = [x.astype(BF16) for x in xs]
        ys = [_dot(jnp.where(mask, n, 0.0).astype(BF16), _block_diag2(b, lo_half)) for n, b in zip(nmats, xb)]
        yield
        xs = [x - _dot(b, _block_diag2(y.astype(BF16), lo_half)) for x, b, y in zip(xs, xb, ys)]
        yield
    return xs


def _interleave(*stage_generators):
    live = list(stage_generators)
    while live:
        for gen in list(live):
            try:
                next(gen)
            except StopIteration:
                live.remove(gen)


def _gdn_program(pg_ref, pm_ref, nega_ref, dtb_ref, ng_ref, o_ref, s_ref, gq_scr, b_scr, o0_scr, el_scr):
    heads = range(GDN_HEADS)

    def init():
        s_ref[...] = jnp.zeros_like(s_ref)

    row, col = _tri_masks()
    tril = (row >= col).astype(F32)
    row2 = lax.broadcasted_iota(jnp.int32, (CHUNK, LANES), 0)
    lane2 = lax.broadcasted_iota(jnp.int32, (CHUNK, LANES), 1)
    col2 = lane2 & (CHUNK - 1)
    lo_half = lane2 < CHUNK
    lo_half1 = lax.broadcasted_iota(jnp.int32, (1, LANES), 1) < CHUNK
    incl2 = row2 >= col2
    strict2 = row2 > col2
    zero_k = jnp.zeros((CHUNK, GDN_DK), BF16)
    zero_r = jnp.zeros((CHUNK, 2 * GDN_DV), BF16)
    nega = nega_ref[...]
    dtb = dtb_ref[...]
    ng = ng_ref[...]

    def wide(arr, l0):
        n = arr.shape[0]
        return jnp.concatenate([jnp.broadcast_to(arr[:, l0:l0 + 1], (n, LANES)),
                                jnp.broadcast_to(arr[:, l0 + 1:l0 + 2], (n, LANES))], axis=1)

    def halves(w, sel):
        return jnp.where(sel, w[:, 0:LANES], w[:, LANES:])

    def prepare(ci):
        units = []
        for j in range(MIX_CHUNKS):
            c = ci * MIX_CHUNKS + j
            rows = pl.ds(c * CHUNK, CHUNK)
            pmv = pm_ref[rows, :]
            beta_all = _sigmoid(pmv)
            g_all = nega * _softplus(pmv + dtb)
            gam = jnp.dot(tril, g_all, precision=HIGHEST, preferred_element_type=F32)
            gam_t = gam.T
            gam_last = gam[CHUNK - 1:CHUNK, :]
            el_scr[pl.ds(c * SUBLANES, SUBLANES), :] = jnp.broadcast_to(
                jnp.exp2(gam_last), (SUBLANES, LANES))
            for h0 in range(0, GDN_HEADS, 2):
                la = LANE_ALPHA + h0
                gam_w = wide(gam, la)
                r0_t = gam_t[la:la + 1, :]
                r1_t = gam_t[la + 1:la + 2, :]
                units.append(dict(
                    slot=c * GDN_HEADS + h0, rows=rows, h0=h0,
                    beta_w=wide(beta_all, LANE_BETA + h0), gam_w=gam_w, gl_w=wide(gam_last, la),
                    grow=jnp.where(lo_half1, jnp.concatenate([r0_t, r0_t], axis=1),
                                   jnp.concatenate([r1_t, r1_t], axis=1))))

        def cols(u, base):
            return pg_ref[u["rows"], base + u["h0"] * GDN_DK:base + (u["h0"] + 2) * GDN_DK]

        yield
        q2s = [cols(u, 0) for u in units]
        k2s = [cols(u, GDN_QK) for u in units]
        kbd = [jnp.concatenate([jnp.concatenate([k2[:, 0:GDN_DK], zero_k], axis=1),
                                jnp.concatenate([zero_k, k2[:, GDN_DK:]], axis=1)], axis=0) for k2 in k2s]
        qkks = [_dot_nt(jnp.concatenate([q2, k2], axis=0), r) for q2, k2, r in zip(q2s, k2s, kbd)]
        yield
        decays = [jnp.exp2(jnp.where(incl2, halves(u["gam_w"], lo_half) - u["grow"], -jnp.inf)) for u in units]
        nmats = [jnp.where(strict2, halves(u["beta_w"], lo_half) * qkk[CHUNK:2 * CHUNK, :] * d, 0.0)
                 for u, qkk, d in zip(units, qkks, decays)]
        amats = [(qkk[0:CHUNK, :] * d).astype(BF16) for qkk, d in zip(qkks, decays)]
        yield
        egws = [jnp.exp2(u["gam_w"]) for u in units]
        k2f = [k2.astype(F32) for k2 in k2s]
        kds = [(kf * jnp.exp2(u["gl_w"] - u["gam_w"])).astype(BF16) for kf, u in zip(k2f, units)]
        yield
        tinvs = yield from _unit_lower_inverses(nmats, row2, col2, lo_half)
        vbs = [(cols(u, 2 * GDN_QK).astype(F32) * u["beta_w"]).astype(BF16) for u in units]
        kbs = [(kf * (u["beta_w"] * e)).astype(BF16) for kf, u, e in zip(k2f, units, egws)]
        rhss = [jnp.concatenate(
            [jnp.concatenate([vb[:, 0:GDN_DV], kb[:, 0:GDN_DK], zero_r], axis=1),
             jnp.concatenate([zero_r, vb[:, GDN_DV:], kb[:, GDN_DK:]], axis=1)], axis=0)
            for vb, kb in zip(vbs, kbs)]
        yield
        sols = [_dot(x.astype(BF16), r).astype(BF16) for x, r in zip(tinvs, rhss)]
        yield
        sbd = [jnp.concatenate([jnp.concatenate([s[:, 0:2 * GDN_DV], zero_r], axis=1),
                                jnp.concatenate([zero_r, s[:, 2 * GDN_DV:]], axis=1)], axis=0) for s in sols]
        a_uw = [_dot(a, r) for a, r in zip(amats, sbd)]
        yield
        for hh in range(2):
            k_uw = [_dot_tn(kd[:, hh * GDN_DK:(hh + 1) * GDN_DK], s[:, hh * 2 * GDN_DV:(hh + 1) * 2 * GDN_DV])
                    for kd, s in zip(kds, sols)]
            yield
            for u, q2, e, au, ku in zip(units, q2s, egws, a_uw, k_uw):
                qe = (q2[:, hh * GDN_DK:(hh + 1) * GDN_DK].astype(F32) * e[:, hh * LANES:(hh + 1) * LANES])
                base = hh * 2 * GDN_DV
                o0_scr[u["slot"] + hh] = au[:, base:base + GDN_DV]
                b_scr[u["slot"] + hh] = ku[:, 0:GDN_DV]
                gq_scr[u["slot"] + hh] = jnp.concatenate(
                    [ku[:, GDN_DV:], qe - au[:, base + GDN_DV:base + 2 * GDN_DV]], axis=0).astype(BF16)
            yield

    def scan(ci):
        for c in range(ci * MIX_CHUNKS, (ci + 1) * MIX_CHUNKS):
            rows = pl.ds(c * CHUNK, CHUNK)
            e_last = el_scr[pl.ds(c * SUBLANES, SUBLANES), :][0:1, :]
            states = [s_ref[h] for h in heads]
            rs = [_dot(gq_scr[c * GDN_HEADS + h], states[h].astype(BF16)) for h in heads]
            yield
            for h in heads:
                s_ref[h] = (e_last[:, LANE_ALPHA + h:LANE_ALPHA + h + 1] * states[h]
                            - rs[h][0:GDN_DK, :] + b_scr[c * GDN_HEADS + h])
            yield
            for h in heads:
                o = rs[h][GDN_DK:, :] + o0_scr[c * GDN_HEADS + h]
                z = pg_ref[rows, GDN_CONV_CH + h * GDN_DV:GDN_CONV_CH + (h + 1) * GDN_DV].astype(F32)
                on = o * lax.rsqrt(jnp.mean(o * o, axis=-1, keepdims=True) + EPS) * ng
                o_ref[rows, h * GDN_DV:(h + 1) * GDN_DV] = (on * _silu(z)).astype(BF16)
                if h % 4 == 3:
                    yield

    return init, prepare, scan


def _ssd_program(ps_ref, pm_ref, nega_ref, dtb_ref, dsk_ref, ng_ref, o_ref, h_ref):
    def init():
        h_ref[...] = jnp.zeros_like(h_ref)

    row, col = _tri_masks()
    incl = row >= col
    tril = incl.astype(F32)
    nega = nega_ref[...]
    dtb = dtb_ref[...]
    dsk = dsk_ref[...]
    lane = lax.broadcasted_iota(jnp.int32, (CHUNK, LANES), 1)
    lo_half = lane < SSD_P
    incl2 = lax.broadcasted_iota(jnp.int32, (CHUNK, LANES), 0) >= (lane & (SSD_P - 1))
    lane1 = lax.broadcasted_iota(jnp.int32, (1, LANES), 1)
    lo_half1 = lane1 < SSD_P
    heads_per_group = SSD_HEADS // SSD_G
    gw = SSD_INNER // SSD_G

    def pair_cols(arr, l0):
        sel = lo_half if arr.shape[0] == CHUNK else lo_half1
        return jnp.where(sel, arr[:, l0:l0 + 1], arr[:, l0 + 1:l0 + 2])

    pairs_per_group = heads_per_group // 2

    def chunk_group(ci):
        groups = []
        for j in range(MIX_CHUNKS):
            c = ci * MIX_CHUNKS + j
            rows = pl.ds(c * CHUNK, CHUNK)
            pmv = pm_ref[rows, :]
            dt_all = _softplus(pmv + dtb)
            acum = jnp.dot(tril, nega * dt_all, precision=HIGHEST, preferred_element_type=F32)
            info = dict(rows=rows, dt=dt_all, acum=acum, acum_t=acum.T, a_last=acum[CHUNK - 1:CHUNK, :])
            for g in range(SSD_G):
                groups.append(dict(info, g=g))
        yield
        for gr in groups:
            g = gr["g"]
            gr["bg"] = ps_ref[gr["rows"], PS_B + g * SSD_N:PS_B + (g + 1) * SSD_N]
            gr["cg"] = ps_ref[gr["rows"], PS_C + g * SSD_N:PS_C + (g + 1) * SSD_N]
        for gr in groups:
            gr["cb2"] = _dot_nt(gr["cg"], jnp.concatenate([gr["bg"], gr["bg"]], axis=0))
        yield

        units = [dict(gr=gr, p=p, l0=LANE_DT + gr["g"] * heads_per_group + 2 * p)
                 for gr in groups for p in range(pairs_per_group)]
        batches = [units[i:i + len(units) // SSD_STAGE_SPLIT] for i in range(0, len(units), len(units) // SSD_STAGE_SPLIT)]
        for batch in batches:
            for u in batch:
                gr, l0 = u["gr"], u["l0"]
                head0 = l0 - LANE_DT
                u["x"] = ps_ref[gr["rows"], PS_X + head0 * SSD_P:PS_X + (head0 + 2) * SSD_P].astype(F32)
                u["ac"] = pair_cols(gr["acum"], l0)
                u["al"] = pair_cols(gr["a_last"], l0)
                r0_t = gr["acum_t"][l0:l0 + 1, :]
                r1_t = gr["acum_t"][l0 + 1:l0 + 2, :]
                u["arow"] = jnp.where(lo_half1, jnp.concatenate([r0_t, r0_t], axis=1),
                                      jnp.concatenate([r1_t, r1_t], axis=1))
                u["xdt"] = u["x"] * pair_cols(gr["dt"], l0)
            yield
        for batch in batches:
            for u in batch:
                decay = jnp.exp2(jnp.where(incl2, u["ac"] - u["arow"], -jnp.inf))
                u["m"] = (u["gr"]["cb2"] * decay).astype(BF16)
                u["rhs"] = _block_diag2(u["xdt"], lo_half).astype(BF16)
            yield
        for batch in batches:
            for u in batch:
                u["y_diag"] = _dot(u["m"], u["rhs"])
                u["xw"] = (u["xdt"] * jnp.exp2(u["al"] - u["ac"])).astype(BF16)
            yield
        for gr in groups:
            mine = [u for u in units if u["gr"] is gr]
            gr["upd"] = _dot_tn(gr["bg"], jnp.concatenate([u["xw"] for u in mine], axis=1))
            gr["scale"] = jnp.concatenate([jnp.exp2(u["al"]) for u in mine], axis=1)
        yield

        states = [h_ref[g] for g in range(SSD_G)]
        for gr in groups:
            gr["y_off"] = _dot(gr["cg"], states[gr["g"]].astype(BF16))
            states[gr["g"]] = gr["scale"] * states[gr["g"]] + gr["upd"]
        for g in range(SSD_G):
            h_ref[g] = states[g]
        yield

        for gr in groups:
            g = gr["g"]
            mine = [u for u in units if u["gr"] is gr]
            y = jnp.concatenate(
                [u["y_diag"] + jnp.exp2(u["ac"]) * gr["y_off"][:, u["p"] * LANES:(u["p"] + 1) * LANES]
                 + pair_cols(dsk, u["l0"]) * u["x"] for u in mine], axis=1)
            z = ps_ref[gr["rows"], g * gw:(g + 1) * gw].astype(F32)
            yz = y * _silu(z)
            yn = yz * lax.rsqrt(jnp.mean(yz * yz, axis=-1, keepdims=True) + EPS)
            o_ref[gr["rows"], g * gw:(g + 1) * gw] = (yn * ng_ref[:, g * gw:(g + 1) * gw]).astype(BF16)
            yield

    return init, chunk_group


def _mixer_kernel(pg_ref, ps_ref, pm_ref, g_nega, g_dtb, g_ng, s_nega, s_dtb, s_dsk, s_ng, go_ref, so_ref,
                  s_ref, gq_scr, b_scr, o0_scr, el_scr, h_ref, *, tt):
    g_init, g_prepare, g_scan = _gdn_program(pg_ref, pm_ref, g_nega, g_dtb, g_ng, go_ref, s_ref,
                                             gq_scr, b_scr, o0_scr, el_scr)
    s_init, s_chunks = _ssd_program(ps_ref, pm_ref, s_nega, s_dtb, s_dsk, s_ng, so_ref, h_ref)

    @pl.when(pl.program_id(1) == 0)
    def _():
        g_init()
        s_init()

    n_groups = tt // CHUNK // MIX_CHUNKS
    for ci in range(n_groups):
        stages = [g_prepare(ci), s_chunks(ci)]
        if ci > 0:
            stages.append(g_scan(ci - 1))
        _interleave(*stages)
    _interleave(g_scan(n_groups - 1))


def _mixer(pg, ps, pm, gdn_a_log, gdn_dt_bias, gdn_norm_g, ssd_a_log, ssd_dt_bias, ssd_d, ssd_norm_g, bsz, seq):
    tt = min(TT_SCAN, seq)
    nt = seq // tt
    units = tt // CHUNK * GDN_HEADS

    def lanes(v, lane0):
        return jnp.zeros((1, LANES), F32).at[0, lane0:lane0 + v.shape[0]].set(v)

    def tile(width):
        return pl.BlockSpec((tt, width), lambda b, t: (b * nt + t, 0))

    def const(width):
        return pl.BlockSpec((1, width), lambda b, t: (0, 0))

    return pl.pallas_call(
        functools.partial(_mixer_kernel, tt=tt),
        grid=(bsz, nt),
        in_specs=[tile(PG_W), tile(PS_W), tile(LANES), const(LANES), const(LANES), const(GDN_DV),
                  const(LANES), const(LANES), const(LANES), const(SSD_INNER)],
        out_specs=[tile(GDN_V), tile(SSD_INNER)],
        out_shape=[jax.ShapeDtypeStruct((bsz * seq, GDN_V), BF16),
                   jax.ShapeDtypeStruct((bsz * seq, SSD_INNER), BF16)],
        scratch_shapes=[
            pltpu.VMEM((GDN_HEADS, GDN_DK, GDN_DV), F32),
            pltpu.VMEM((units, GDN_DK + CHUNK, GDN_DK), BF16),
            pltpu.VMEM((units, GDN_DK, GDN_DV), F32),
            pltpu.VMEM((units, CHUNK, GDN_DV), F32),
            pltpu.VMEM((tt // CHUNK * SUBLANES, LANES), F32),
            pltpu.VMEM((SSD_G, SSD_N, SSD_INNER // SSD_G), F32),
        ],
        compiler_params=pltpu.CompilerParams(
            dimension_semantics=("parallel", "arbitrary"), vmem_limit_bytes=VMEM_LIMIT),
        name="mixer_scan",
    )(pg, ps, pm, lanes(-LOG2_E * jnp.exp(gdn_a_log), LANE_ALPHA), lanes(gdn_dt_bias, LANE_ALPHA),
      gdn_norm_g.reshape(1, GDN_DV), lanes(-LOG2_E * jnp.exp(ssd_a_log), LANE_DT), lanes(ssd_dt_bias, LANE_DT),
      lanes(ssd_d, LANE_DT), ssd_norm_g.reshape(1, SSD_INNER))


RT_IDX = 0
RT_GATE = TOP_K
RT_POS = 2 * TOP_K


def _outproj_router_kernel(go_ref, so_ref, x_ref, mod_ref, g_ref, wo_ref, rw_ref, rb_ref,
                           xn_ref, h_ref, rt_ref, cnt_ref, run_ref, *, tm):
    i = pl.program_id(0)

    @pl.when(i == 0)
    def _():
        run_ref[...] = jnp.zeros_like(run_ref)

    sub = tm // ROUTER_SPLIT
    lane = lax.broadcasted_iota(jnp.int32, (sub, LANES), 1).astype(F32)
    trow = lax.broadcasted_iota(jnp.int32, (sub, sub), 0)
    tcol = lax.broadcasted_iota(jnp.int32, (sub, sub), 1)
    earlier = (trow > tcol).astype(BF16)
    onehots = [None] * ROUTER_SPLIT

    def part(j):
        rows = slice(j * sub, (j + 1) * sub)
        mix = _dot(go_ref[rows, :], wo_ref[0:GDN_V, :]) + _dot(so_ref[rows, :], wo_ref[GDN_V:, :])
        yield
        xn = x_ref[rows, :] + mod_ref[2:3, :] * mix
        xn_ref[rows, :] = xn
        h = _modulated_norm(xn, g_ref[...], mod_ref[3:4, :], mod_ref[4:5, :])
        h_ref[rows, :] = _pack_rows(h)
        yield
        h_hi = h.astype(BF16)
        h_lo = (h - h_hi.astype(F32)).astype(BF16)
        hw = _dot(h_hi, rw_ref[...])
        work = hw[:, 0:LANES] + hw[:, LANES:] + _dot(h_lo, rw_ref[:, 0:LANES]) + rb_ref[...]
        yield
        tops = []
        idxs = []
        for _ in range(TOP_K):
            m = jnp.max(work, axis=-1, keepdims=True)
            idx = jnp.min(jnp.where(work == m, lane, float(LANES)), axis=-1, keepdims=True)
            work = jnp.where(lane == idx, -jnp.inf, work)
            tops.append(m)
            idxs.append(idx)
            yield
        exps = [jnp.exp(m - tops[0]) for m in tops]
        denom = exps[0] + exps[1] + exps[2] + exps[3]
        onehot = jnp.zeros((sub, LANES), F32)
        for idx in idxs:
            onehot = onehot + (lane == idx).astype(F32)
        onehots[j] = onehot
        yield
        prior = run_ref[0:1, :]
        for jj in range(j):
            prior = prior + jnp.sum(onehots[jj], axis=0, keepdims=True)
        before = _dot(earlier, onehot.astype(BF16)) + prior
        yield
        rt = jnp.zeros((sub, LANES), F32)
        for k in range(TOP_K):
            pos = jnp.sum(jnp.where(lane == idxs[k], before, 0.0), axis=-1, keepdims=True)
            rt = jnp.where(lane == RT_IDX + k, idxs[k], rt)
            rt = jnp.where(lane == RT_GATE + k, exps[k] / denom, rt)
            rt = jnp.where(lane == RT_POS + k, pos, rt)
        rt_ref[rows, :] = rt

    _interleave(*[part(j) for j in range(ROUTER_SPLIT)])
    run = run_ref[0:1, :]
    for onehot in onehots:
        run = run + jnp.sum(onehot, axis=0, keepdims=True)
    run_ref[...] = jnp.broadcast_to(run, run_ref.shape)
    cnt_ref[...] = jnp.broadcast_to(run, cnt_ref.shape)


def _outproj_router(go, so, x2, mod_l, g, w_out, rw, rb, seq):
    n_tok = x2.shape[0]
    tm = min(TM_ROUTER, seq)
    return pl.pallas_call(
        functools.partial(_outproj_router_kernel, tm=tm),
        grid=(n_tok // tm,),
        in_specs=[
            pl.BlockSpec((tm, GDN_V), lambda i: (i, 0)),
            pl.BlockSpec((tm, SSD_INNER), lambda i: (i, 0)),
            pl.BlockSpec((tm, D_MODEL), lambda i: (i, 0)),
            pl.BlockSpec((None, SUBLANES, D_MODEL), lambda i: ((i * tm) // seq, 0, 0)),
            pl.BlockSpec((1, D_MODEL), lambda i: (0, 0)),
            pl.BlockSpec((GDN_V + SSD_INNER, D_MODEL), lambda i: (0, 0)),
            pl.BlockSpec((D_MODEL, 2 * LANES), lambda i: (0, 0)),
            pl.BlockSpec((1, LANES), lambda i: (0, 0)),
        ],
        out_specs=[
            pl.BlockSpec((tm, D_MODEL), lambda i: (i, 0)),
            pl.BlockSpec((tm, ROW_WORDS), lambda i: (i, 0)),
            pl.BlockSpec((tm, LANES), lambda i: (i, 0)),
            pl.BlockSpec((SUBLANES, LANES), lambda i: (0, 0)),
        ],
        out_shape=[
            jax.ShapeDtypeStruct((n_tok, D_MODEL), F32),
            jax.ShapeDtypeStruct((n_tok, ROW_WORDS), U32),
            jax.ShapeDtypeStruct((n_tok, LANES), F32),
            jax.ShapeDtypeStruct((SUBLANES, LANES), F32),
        ],
        scratch_shapes=[pltpu.VMEM((SUBLANES, LANES), F32)],
        compiler_params=pltpu.CompilerParams(
            dimension_semantics=("arbitrary",), vmem_limit_bytes=VMEM_LIMIT),
        name="outproj_router",
    )(go, so, x2, mod_l, g, w_out, rw, rb)


def _sc_workers():
    info = plsc.get_sparse_core_info()
    return info.num_cores, info.num_subcores


def _scatter_rows(src, dest_t, n_rows):
    n_tok, width = src.shape
    n_k = dest_t.shape[0]
    nc, ns = _sc_workers()
    per_w = n_tok // (nc * ns)
    win = min(SC_WINDOW, per_w)
    mesh = plsc.VectorSubcoreMesh(core_axis_name="c", subcore_axis_name="s")

    @functools.partial(
        pl.kernel, mesh=mesh,
        out_type=jax.ShapeDtypeStruct((n_rows, width), src.dtype),
        scratch_types=[pltpu.VMEM((win,), jnp.int32), pltpu.VMEM((win, width), src.dtype)],
    )
    def scatter_kernel(src_hbm, idx_hbm, out_hbm, idx_v, rows_v):
        wid = lax.axis_index("s") * nc + lax.axis_index("c")
        base = wid * per_w

        @pl.loop(0, per_w // win)
        def _(j):
            off = base + j * win
            pltpu.sync_copy(src_hbm.at[pl.ds(off, win)], rows_v)
            for k in range(n_k):
                pltpu.sync_copy(idx_hbm.at[k, pl.ds(off, win)], idx_v)
                pltpu.sync_copy(rows_v, out_hbm.at[idx_v])

    return scatter_kernel(src, dest_t)


def _gather_rows(table, idx):
    n_idx = idx.shape[0]
    width = table.shape[1]
    nc, ns = _sc_workers()
    per_w = n_idx // (nc * ns)
    win = min(SC_WINDOW, per_w)
    mesh = plsc.VectorSubcoreMesh(core_axis_name="c", subcore_axis_name="s")

    @functools.partial(
        pl.kernel, mesh=mesh,
        out_type=jax.ShapeDtypeStruct((n_idx, width), table.dtype),
        scratch_types=[pltpu.VMEM((per_w,), jnp.int32), pltpu.VMEM((win, width), table.dtype)],
    )
    def gather_kernel(table_hbm, idx_hbm, out_hbm, idx_v, rows_v):
        wid = lax.axis_index("s") * nc + lax.axis_index("c")
        base = wid * per_w
        pltpu.sync_copy(idx_hbm.at[pl.ds(base, per_w)], idx_v)

        @pl.loop(0, per_w // win)
        def _(j):
            pltpu.sync_copy(table_hbm.at[idx_v.at[pl.ds(j * win, win)]], rows_v)
            pltpu.sync_copy(rows_v, out_hbm.at[pl.ds(base + j * win, win)])

    return gather_kernel(table, idx)


def _expert_kernel(be_ref, bv_ref, x_ref, wgu_ref, bgu_ref, wd_ref, bd_ref, y_ref, wgu_b, wd_b, *, tr):
    i = pl.program_id(0)
    e = be_ref[i]
    prev = be_ref[jnp.maximum(i - 1, 0)]

    @pl.when((i == 0) | (e != prev))
    def _():
        wgu_b[...] = wgu_ref[...].astype(BF16)
        wd_b[...] = wd_ref[...].astype(BF16)

    valid = bv_ref[i]

    @pl.when(valid == 0)
    def _():
        y_ref[...] = jnp.zeros_like(y_ref)

    @pl.when(valid > 0)
    def _():
        rows = lax.broadcasted_iota(jnp.int32, (tr, D_MODEL), 0)
        x = jnp.where(rows < valid, _unpack_rows(x_ref[...]), 0.0).astype(BF16)
        acc = jnp.zeros((tr, D_MODEL), F32)
        for f in range(0, D_FF, FF_CHUNK):
            gate = _dot(x, wgu_b[:, f:f + FF_CHUNK]) + bgu_ref[:, f:f + FF_CHUNK]
            up = _dot(x, wgu_b[:, D_FF + f:D_FF + f + FF_CHUNK]) + bgu_ref[:, D_FF + f:D_FF + f + FF_CHUNK]
            gate = jnp.minimum(gate, SWIGLU_LIMIT)
            up = jnp.clip(up, -SWIGLU_LIMIT, SWIGLU_LIMIT)
            act = (up + 1.0) * (gate * _sigmoid(gate * SWIGLU_ALPHA))
            acc = acc + _dot(act.astype(BF16), wd_b[f:f + FF_CHUNK, :])
        y_ref[...] = _pack_rows(acc + bd_ref[...])


def _experts(x_rows, blk_expert, blk_valid, w_gu, b_gu, w_down, b_down, layer):
    n_rows, row_w = x_rows.shape
    depth = w_gu.shape[0]
    tr = ROW_BLOCK
    nb = n_rows // tr
    grid_spec = pltpu.PrefetchScalarGridSpec(
        num_scalar_prefetch=2,
        grid=(nb,),
        in_specs=[
            pl.BlockSpec((tr, row_w), lambda i, be, bv: (i, 0)),
            pl.BlockSpec((None, None, D_MODEL, 2 * D_FF), lambda i, be, bv: (layer, be[i], 0, 0)),
            pl.BlockSpec((None, None, 1, 2 * D_FF), lambda i, be, bv: (layer, be[i], 0, 0)),
            pl.BlockSpec((None, None, D_FF, D_MODEL), lambda i, be, bv: (layer, be[i], 0, 0)),
            pl.BlockSpec((None, None, 1, D_MODEL), lambda i, be, bv: (layer, be[i], 0, 0)),
        ],
        out_specs=pl.BlockSpec((tr, row_w), lambda i, be, bv: (i, 0)),
        scratch_shapes=[pltpu.VMEM((D_MODEL, 2 * D_FF), BF16), pltpu.VMEM((D_FF, D_MODEL), BF16)],
    )
    return pl.pallas_call(
        functools.partial(_expert_kernel, tr=tr),
        grid_spec=grid_spec,
        out_shape=jax.ShapeDtypeStruct((n_rows, row_w), x_rows.dtype),
        compiler_params=pltpu.CompilerParams(
            dimension_semantics=("arbitrary",), vmem_limit_bytes=VMEM_LIMIT),
        name="moe_experts",
    )(blk_expert, blk_valid, x_rows, w_gu, b_gu.reshape(depth, N_EXPERTS, 1, 2 * D_FF), w_down,
      b_down.reshape(depth, N_EXPERTS, 1, D_MODEL))


def _combine_kernel(xn_ref, y4_ref, rt_ref, mod_ref, g_ref, o_ref, *, final):
    rt = rt_ref[...]
    acc = rt[:, RT_GATE:RT_GATE + 1] * _unpack_rows(y4_ref[0])
    for k in range(1, TOP_K):
        acc = acc + rt[:, RT_GATE + k:RT_GATE + k + 1] * _unpack_rows(y4_ref[k])
    x = xn_ref[...] + mod_ref[5:6, :] * acc
    if final:
        ms = jnp.mean(x * x, axis=-1, keepdims=True)
        x = x * lax.rsqrt(ms + EPS) * g_ref[...]
    o_ref[...] = x


def _combine(xn, y4, rt, mod_l, final_g, seq, final):
    n_tok = xn.shape[0]
    row_w = y4.shape[2]
    tm = min(TM_PROJ, seq)
    return pl.pallas_call(
        functools.partial(_combine_kernel, final=final),
        grid=(n_tok // tm,),
        in_specs=[
            pl.BlockSpec((tm, D_MODEL), lambda i: (i, 0)),
            pl.BlockSpec((TOP_K, tm, row_w), lambda i: (0, i, 0)),
            pl.BlockSpec((tm, LANES), lambda i: (i, 0)),
            pl.BlockSpec((None, SUBLANES, D_MODEL), lambda i: ((i * tm) // seq, 0, 0)),
            pl.BlockSpec((1, D_MODEL), lambda i: (0, 0)),
        ],
        out_specs=pl.BlockSpec((tm, D_MODEL), lambda i: (i, 0)),
        out_shape=jax.ShapeDtypeStruct((n_tok, D_MODEL), F32),
        compiler_params=pltpu.CompilerParams(
            dimension_semantics=("parallel",), vmem_limit_bytes=VMEM_LIMIT),
        name="moe_combine",
    )(xn, y4, rt, mod_l, final_g)


def _permute_w_in(w_in):
    off = 0
    gdn_qkvz = w_in[:, off:off + PG_W]; off += PG_W
    small_ba = w_in[:, off:off + 2 * GDN_HEADS]; off += 2 * GDN_HEADS
    ssd_zxbc = w_in[:, off:off + PS_W]; off += PS_W
    ssd_dt = w_in[:, off:off + SSD_HEADS]
    pad = jnp.zeros((w_in.shape[0], LANES - 2 * GDN_HEADS - SSD_HEADS), w_in.dtype)
    return jnp.concatenate([gdn_qkvz, ssd_zxbc, small_ba, ssd_dt, pad], axis=1).astype(BF16)


def _moe_schedule(rt, counts_row, n_rows):
    idx = rt[:, RT_IDX:RT_IDX + TOP_K].astype(jnp.int32)
    pos = rt[:, RT_POS:RT_POS + TOP_K].astype(jnp.int32)
    counts = counts_row[0, :N_EXPERTS].astype(jnp.int32)
    padded = (counts + ROW_BLOCK - 1) // ROW_BLOCK * ROW_BLOCK
    pad_end = jnp.cumsum(padded)
    pad_start = pad_end - padded
    experts = jnp.arange(N_EXPERTS, dtype=jnp.int32)
    dest = pos + jnp.sum(jnp.where(idx[..., None] == experts, pad_start, 0), axis=-1)
    blk_start = jnp.arange(n_rows // ROW_BLOCK, dtype=jnp.int32) * ROW_BLOCK
    blk_expert = jnp.minimum(jnp.sum(blk_start[:, None] >= pad_end[None, :], axis=1), N_EXPERTS - 1)
    blk_onehot = blk_expert[:, None] == experts
    blk_valid = (jnp.sum(jnp.where(blk_onehot, counts + pad_start, 0), axis=1) - blk_start)
    blk_valid = jnp.where(blk_start < pad_end[-1], jnp.clip(blk_valid, 0, ROW_BLOCK), 0)
    return dest, blk_expert.astype(jnp.int32), blk_valid.astype(jnp.int32)


def _forward(x, c, ada_w, ada_b, norm1_g, norm2_g, w_in, gdn_conv_w, gdn_a_log, gdn_dt_bias, gdn_norm_g,
             ssd_conv_w, ssd_conv_b, ssd_a_log, ssd_dt_bias, ssd_d, ssd_norm_g, w_out, router_w, router_b,
             moe_w_gu, moe_b_gu, moe_w_down, moe_b_down, final_g):
    bsz, seq, _ = x.shape
    depth = ada_w.shape[0]
    n_tok = bsz * seq
    n_rows = n_tok * TOP_K + N_EXPERTS * ROW_BLOCK
    mod = _ada_mod(c, ada_w, ada_b)
    x2 = x.reshape(n_tok, D_MODEL)
    fg = final_g.reshape(1, D_MODEL)
    mix_w = (w_in, gdn_conv_w, ssd_conv_w, ssd_conv_b, router_w, router_b, w_out)

    def prepare_weights(l, w_in_, gdn_conv_w_, ssd_conv_w_, ssd_conv_b_, router_w_, router_b_, w_out_):
        conv_w = jnp.concatenate([gdn_conv_w_[l], ssd_conv_w_[l]], axis=1)
        conv_b = jnp.concatenate([jnp.zeros((1, GDN_CONV_CH), F32), ssd_conv_b_[l].reshape(1, -1)], axis=1)
        rw = jnp.zeros((D_MODEL, LANES), F32).at[:, :N_EXPERTS].set(router_w_[l])
        rw_hi = rw.astype(BF16)
        rw = jnp.concatenate([rw_hi, (rw - rw_hi.astype(F32)).astype(BF16)], axis=1)
        rb = jnp.full((1, LANES), -1e30, F32).at[0, :N_EXPERTS].set(router_b_[l])
        return _permute_w_in(w_in_[l]), conv_w, conv_b, rw, rb, w_out_[l].astype(BF16)

    prepared = prepare_weights(0, *mix_w)
    for l in range(depth):
        w_perm, conv_w, conv_b, rw, rb, w_out_b = prepared
        pg, ps, pm = _inproj(x2, mod[l], norm1_g[l].reshape(1, D_MODEL), w_perm, conv_w, conv_b, seq)
        go, so = _mixer(pg, ps, pm, gdn_a_log[l], gdn_dt_bias[l], gdn_norm_g[l],
                        ssd_a_log[l], ssd_dt_bias[l], ssd_d[l], ssd_norm_g[l], bsz, seq)
        xn, h2, rt, cnt = _outproj_router(go, so, x2, mod[l], norm2_g[l].reshape(1, D_MODEL),
                                          w_out_b, rw, rb, seq)
        dest, blk_expert, blk_valid = _moe_schedule(rt, cnt, n_rows)
        dest_t = dest.T
        if l + 1 < depth:
            dest_t, tied = lax.optimization_barrier((dest_t, mix_w))
            prepared = prepare_weights(l + 1, *tied)
        x_rows = _scatter_rows(h2, dest_t, n_rows)
        y_rows = _experts(x_rows, blk_expert, blk_valid, moe_w_gu, moe_b_gu, moe_w_down, moe_b_down, l)
        y4 = _gather_rows(y_rows, dest_t.reshape(TOP_K * n_tok)).reshape(TOP_K, n_tok, y_rows.shape[1])
        x2 = _combine(xn, y4, rt, mod[l], fg, seq, final=(l == depth - 1))
    return x2.reshape(bsz, seq, D_MODEL)


def kernel(x, c, ada_w, ada_b, norm1_g, norm2_g, w_in, gdn_conv_w, gdn_a_log, gdn_dt_bias, gdn_norm_g, ssd_conv_w, ssd_conv_b, ssd_a_log, ssd_dt_bias, ssd_d, ssd_norm_g, w_out, router_w, router_b, moe_w_gu, moe_b_gu, moe_w_down, moe_b_down, final_g):
    return _forward(x, c, ada_w, ada_b, norm1_g, norm2_g, w_in, gdn_conv_w, gdn_a_log, gdn_dt_bias, gdn_norm_g,
                    ssd_conv_w, ssd_conv_b, ssd_a_log, ssd_dt_bias, ssd_d, ssd_norm_g, w_out, router_w, router_b,
                    moe_w_gu, moe_b_gu, moe_w_down, moe_b_down, final_g)
```

```python
import functools

import jax
import jax.numpy as jnp
from jax import lax
from jax.experimental import pallas as pl
from jax.experimental.pallas import tpu as pltpu
from jax.experimental.pallas import tpu_sc as plsc

F32 = jnp.float32
BF16 = jnp.bfloat16
U32 = jnp.uint32
HIGHEST = lax.Precision.HIGHEST

D_MODEL = 1024
CHUNK = 64
CONV_K = 4
GDN_HEADS = 8
GDN_DK = 128
GDN_DV = 128
GDN_QK = GDN_HEADS * GDN_DK
GDN_V = GDN_HEADS * GDN_DV
GDN_CONV_CH = 2 * GDN_QK + GDN_V
SSD_P = 64
SSD_HEADS = 16
SSD_G = 2
SSD_N = 128
SSD_INNER = SSD_HEADS * SSD_P
SSD_BC = SSD_G * SSD_N
SSD_CONV_CH = SSD_INNER + 2 * SSD_BC
N_EXPERTS = 32
TOP_K = 4
D_FF = D_MODEL
SWIGLU_ALPHA = 1.702
SWIGLU_LIMIT = 7.0
EPS = 1e-6
LOG2_E = 1.4426950408889634

LANES = 128
SUBLANES = 8
PG_W = GDN_CONV_CH + GDN_V
PS_W = SSD_INNER + SSD_CONV_CH
PS_X = SSD_INNER
PS_B = 2 * SSD_INNER
PS_C = 2 * SSD_INNER + SSD_BC
N_CONV = GDN_CONV_CH + SSD_CONV_CH
LANE_BETA = 0
LANE_ALPHA = GDN_HEADS
LANE_DT = 2 * GDN_HEADS
VMEM_LIMIT = 56 * 1024 * 1024

TM_PROJ = 512
TM_ROUTER = 512
ROUTER_SPLIT = 2
TT_SCAN = 512
MIX_CHUNKS = 4
SSD_STAGE_SPLIT = 4
ROW_BLOCK = 512
FF_CHUNK = 512
CONV_SLAB = 512
SC_WINDOW = 128
ROW_WORDS = D_MODEL // 2


def _dot(a, b):
    return jnp.dot(a, b, preferred_element_type=F32)


def _dot_nt(a, b):
    return lax.dot_general(a, b, (((1,), (1,)), ((), ())), preferred_element_type=F32)


def _dot_tn(a, b):
    return lax.dot_general(a, b, (((0,), (0,)), ((), ())), preferred_element_type=F32)


def _sigmoid(x):
    return 1.0 / (1.0 + jnp.exp(-x))


def _silu(x):
    h = 0.5 * x
    return h + h * jnp.tanh(h)


def _softplus(x):
    return jnp.maximum(x, 0.0) + jnp.log(1.0 + jnp.exp(-jnp.abs(x)))


def _pack_rows(h):
    w = h.shape[1] // 2
    hi = pltpu.bitcast(h[:, :w].astype(BF16).astype(F32), U32)
    lo = pltpu.bitcast(h[:, w:].astype(BF16).astype(F32), U32)
    return hi | (lo >> 16)


def _unpack_rows(p):
    hi = pltpu.bitcast(p & jnp.uint32(0xFFFF0000), F32)
    lo = pltpu.bitcast(p << 16, F32)
    return jnp.concatenate([hi, lo], axis=1)


def _ada_kernel(c_ref, w_ref, b_ref, o_ref):
    c = c_ref[...]
    o_ref[0] = jnp.dot(_silu(c), w_ref[0], precision=HIGHEST, preferred_element_type=F32) + b_ref[0]


def _ada_mod(c, ada_w, ada_b):
    depth = ada_w.shape[0]
    bsz = c.shape[0]
    c8 = jnp.zeros((SUBLANES, D_MODEL), F32).at[:bsz].set(c)
    out = pl.pallas_call(
        _ada_kernel,
        grid=(depth, 6),
        in_specs=[
            pl.BlockSpec((SUBLANES, D_MODEL), lambda l, j: (0, 0)),
            pl.BlockSpec((1, D_MODEL, D_MODEL), lambda l, j: (l, 0, j)),
            pl.BlockSpec((1, 1, D_MODEL), lambda l, j: (l, 0, j)),
        ],
        out_specs=pl.BlockSpec((1, SUBLANES, D_MODEL), lambda l, j: (l, 0, j)),
        out_shape=jax.ShapeDtypeStruct((depth, SUBLANES, 6 * D_MODEL), F32),
        name="ada_mod",
    )(c8, ada_w, ada_b.reshape(depth, 1, 6 * D_MODEL))
    mod = out[:, :bsz].reshape(depth, bsz, 6, D_MODEL)
    return jnp.concatenate([mod, jnp.zeros((depth, bsz, 2, D_MODEL), F32)], axis=2)


def _modulated_norm(x, g, shift, scale):
    ms = jnp.mean(x * x, axis=-1, keepdims=True)
    return (x * lax.rsqrt(ms + EPS) * g) * (1.0 + scale) + shift


def _inproj_kernel(x_ref, mod_ref, g_ref, w_ref, cw_ref, cb_ref, og_ref, os_ref, om_ref,
                   tail_ref, cbuf_ref, *, tm, seq):
    i = pl.program_id(0)

    @pl.when((i * tm) % seq == 0)
    def _():
        tail_ref[...] = jnp.zeros_like(tail_ref)

    h = _modulated_norm(x_ref[...], g_ref[...], mod_ref[0:1, :], mod_ref[1:2, :]).astype(BF16)

    def conv_slab(n, wcol, ccol, bias):
        cs = slice(ccol, ccol + CONV_SLAB)
        pbuf = cbuf_ref.at[n % 2, 0]
        bbuf = cbuf_ref.at[n % 2, 1]
        p = _dot(h, w_ref[:, wcol:wcol + CONV_SLAB])
        pbuf[0:SUBLANES, :] = tail_ref[0, :, cs]
        pbuf[SUBLANES:SUBLANES + tm, :] = p
        tail_ref[0, :, cs] = p[tm - SUBLANES:tm, :]
        p1 = pbuf[SUBLANES - 1:SUBLANES - 1 + tm, :]
        b = p * cw_ref[1:2, cs] + p1 * cw_ref[0:1, cs]
        bbuf[0:SUBLANES, :] = tail_ref[1, :, cs]
        bbuf[SUBLANES:SUBLANES + tm, :] = b
        tail_ref[1, :, cs] = b[tm - SUBLANES:tm, :]
        y = p * cw_ref[3:4, cs] + p1 * cw_ref[2:3, cs] + bbuf[SUBLANES - 2:SUBLANES - 2 + tm, :]
        if bias:
            y = y + cb_ref[:, cs]
        return _silu(y)

    n = 0
    for part, scale in ((0, GDN_DK ** -0.5), (1, 1.0)):
        for j in range(GDN_QK // CONV_SLAB):
            col = part * GDN_QK + j * CONV_SLAB
            y = conv_slab(n, col, col, False)
            n += 1
            for hh in range(CONV_SLAB // GDN_DK):
                yh = y[:, hh * GDN_DK:(hh + 1) * GDN_DK]
                inv = lax.rsqrt(jnp.sum(yh * yh, axis=-1, keepdims=True) + EPS) * scale
                og_ref[:, col + hh * GDN_DK:col + (hh + 1) * GDN_DK] = (yh * inv).astype(BF16)
    for j in range(GDN_V // CONV_SLAB):
        col = 2 * GDN_QK + j * CONV_SLAB
        og_ref[:, col:col + CONV_SLAB] = conv_slab(n, col, col, False).astype(BF16)
        n += 1
    og_ref[:, GDN_CONV_CH:PG_W] = _dot(h, w_ref[:, GDN_CONV_CH:PG_W]).astype(BF16)
    os_ref[:, 0:SSD_INNER] = _dot(h, w_ref[:, PG_W:PG_W + SSD_INNER]).astype(BF16)
    for j in range(SSD_CONV_CH // CONV_SLAB):
        col = j * CONV_SLAB
        os_ref[:, SSD_INNER + col:SSD_INNER + col + CONV_SLAB] = conv_slab(
            n, PG_W + SSD_INNER + col, GDN_CONV_CH + col, True).astype(BF16)
        n += 1
    om_ref[...] = _dot(h, w_ref[:, PG_W + PS_W:])


def _inproj(x2, mod_l, g, w_perm, conv_w, conv_b, seq):
    n_tok = x2.shape[0]
    tm = min(TM_PROJ, seq)
    wtot = w_perm.shape[1]
    return pl.pallas_call(
        functools.partial(_inproj_kernel, tm=tm, seq=seq),
        grid=(n_tok // tm,),
        in_specs=[
            pl.BlockSpec((tm, D_MODEL), lambda i: (i, 0)),
            pl.BlockSpec((None, SUBLANES, D_MODEL), lambda i: ((i * tm) // seq, 0, 0)),
            pl.BlockSpec((1, D_MODEL), lambda i: (0, 0)),
            pl.BlockSpec((D_MODEL, wtot), lambda i: (0, 0)),
            pl.BlockSpec((CONV_K, N_CONV), lambda i: (0, 0)),
            pl.BlockSpec((1, N_CONV), lambda i: (0, 0)),
        ],
        out_specs=[
            pl.BlockSpec((tm, PG_W), lambda i: (i, 0)),
            pl.BlockSpec((tm, PS_W), lambda i: (i, 0)),
            pl.BlockSpec((tm, LANES), lambda i: (i, 0)),
        ],
        out_shape=[
            jax.ShapeDtypeStruct((n_tok, PG_W), BF16),
            jax.ShapeDtypeStruct((n_tok, PS_W), BF16),
            jax.ShapeDtypeStruct((n_tok, LANES), F32),
        ],
        scratch_shapes=[
            pltpu.VMEM((2, SUBLANES, N_CONV), F32),
            pltpu.VMEM((2, 2, SUBLANES + tm, CONV_SLAB), F32),
        ],
        compiler_params=pltpu.CompilerParams(
            dimension_semantics=("arbitrary",), vmem_limit_bytes=VMEM_LIMIT),
        name="norm_inproj",
    )(x2, mod_l, g, w_perm, conv_w, conv_b)


def _tri_masks():
    row = lax.broadcasted_iota(jnp.int32, (CHUNK, CHUNK), 0)
    col = lax.broadcasted_iota(jnp.int32, (CHUNK, CHUNK), 1)
    return row, col


def _block_diag2(m, lo_half):
    zero = jnp.zeros_like(m)
    return jnp.concatenate([jnp.where(lo_half, m, zero), jnp.where(lo_half, zero, m)], axis=0)


def _unit_lower_inverses(nmats, row, col, lo_half):
    eye = (row == col).astype(F32)
    pair = (row >> 1) == (col >> 1)
    xs = [eye - jnp.where(pair, n, 0.0) for n in nmats]
    for lg in range(2, CHUNK.bit_length()):
        mask = ((row >> lg) == (col >> lg)) & ((row >> (lg - 1)) != (col >> (lg - 1)))
        xb = [x.astype(BF16) for x in xs]
        ys = [_dot(jnp.where(mask, n, 0.0).astype(BF16), _block_diag2(b, lo_half)) for n, b in zip(nmats, xb)]
        yield
        xs = [x - _dot(b, _block_diag2(y.astype(BF16), lo_half)) for x, b, y in zip(xs, xb, ys)]
        yield
    return xs


def _interleave(*stage_generators):
    live = list(stage_generators)
    while live:
        for gen in list(live):
            try:
                next(gen)
            except StopIteration:
                live.remove(gen)


def _gdn_program(pg_ref, pm_ref, nega_ref, dtb_ref, ng_ref, o_ref, s_ref, gq_scr, b_scr, o0_scr, el_scr):
    heads = range(GDN_HEADS)

    def init():
        s_ref[...] = jnp.zeros_like(s_ref)

    row, col = _tri_masks()
    tril = (row >= col).astype(F32)
    row2 = lax.broadcasted_iota(jnp.int32, (CHUNK, LANES), 0)
    lane2 = lax.broadcasted_iota(jnp.int32, (CHUNK, LANES), 1)
    col2 = lane2 & (CHUNK - 1)
    lo_half = lane2 < CHUNK
    lo_half1 = lax.broadcasted_iota(jnp.int32, (1, LANES), 1) < CHUNK
    incl2 = row2 >= col2
    strict2 = row2 > col2
    zero_k = jnp.zeros((CHUNK, GDN_DK), BF16)
    zero_r = jnp.zeros((CHUNK, 2 * GDN_DV), BF16)
    nega = nega_ref[...]
    dtb = dtb_ref[...]
    ng = ng_ref[...]

    def wide(arr, l0):
        n = arr.shape[0]
        return jnp.concatenate([jnp.broadcast_to(arr[:, l0:l0 + 1], (n, LANES)),
                                jnp.broadcast_to(arr[:, l0 + 1:l0 + 2], (n, LANES))], axis=1)

    def halves(w, sel):
        return jnp.where(sel, w[:, 0:LANES], w[:, LANES:])

    def prepare(ci):
        units = []
        for j in range(MIX_CHUNKS):
            c = ci * MIX_CHUNKS + j
            rows = pl.ds(c * CHUNK, CHUNK)
            pmv = pm_ref[rows, :]
            beta_all = _sigmoid(pmv)
            g_all = nega * _softplus(pmv + dtb)
            gam = jnp.dot(tril, g_all, precision=HIGHEST, preferred_element_type=F32)
            gam_t = gam.T
            gam_last = gam[CHUNK - 1:CHUNK, :]
            el_scr[pl.ds(c * SUBLANES, SUBLANES), :] = jnp.broadcast_to(
                jnp.exp2(gam_last), (SUBLANES, LANES))
            for h0 in range(0, GDN_HEADS, 2):
                la = LANE_ALPHA + h0
                gam_w = wide(gam, la)
                r0_t = gam_t[la:la + 1, :]
                r1_t = gam_t[la + 1:la + 2, :]
                units.append(dict(
                    slot=c * GDN_HEADS + h0, rows=rows, h0=h0,
                    beta_w=wide(beta_all, LANE_BETA + h0), gam_w=gam_w, gl_w=wide(gam_last, la),
                    grow=jnp.where(lo_half1, jnp.concatenate([r0_t, r0_t], axis=1),
                                   jnp.concatenate([r1_t, r1_t], axis=1))))

        def cols(u, base):
            return pg_ref[u["rows"], base + u["h0"] * GDN_DK:base + (u["h0"] + 2) * GDN_DK]

        yield
        q2s = [cols(u, 0) for u in units]
        k2s = [cols(u, GDN_QK) for u in units]
        kbd = [jnp.concatenate([jnp.concatenate([k2[:, 0:GDN_DK], zero_k], axis=1),
                                jnp.concatenate([zero_k, k2[:, GDN_DK:]], axis=1)], axis=0) for k2 in k2s]
        qkks = [_dot_nt(jnp.concatenate([q2, k2], axis=0), r) for q2, k2, r in zip(q2s, k2s, kbd)]
        yield
        decays = [jnp.exp2(jnp.where(incl2, halves(u["gam_w"], lo_half) - u["grow"], -jnp.inf)) for u in units]
        nmats = [jnp.where(strict2, halves(u["beta_w"], lo_half) * qkk[CHUNK:2 * CHUNK, :] * d, 0.0)
                 for u, qkk, d in zip(units, qkks, decays)]
        amats = [(qkk[0:CHUNK, :] * d).astype(BF16) for qkk, d in zip(qkks, decays)]
        yield
        egws = [jnp.exp2(u["gam_w"]) for u in units]
        k2f = [k2.astype(F32) for k2 in k2s]
        kds = [(kf * jnp.exp2(u["gl_w"] - u["gam_w"])).astype(BF16) for kf, u in zip(k2f, units)]
        yield
        tinvs = yield from _unit_lower_inverses(nmats, row2, col2, lo_half)
        vbs = [(cols(u, 2 * GDN_QK).astype(F32) * u["beta_w"]).astype(BF16) for u in units]
        kbs = [(kf * (u["beta_w"] * e)).astype(BF16) for kf, u, e in zip(k2f, units, egws)]
        rhss = [jnp.concatenate(
            [jnp.concatenate([vb[:, 0:GDN_DV], kb[:, 0:GDN_DK], zero_r], axis=1),
             jnp.concatenate([zero_r, vb[:, GDN_DV:], kb[:, GDN_DK:]], axis=1)], axis=0)
            for vb, kb in zip(vbs, kbs)]
        yield
        sols = [_dot(x.astype(BF16), r).astype(BF16) for x, r in zip(tinvs, rhss)]
        yield
        sbd = [jnp.concatenate([jnp.concatenate([s[:, 0:2 * GDN_DV], zero_r], axis=1),
                                jnp.concatenate([zero_r, s[:, 2 * GDN_DV:]], axis=1)], axis=0) for s in sols]
        a_uw = [_dot(a, r) for a, r in zip(amats, sbd)]
        yield
        for hh in range(2):
            k_uw = [_dot_tn(kd[:, hh * GDN_DK:(hh + 1) * GDN_DK], s[:, hh * 2 * GDN_DV:(hh + 1) * 2 * GDN_DV])
                    for kd, s in zip(kds, sols)]
            yield
            for u, q2, e, au, ku in zip(units, q2s, egws, a_uw, k_uw):
                qe = (q2[:, hh * GDN_DK:(hh + 1) * GDN_DK].astype(F32) * e[:, hh * LANES:(hh + 1) * LANES])
                base = hh * 2 * GDN_DV
                o0_scr[u["slot"] + hh] = au[:, base:base + GDN_DV]
                b_scr[u["slot"] + hh] = ku[:, 0:GDN_DV]
                gq_scr[u["slot"] + hh] = jnp.concatenate(
                    [ku[:, GDN_DV:], qe - au[:, base + GDN_DV:base + 2 * GDN_DV]], axis=0).astype(BF16)
            yield

    def scan(ci):
        for c in range(ci * MIX_CHUNKS, (ci + 1) * MIX_CHUNKS):
            rows = pl.ds(c * CHUNK, CHUNK)
            e_last = el_scr[pl.ds(c * SUBLANES, SUBLANES), :][0:1, :]
            states = [s_ref[h] for h in heads]
            rs = [_dot(gq_scr[c * GDN_HEADS + h], states[h].astype(BF16)) for h in heads]
            yield
            for h in heads:
                s_ref[h] = (e_last[:, LANE_ALPHA + h:LANE_ALPHA + h + 1] * states[h]
                            - rs[h][0:GDN_DK, :] + b_scr[c * GDN_HEADS + h])
            yield
            for h in heads:
                o = rs[h][GDN_DK:, :] + o0_scr[c * GDN_HEADS + h]
                z = pg_ref[rows, GDN_CONV_CH + h * GDN_DV:GDN_CONV_CH + (h + 1) * GDN_DV].astype(F32)
                on = o * lax.rsqrt(jnp.mean(o * o, axis=-1, keepdims=True) + EPS) * ng
                o_ref[rows, h * GDN_DV:(h + 1) * GDN_DV] = (on * _silu(z)).astype(BF16)
                if h % 4 == 3:
                    yield

    return init, prepare, scan


def _ssd_program(ps_ref, pm_ref, nega_ref, dtb_ref, dsk_ref, ng_ref, o_ref, h_ref):
    def init():
        h_ref[...] = jnp.zeros_like(h_ref)

    row, col = _tri_masks()
    incl = row >= col
    tril = incl.astype(F32)
    nega = nega_ref[...]
    dtb = dtb_ref[...]
    dsk = dsk_ref[...]
    lane = lax.broadcasted_iota(jnp.int32, (CHUNK, LANES), 1)
    lo_half = lane < SSD_P
    incl2 = lax.broadcasted_iota(jnp.int32, (CHUNK, LANES), 0) >= (lane & (SSD_P - 1))
    lane1 = lax.broadcasted_iota(jnp.int32, (1, LANES), 1)
    lo_half1 = lane1 < SSD_P
    heads_per_group = SSD_HEADS // SSD_G
    gw = SSD_INNER // SSD_G

    def pair_cols(arr, l0):
        sel = lo_half if arr.shape[0] == CHUNK else lo_half1
        return jnp.where(sel, arr[:, l0:l0 + 1], arr[:, l0 + 1:l0 + 2])

    pairs_per_group = heads_per_group // 2

    def chunk_group(ci):
        groups = []
        for j in range(MIX_CHUNKS):
            c = ci * MIX_CHUNKS + j
            rows = pl.ds(c * CHUNK, CHUNK)
            pmv = pm_ref[rows, :]
            dt_all = _softplus(pmv + dtb)
            acum = jnp.dot(tril, nega * dt_all, precision=HIGHEST, preferred_element_type=F32)
            info = dict(rows=rows, dt=dt_all, acum=acum, acum_t=acum.T, a_last=acum[CHUNK - 1:CHUNK, :])
            for g in range(SSD_G):
                groups.append(dict(info, g=g))
        yield
        for gr in groups:
            g = gr["g"]
            gr["bg"] = ps_ref[gr["rows"], PS_B + g * SSD_N:PS_B + (g + 1) * SSD_N]
            gr["cg"] = ps_ref[gr["rows"], PS_C + g * SSD_N:PS_C + (g + 1) * SSD_N]
        for gr in groups:
            gr["cb2"] = _dot_nt(gr["cg"], jnp.concatenate([gr["bg"], gr["bg"]], axis=0))
        yield

        units = [dict(gr=gr, p=p, l0=LANE_DT + gr["g"] * heads_per_group + 2 * p)
                 for gr in groups for p in range(pairs_per_group)]
        batches = [units[i:i + len(units) // SSD_STAGE_SPLIT] for i in range(0, len(units), len(units) // SSD_STAGE_SPLIT)]
        for batch in batches:
            for u in batch:
                gr, l0 = u["gr"], u["l0"]
                head0 = l0 - LANE_DT
                u["x"] = ps_ref[gr["rows"], PS_X + head0 * SSD_P:PS_X + (head0 + 2) * SSD_P].astype(F32)
                u["ac"] = pair_cols(gr["acum"], l0)
                u["al"] = pair_cols(gr["a_last"], l0)
                r0_t = gr["acum_t"][l0:l0 + 1, :]
                r1_t = gr["acum_t"][l0 + 1:l0 + 2, :]
                u["arow"] = jnp.where(lo_half1, jnp.concatenate([r0_t, r0_t], axis=1),
                                      jnp.concatenate([r1_t, r1_t], axis=1))
                u["xdt"] = u["x"] * pair_cols(gr["dt"], l0)
            yield
        for batch in batches:
            for u in batch:
                decay = jnp.exp2(jnp.where(incl2, u["ac"] - u["arow"], -jnp.inf))
                u["m"] = (u["gr"]["cb2"] * decay).astype(BF16)
                u["rhs"] = _block_diag2(u["xdt"], lo_half).astype(BF16)
            yield
        for batch in batches:
            for u in batch:
                u["y_diag"] = _dot(u["m"], u["rhs"])
                u["xw"] = (u["xdt"] * jnp.exp2(u["al"] - u["ac"])).astype(BF16)
            yield
        for gr in groups:
            mine = [u for u in units if u["gr"] is gr]
            gr["upd"] = _dot_tn(gr["bg"], jnp.concatenate([u["xw"] for u in mine], axis=1))
            gr["scale"] = jnp.concatenate([jnp.exp2(u["al"]) for u in mine], axis=1)
        yield

        states = [h_ref[g] for g in range(SSD_G)]
        for gr in groups:
            gr["y_off"] = _dot(gr["cg"], states[gr["g"]].astype(BF16))
            states[gr["g"]] = gr["scale"] * states[gr["g"]] + gr["upd"]
        for g in range(SSD_G):
            h_ref[g] = states[g]
        yield

        for gr in groups:
            g = gr["g"]
            mine = [u for u in units if u["gr"] is gr]
            y = jnp.concatenate(
                [u["y_diag"] + jnp.exp2(u["ac"]) * gr["y_off"][:, u["p"] * LANES:(u["p"] + 1) * LANES]
                 + pair_cols(dsk, u["l0"]) * u["x"] for u in mine], axis=1)
            z = ps_ref[gr["rows"], g * gw:(g + 1) * gw].astype(F32)
            yz = y * _silu(z)
            yn = yz * lax.rsqrt(jnp.mean(yz * yz, axis=-1, keepdims=True) + EPS)
            o_ref[gr["rows"], g * gw:(g + 1) * gw] = (yn * ng_ref[:, g * gw:(g + 1) * gw]).astype(BF16)
            yield

    return init, chunk_group


def _mixer_kernel(pg_ref, ps_ref, pm_ref, g_nega, g_dtb, g_ng, s_nega, s_dtb, s_dsk, s_ng, go_ref, so_ref,
                  s_ref, gq_scr, b_scr, o0_scr, el_scr, h_ref, *, tt):
    g_init, g_prepare, g_scan = _gdn_program(pg_ref, pm_ref, g_nega, g_dtb, g_ng, go_ref, s_ref,
                                             gq_scr, b_scr, o0_scr, el_scr)
    s_init, s_chunks = _ssd_program(ps_ref, pm_ref, s_nega, s_dtb, s_dsk, s_ng, so_ref, h_ref)

    @pl.when(pl.program_id(1) == 0)
    def _():
        g_init()
        s_init()

    n_groups = tt // CHUNK // MIX_CHUNKS
    for ci in range(n_groups):
        stages = [g_prepare(ci), s_chunks(ci)]
        if ci > 0:
            stages.append(g_scan(ci - 1))
        _interleave(*stages)
    _interleave(g_scan(n_groups - 1))


def _mixer(pg, ps, pm, gdn_a_log, gdn_dt_bias, gdn_norm_g, ssd_a_log, ssd_dt_bias, ssd_d, ssd_norm_g, bsz, seq):
    tt = min(TT_SCAN, seq)
    nt = seq // tt
    units = tt // CHUNK * GDN_HEADS

    def lanes(v, lane0):
        return jnp.zeros((1, LANES), F32).at[0, lane0:lane0 + v.shape[0]].set(v)

    def tile(width):
        return pl.BlockSpec((tt, width), lambda b, t: (b * nt + t, 0))

    def const(width):
        return pl.BlockSpec((1, width), lambda b, t: (0, 0))

    return pl.pallas_call(
        functools.partial(_mixer_kernel, tt=tt),
        grid=(bsz, nt),
        in_specs=[tile(PG_W), tile(PS_W), tile(LANES), const(LANES), const(LANES), const(GDN_DV),
                  const(LANES), const(LANES), const(LANES), const(SSD_INNER)],
        out_specs=[tile(GDN_V), tile(SSD_INNER)],
        out_shape=[jax.ShapeDtypeStruct((bsz * seq, GDN_V), BF16),
                   jax.ShapeDtypeStruct((bsz * seq, SSD_INNER), BF16)],
        scratch_shapes=[
            pltpu.VMEM((GDN_HEADS, GDN_DK, GDN_DV), F32),
            pltpu.VMEM((units, GDN_DK + CHUNK, GDN_DK), BF16),
            pltpu.VMEM((units, GDN_DK, GDN_DV), F32),
            pltpu.VMEM((units, CHUNK, GDN_DV), F32),
            pltpu.VMEM((tt // CHUNK * SUBLANES, LANES), F32),
            pltpu.VMEM((SSD_G, SSD_N, SSD_INNER // SSD_G), F32),
        ],
        compiler_params=pltpu.CompilerParams(
            dimension_semantics=("parallel", "arbitrary"), vmem_limit_bytes=VMEM_LIMIT),
        name="mixer_scan",
    )(pg, ps, pm, lanes(-LOG2_E * jnp.exp(gdn_a_log), LANE_ALPHA), lanes(gdn_dt_bias, LANE_ALPHA),
      gdn_norm_g.reshape(1, GDN_DV), lanes(-LOG2_E * jnp.exp(ssd_a_log), LANE_DT), lanes(ssd_dt_bias, LANE_DT),
      lanes(ssd_d, LANE_DT), ssd_norm_g.reshape(1, SSD_INNER))


RT_IDX = 0
RT_GATE = TOP_K
RT_POS = 2 * TOP_K


def _outproj_router_kernel(go_ref, so_ref, x_ref, mod_ref, g_ref, wo_ref, rw_ref, rb_ref,
                           xn_ref, h_ref, rt_ref, cnt_ref, run_ref, *, tm):
    i = pl.program_id(0)

    @pl.when(i == 0)
    def _():
        run_ref[...] = jnp.zeros_like(run_ref)

    sub = tm // ROUTER_SPLIT
    lane = lax.broadcasted_iota(jnp.int32, (sub, LANES), 1).astype(F32)
    trow = lax.broadcasted_iota(jnp.int32, (sub, sub), 0)
    tcol = lax.broadcasted_iota(jnp.int32, (sub, sub), 1)
    earlier = (trow > tcol).astype(BF16)
    onehots = [None] * ROUTER_SPLIT

    def part(j):
        rows = slice(j * sub, (j + 1) * sub)
        mix = _dot(go_ref[rows, :], wo_ref[0:GDN_V, :]) + _dot(so_ref[rows, :], wo_ref[GDN_V:, :])
        yield
        xn = x_ref[rows, :] + mod_ref[2:3, :] * mix
        xn_ref[rows, :] = xn
        h = _modulated_norm(xn, g_ref[...], mod_ref[3:4, :], mod_ref[4:5, :])
        h_ref[rows, :] = _pack_rows(h)
        yield
        h_hi = h.astype(BF16)
        h_lo = (h - h_hi.astype(F32)).astype(BF16)
        hw = _dot(h_hi, rw_ref[...])
        work = hw[:, 0:LANES] + hw[:, LANES:] + _dot(h_lo, rw_ref[:, 0:LANES]) + rb_ref[...]
        yield
        tops = []
        idxs = []
        for _ in range(TOP_K):
            m = jnp.max(work, axis=-1, keepdims=True)
            idx = jnp.min(jnp.where(work == m, lane, float(LANES)), axis=-1, keepdims=True)
            work = jnp.where(lane == idx, -jnp.inf, work)
            tops.append(m)
            idxs.append(idx)
            yield
        exps = [jnp.exp(m - tops[0]) for m in tops]
        denom = exps[0] + exps[1] + exps[2] + exps[3]
        onehot = jnp.zeros((sub, LANES), F32)
        for idx in idxs:
            onehot = onehot + (lane == idx).astype(F32)
        onehots[j] = onehot
        yield
        prior = run_ref[0:1, :]
        for jj in range(j):
            prior = prior + jnp.sum(onehots[jj], axis=0, keepdims=True)
        before = _dot(earlier, onehot.astype(BF16)) + prior
        yield
        rt = jnp.zeros((sub, LANES), F32)
        for k in range(TOP_K):
            pos = jnp.sum(jnp.where(lane == idxs[k], before, 0.0), axis=-1, keepdims=True)
            rt = jnp.where(lane == RT_IDX + k, idxs[k], rt)
            rt = jnp.where(lane == RT_GATE + k, exps[k] / denom, rt)
            rt = jnp.where(lane == RT_POS + k, pos, rt)
        rt_ref[rows, :] = rt

    _interleave(*[part(j) for j in range(ROUTER_SPLIT)])
    run = run_ref[0:1, :]
    for onehot in onehots:
        run = run + jnp.sum(onehot, axis=0, keepdims=True)
    run_ref[...] = jnp.broadcast_to(run, run_ref.shape)
    cnt_ref[...] = jnp.broadcast_to(run, cnt_ref.shape)


def _outproj_router(go, so, x2, mod_l, g, w_out, rw, rb, seq):
    n_tok = x2.shape[0]
    tm = min(TM_ROUTER, seq)
    return pl.pallas_call(
        functools.partial(_outproj_router_kernel, tm=tm),
        grid=(n_tok // tm,),
        in_specs=[
            pl.BlockSpec((tm, GDN_V), lambda i: (i, 0)),
            pl.BlockSpec((tm, SSD_INNER), lambda i: (i, 0)),
            pl.BlockSpec((tm, D_MODEL), lambda i: (i, 0)),
            pl.BlockSpec((None, SUBLANES, D_MODEL), lambda i: ((i * tm) // seq, 0, 0)),
            pl.BlockSpec((1, D_MODEL), lambda i: (0, 0)),
            pl.BlockSpec((GDN_V + SSD_INNER, D_MODEL), lambda i: (0, 0)),
            pl.BlockSpec((D_MODEL, 2 * LANES), lambda i: (0, 0)),
            pl.BlockSpec((1, LANES), lambda i: (0, 0)),
        ],
        out_specs=[
            pl.BlockSpec((tm, D_MODEL), lambda i: (i, 0)),
            pl.BlockSpec((tm, ROW_WORDS), lambda i: (i, 0)),
            pl.BlockSpec((tm, LANES), lambda i: (i, 0)),
            pl.BlockSpec((SUBLANES, LANES), lambda i: (0, 0)),
        ],
        out_shape=[
            jax.ShapeDtypeStruct((n_tok, D_MODEL), F32),
            jax.ShapeDtypeStruct((n_tok, ROW_WORDS), U32),
            jax.ShapeDtypeStruct((n_tok, LANES), F32),
            jax.ShapeDtypeStruct((SUBLANES, LANES), F32),
        ],
        scratch_shapes=[pltpu.VMEM((SUBLANES, LANES), F32)],
        compiler_params=pltpu.CompilerParams(
            dimension_semantics=("arbitrary",), vmem_limit_bytes=VMEM_LIMIT),
        name="outproj_router",
    )(go, so, x2, mod_l, g, w_out, rw, rb)


def _sc_workers():
    info = plsc.get_sparse_core_info()
    return info.num_cores, info.num_subcores


def _scatter_rows(src, dest_t, n_rows):
    n_tok, width = src.shape
    n_k = dest_t.shape[0]
    nc, ns = _sc_workers()
    per_w = n_tok // (nc * ns)
    win = min(SC_WINDOW, per_w)
    mesh = plsc.VectorSubcoreMesh(core_axis_name="c", subcore_axis_name="s")

    @functools.partial(
        pl.kernel, mesh=mesh,
        out_type=jax.ShapeDtypeStruct((n_rows, width), src.dtype),
        scratch_types=[pltpu.VMEM((win,), jnp.int32), pltpu.VMEM((win, width), src.dtype)],
    )
    def scatter_kernel(src_hbm, idx_hbm, out_hbm, idx_v, rows_v):
        wid = lax.axis_index("s") * nc + lax.axis_index("c")
        base = wid * per_w

        @pl.loop(0, per_w // win)
        def _(j):
            off = base + j * win
            pltpu.sync_copy(src_hbm.at[pl.ds(off, win)], rows_v)
            for k in range(n_k):
                pltpu.sync_copy(idx_hbm.at[k, pl.ds(off, win)], idx_v)
                pltpu.sync_copy(rows_v, out_hbm.at[idx_v])

    return scatter_kernel(src, dest_t)


def _gather_rows(table, idx):
    n_idx = idx.shape[0]
    width = table.shape[1]
    nc, ns = _sc_workers()
    per_w = n_idx // (nc * ns)
    win = min(SC_WINDOW, per_w)
    mesh = plsc.VectorSubcoreMesh(core_axis_name="c", subcore_axis_name="s")

    @functools.partial(
        pl.kernel, mesh=mesh,
        out_type=jax.ShapeDtypeStruct((n_idx, width), table.dtype),
        scratch_types=[pltpu.VMEM((per_w,), jnp.int32), pltpu.VMEM((win, width), table.dtype)],
    )
    def gather_kernel(table_hbm, idx_hbm, out_hbm, idx_v, rows_v):
        wid = lax.axis_index("s") * nc + lax.axis_index("c")
        base = wid * per_w
        pltpu.sync_copy(idx_hbm.at[pl.ds(base, per_w)], idx_v)

        @pl.loop(0, per_w // win)
        def _(j):
            pltpu.sync_copy(table_hbm.at[idx_v.at[pl.ds(j * win, win)]], rows_v)
            pltpu.sync_copy(rows_v, out_hbm.at[pl.ds(base + j * win, win)])

    return gather_kernel(table, idx)


def _expert_kernel(be_ref, bv_ref, x_ref, wgu_ref, bgu_ref, wd_ref, bd_ref, y_ref, wgu_b, wd_b, *, tr):
    i = pl.program_id(0)
    e = be_ref[i]
    prev = be_ref[jnp.maximum(i - 1, 0)]

    @pl.when((i == 0) | (e != prev))
    def _():
        wgu_b[...] = wgu_ref[...].astype(BF16)
        wd_b[...] = wd_ref[...].astype(BF16)

    valid = bv_ref[i]

    @pl.when(valid == 0)
    def _():
        y_ref[...] = jnp.zeros_like(y_ref)

    @pl.when(valid > 0)
    def _():
        rows = lax.broadcasted_iota(jnp.int32, (tr, D_MODEL), 0)
        x = jnp.where(rows < valid, _unpack_rows(x_ref[...]), 0.0).astype(BF16)
        acc = jnp.zeros((tr, D_MODEL), F32)
        for f in range(0, D_FF, FF_CHUNK):
            gate = _dot(x, wgu_b[:, f:f + FF_CHUNK]) + bgu_ref[:, f:f + FF_CHUNK]
            up = _dot(x, wgu_b[:, D_FF + f:D_FF + f + FF_CHUNK]) + bgu_ref[:, D_FF + f:D_FF + f + FF_CHUNK]
            gate = jnp.minimum(gate, SWIGLU_LIMIT)
            up = jnp.clip(up, -SWIGLU_LIMIT, SWIGLU_LIMIT)
            act = (up + 1.0) * (gate * _sigmoid(gate * SWIGLU_ALPHA))
            acc = acc + _dot(act.astype(BF16), wd_b[f:f + FF_CHUNK, :])
        y_ref[...] = _pack_rows(acc + bd_ref[...])


def _experts(x_rows, blk_expert, blk_valid, w_gu, b_gu, w_down, b_down, layer):
    n_rows, row_w = x_rows.shape
    depth = w_gu.shape[0]
    tr = ROW_BLOCK
    nb = n_rows // tr
    grid_spec = pltpu.PrefetchScalarGridSpec(
        num_scalar_prefetch=2,
        grid=(nb,),
        in_specs=[
            pl.BlockSpec((tr, row_w), lambda i, be, bv: (i, 0)),
            pl.BlockSpec((None, None, D_MODEL, 2 * D_FF), lambda i, be, bv: (layer, be[i], 0, 0)),
            pl.BlockSpec((None, None, 1, 2 * D_FF), lambda i, be, bv: (layer, be[i], 0, 0)),
            pl.BlockSpec((None, None, D_FF, D_MODEL), lambda i, be, bv: (layer, be[i], 0, 0)),
            pl.BlockSpec((None, None, 1, D_MODEL), lambda i, be, bv: (layer, be[i], 0, 0)),
        ],
        out_specs=pl.BlockSpec((tr, row_w), lambda i, be, bv: (i, 0)),
        scratch_shapes=[pltpu.VMEM((D_MODEL, 2 * D_FF), BF16), pltpu.VMEM((D_FF, D_MODEL), BF16)],
    )
    return pl.pallas_call(
        functools.partial(_expert_kernel, tr=tr),
        grid_spec=grid_spec,
        out_shape=jax.ShapeDtypeStruct((n_rows, row_w), x_rows.dtype),
        compiler_params=pltpu.CompilerParams(
            dimension_semantics=("arbitrary",), vmem_limit_bytes=VMEM_LIMIT),
        name="moe_experts",
    )(blk_expert, blk_valid, x_rows, w_gu, b_gu.reshape(depth, N_EXPERTS, 1, 2 * D_FF), w_down,
      b_down.reshape(depth, N_EXPERTS, 1, D_MODEL))


def _combine_kernel(xn_ref, y4_ref, rt_ref, mod_ref, g_ref, o_ref, *, final):
    rt = rt_ref[...]
    acc = rt[:, RT_GATE:RT_GATE + 1] * _unpack_rows(y4_ref[0])
    for k in range(1, TOP_K):
        acc = acc + rt[:, RT_GATE + k:RT_GATE + k + 1] * _unpack_rows(y4_ref[k])
    x = xn_ref[...] + mod_ref[5:6, :] * acc
    if final:
        ms = jnp.mean(x * x, axis=-1, keepdims=True)
        x = x * lax.rsqrt(ms + EPS) * g_ref[...]
    o_ref[...] = x


def _combine(xn, y4, rt, mod_l, final_g, seq, final):
    n_tok = xn.shape[0]
    row_w = y4.shape[2]
    tm = min(TM_PROJ, seq)
    return pl.pallas_call(
        functools.partial(_combine_kernel, final=final),
        grid=(n_tok // tm,),
        in_specs=[
            pl.BlockSpec((tm, D_MODEL), lambda i: (i, 0)),
            pl.BlockSpec((TOP_K, tm, row_w), lambda i: (0, i, 0)),
            pl.BlockSpec((tm, LANES), lambda i: (i, 0)),
            pl.BlockSpec((None, SUBLANES, D_MODEL), lambda i: ((i * tm) // seq, 0, 0)),
            pl.BlockSpec((1, D_MODEL), lambda i: (0, 0)),
        ],
        out_specs=pl.BlockSpec((tm, D_MODEL), lambda i: (i, 0)),
        out_shape=jax.ShapeDtypeStruct((n_tok, D_MODEL), F32),
        compiler_params=pltpu.CompilerParams(
            dimension_semantics=("parallel",), vmem_limit_bytes=VMEM_LIMIT),
        name="moe_combine",
    )(xn, y4, rt, mod_l, final_g)


def _permute_w_in(w_in):
    off = 0
    gdn_qkvz = w_in[:, off:off + PG_W]; off += PG_W
    small_ba = w_in[:, off:off + 2 * GDN_HEADS]; off += 2 * GDN_HEADS
    ssd_zxbc = w_in[:, off:off + PS_W]; off += PS_W
    ssd_dt = w_in[:, off:off + SSD_HEADS]
    pad = jnp.zeros((w_in.shape[0], LANES - 2 * GDN_HEADS - SSD_HEADS), w_in.dtype)
    return jnp.concatenate([gdn_qkvz, ssd_zxbc, small_ba, ssd_dt, pad], axis=1).astype(BF16)


def _moe_schedule(rt, counts_row, n_rows):
    idx = rt[:, RT_IDX:RT_IDX + TOP_K].astype(jnp.int32)
    pos = rt[:, RT_POS:RT_POS + TOP_K].astype(jnp.int32)
    counts = counts_row[0, :N_EXPERTS].astype(jnp.int32)
    padded = (counts + ROW_BLOCK - 1) // ROW_BLOCK * ROW_BLOCK
    pad_end = jnp.cumsum(padded)
    pad_start = pad_end - padded
    experts = jnp.arange(N_EXPERTS, dtype=jnp.int32)
    dest = pos + jnp.sum(jnp.where(idx[..., None] == experts, pad_start, 0), axis=-1)
    blk_start = jnp.arange(n_rows // ROW_BLOCK, dtype=jnp.int32) * ROW_BLOCK
    blk_expert = jnp.minimum(jnp.sum(blk_start[:, None] >= pad_end[None, :], axis=1), N_EXPERTS - 1)
    blk_onehot = blk_expert[:, None] == experts
    blk_valid = (jnp.sum(jnp.where(blk_onehot, counts + pad_start, 0), axis=1) - blk_start)
    blk_valid = jnp.where(blk_start < pad_end[-1], jnp.clip(blk_valid, 0, ROW_BLOCK), 0)
    return dest, blk_expert.astype(jnp.int32), blk_valid.astype(jnp.int32)


def _forward(x, c, ada_w, ada_b, norm1_g, norm2_g, w_in, gdn_conv_w, gdn_a_log, gdn_dt_bias, gdn_norm_g,
             ssd_conv_w, ssd_conv_b, ssd_a_log, ssd_dt_bias, ssd_d, ssd_norm_g, w_out, router_w, router_b,
             moe_w_gu, moe_b_gu, moe_w_down, moe_b_down, final_g):
    bsz, seq, _ = x.shape
    depth = ada_w.shape[0]
    n_tok = bsz * seq
    n_rows = n_tok * TOP_K + N_EXPERTS * ROW_BLOCK
    mod = _ada_mod(c, ada_w, ada_b)
    x2 = x.reshape(n_tok, D_MODEL)
    fg = final_g.reshape(1, D_MODEL)

    def prepare_weights(l, anchor):
        conv_w = jnp.concatenate([gdn_conv_w[l], ssd_conv_w[l]], axis=1) + anchor
        conv_b = jnp.concatenate([jnp.zeros((1, GDN_CONV_CH), F32), ssd_conv_b[l].reshape(1, -1)], axis=1)
        rw = jnp.zeros((D_MODEL, LANES), F32).at[:, :N_EXPERTS].set(router_w[l]) + anchor
        rw_hi = rw.astype(BF16)
        rw = jnp.concatenate([rw_hi, (rw - rw_hi.astype(F32)).astype(BF16)], axis=1)
        rb = jnp.full((1, LANES), -1e30, F32).at[0, :N_EXPERTS].set(router_b[l])
        return (_permute_w_in(w_in[l] + anchor), conv_w, conv_b, rw, rb, (w_out[l] + anchor).astype(BF16))

    prepared = prepare_weights(0, jnp.zeros((), F32))
    for l in range(depth):
        w_perm, conv_w, conv_b, rw, rb, w_out_b = prepared
        pg, ps, pm = _inproj(x2, mod[l], norm1_g[l].reshape(1, D_MODEL), w_perm, conv_w, conv_b, seq)
        go, so = _mixer(pg, ps, pm, gdn_a_log[l], gdn_dt_bias[l], gdn_norm_g[l],
                        ssd_a_log[l], ssd_dt_bias[l], ssd_d[l], ssd_norm_g[l], bsz, seq)
        xn, h2, rt, cnt = _outproj_router(go, so, x2, mod[l], norm2_g[l].reshape(1, D_MODEL),
                                          w_out_b, rw, rb, seq)
        dest, blk_expert, blk_valid = _moe_schedule(rt, cnt, n_rows)
        dest_t = dest.T
        if l + 1 < depth:
            dest_t, anchor = lax.optimization_barrier((dest_t, jnp.zeros((), F32)))
            prepared = prepare_weights(l + 1, anchor)
        x_rows = _scatter_rows(h2, dest_t, n_rows)
        y_rows = _experts(x_rows, blk_expert, blk_valid, moe_w_gu, moe_b_gu, moe_w_down, moe_b_down, l)
        y4 = _gather_rows(y_rows, dest_t.reshape(TOP_K * n_tok)).reshape(TOP_K, n_tok, y_rows.shape[1])
        x2 = _combine(xn, y4, rt, mod[l], fg, seq, final=(l == depth - 1))
    return x2.reshape(bsz, seq, D_MODEL)


def kernel(x, c, ada_w, ada_b, norm1_g, norm2_g, w_in, gdn_conv_w, gdn_a_log, gdn_dt_bias, gdn_norm_g, ssd_conv_w, ssd_conv_b, ssd_a_log, ssd_dt_bias, ssd_d, ssd_norm_g, w_out, router_w, router_b, moe_w_gu, moe_b_gu, moe_w_down, moe_b_down, final_g):
    return _forward(x, c, ada_w, ada_b, norm1_g, norm2_g, w_in, gdn_conv_w, gdn_a_log, gdn_dt_bias, gdn_norm_g,
                    ssd_conv_w, ssd_conv_b, ssd_a_log, ssd_dt_bias, ssd_d, ssd_norm_g, w_out, router_w, router_b,
                    moe_w_gu, moe_b_gu, moe_w_down, moe_b_down, final_g)
```

```python
import functools

import jax
import jax.numpy as jnp
from jax import lax
from jax.experimental import pallas as pl
from jax.experimental.pallas import tpu as pltpu
from jax.experimental.pallas import tpu_sc as plsc

F32 = jnp.float32
BF16 = jnp.bfloat16
U32 = jnp.uint32
HIGHEST = lax.Precision.HIGHEST

D_MODEL = 1024
CHUNK = 64
CONV_K = 4
GDN_HEADS = 8
GDN_DK = 128
GDN_DV = 128
GDN_QK = GDN_HEADS * GDN_DK
GDN_V = GDN_HEADS * GDN_DV
GDN_CONV_CH = 2 * GDN_QK + GDN_V
SSD_P = 64
SSD_HEADS = 16
SSD_G = 2
SSD_N = 128
SSD_INNER = SSD_HEADS * SSD_P
SSD_BC = SSD_G * SSD_N
SSD_CONV_CH = SSD_INNER + 2 * SSD_BC
N_EXPERTS = 32
TOP_K = 4
D_FF = D_MODEL
SWIGLU_ALPHA = 1.702
SWIGLU_LIMIT = 7.0
EPS = 1e-6
LOG2_E = 1.4426950408889634

LANES = 128
SUBLANES = 8
PG_W = GDN_CONV_CH + GDN_V
PS_W = SSD_INNER + SSD_CONV_CH
PS_X = SSD_INNER
PS_B = 2 * SSD_INNER
PS_C = 2 * SSD_INNER + SSD_BC
N_CONV = GDN_CONV_CH + SSD_CONV_CH
LANE_BETA = 0
LANE_ALPHA = GDN_HEADS
LANE_DT = 2 * GDN_HEADS
VMEM_LIMIT = 56 * 1024 * 1024

TM_PROJ = 512
TM_ROUTER = 512
ROUTER_SPLIT = 2
TT_SCAN = 512
MIX_CHUNKS = 4
SSD_STAGE_SPLIT = 4
ROW_BLOCK = 512
FF_CHUNK = 512
CONV_SLAB = 512
SC_WINDOW = 128
ROW_WORDS = D_MODEL // 2


def _dot(a, b):
    return jnp.dot(a, b, preferred_element_type=F32)


def _dot_nt(a, b):
    return lax.dot_general(a, b, (((1,), (1,)), ((), ())), preferred_element_type=F32)


def _dot_tn(a, b):
    return lax.dot_general(a, b, (((0,), (0,)), ((), ())), preferred_element_type=F32)


def _sigmoid(x):
    return 1.0 / (1.0 + jnp.exp(-x))


def _silu(x):
    h = 0.5 * x
    return h + h * jnp.tanh(h)


def _softplus(x):
    return jnp.maximum(x, 0.0) + jnp.log(1.0 + jnp.exp(-jnp.abs(x)))


def _pack_rows(h):
    w = h.shape[1] // 2
    hi = pltpu.bitcast(h[:, :w].astype(BF16).astype(F32), U32)
    lo = pltpu.bitcast(h[:, w:].astype(BF16).astype(F32), U32)
    return hi | (lo >> 16)


def _unpack_rows(p):
    hi = pltpu.bitcast(p & jnp.uint32(0xFFFF0000), F32)
    lo = pltpu.bitcast(p << 16, F32)
    return jnp.concatenate([hi, lo], axis=1)


def _ada_kernel(c_ref, w_ref, b_ref, o_ref):
    c = c_ref[...]
    o_ref[0] = jnp.dot(_silu(c), w_ref[0], precision=HIGHEST, preferred_element_type=F32) + b_ref[0]


def _ada_mod(c, ada_w, ada_b):
    depth = ada_w.shape[0]
    bsz = c.shape[0]
    c8 = jnp.zeros((SUBLANES, D_MODEL), F32).at[:bsz].set(c)
    out = pl.pallas_call(
        _ada_kernel,
        grid=(depth, 6),
        in_specs=[
            pl.BlockSpec((SUBLANES, D_MODEL), lambda l, j: (0, 0)),
            pl.BlockSpec((1, D_MODEL, D_MODEL), lambda l, j: (l, 0, j)),
            pl.BlockSpec((1, 1, D_MODEL), lambda l, j: (l, 0, j)),
        ],
        out_specs=pl.BlockSpec((1, SUBLANES, D_MODEL), lambda l, j: (l, 0, j)),
        out_shape=jax.ShapeDtypeStruct((depth, SUBLANES, 6 * D_MODEL), F32),
        name="ada_mod",
    )(c8, ada_w, ada_b.reshape(depth, 1, 6 * D_MODEL))
    mod = out[:, :bsz].reshape(depth, bsz, 6, D_MODEL)
    return jnp.concatenate([mod, jnp.zeros((depth, bsz, 2, D_MODEL), F32)], axis=2)


def _modulated_norm(x, g, shift, scale):
    ms = jnp.mean(x * x, axis=-1, keepdims=True)
    return (x * lax.rsqrt(ms + EPS) * g) * (1.0 + scale) + shift


def _inproj_kernel(x_ref, mod_ref, g_ref, w_ref, cw_ref, cb_ref, og_ref, os_ref, om_ref,
                   tail_ref, cbuf_ref, *, tm, seq):
    i = pl.program_id(0)

    @pl.when((i * tm) % seq == 0)
    def _():
        tail_ref[...] = jnp.zeros_like(tail_ref)

    h = _modulated_norm(x_ref[...], g_ref[...], mod_ref[0:1, :], mod_ref[1:2, :]).astype(BF16)

    def conv_slab(n, wcol, ccol, bias):
        cs = slice(ccol, ccol + CONV_SLAB)
        pbuf = cbuf_ref.at[n % 2, 0]
        bbuf = cbuf_ref.at[n % 2, 1]
        p = _dot(h, w_ref[:, wcol:wcol + CONV_SLAB])
        pbuf[0:SUBLANES, :] = tail_ref[0, :, cs]
        pbuf[SUBLANES:SUBLANES + tm, :] = p
        tail_ref[0, :, cs] = p[tm - SUBLANES:tm, :]
        p1 = pbuf[SUBLANES - 1:SUBLANES - 1 + tm, :]
        b = p * cw_ref[1:2, cs] + p1 * cw_ref[0:1, cs]
        bbuf[0:SUBLANES, :] = tail_ref[1, :, cs]
        bbuf[SUBLANES:SUBLANES + tm, :] = b
        tail_ref[1, :, cs] = b[tm - SUBLANES:tm, :]
        y = p * cw_ref[3:4, cs] + p1 * cw_ref[2:3, cs] + bbuf[SUBLANES - 2:SUBLANES - 2 + tm, :]
        if bias:
            y = y + cb_ref[:, cs]
        return _silu(y)

    n = 0
    for part, scale in ((0, GDN_DK ** -0.5), (1, 1.0)):
        for j in range(GDN_QK // CONV_SLAB):
            col = part * GDN_QK + j * CONV_SLAB
            y = conv_slab(n, col, col, False)
            n += 1
            for hh in range(CONV_SLAB // GDN_DK):
                yh = y[:, hh * GDN_DK:(hh + 1) * GDN_DK]
                inv = lax.rsqrt(jnp.sum(yh * yh, axis=-1, keepdims=True) + EPS) * scale
                og_ref[:, col + hh * GDN_DK:col + (hh + 1) * GDN_DK] = (yh * inv).astype(BF16)
    for j in range(GDN_V // CONV_SLAB):
        col = 2 * GDN_QK + j * CONV_SLAB
        og_ref[:, col:col + CONV_SLAB] = conv_slab(n, col, col, False).astype(BF16)
        n += 1
    og_ref[:, GDN_CONV_CH:PG_W] = _dot(h, w_ref[:, GDN_CONV_CH:PG_W]).astype(BF16)
    os_ref[:, 0:SSD_INNER] = _dot(h, w_ref[:, PG_W:PG_W + SSD_INNER]).astype(BF16)
    for j in range(SSD_CONV_CH // CONV_SLAB):
        col = j * CONV_SLAB
        os_ref[:, SSD_INNER + col:SSD_INNER + col + CONV_SLAB] = conv_slab(
            n, PG_W + SSD_INNER + col, GDN_CONV_CH + col, True).astype(BF16)
        n += 1
    om_ref[...] = _dot(h, w_ref[:, PG_W + PS_W:])


def _inproj(x2, mod_l, g, w_perm, conv_w, conv_b, seq):
    n_tok = x2.shape[0]
    tm = min(TM_PROJ, seq)
    wtot = w_perm.shape[1]
    return pl.pallas_call(
        functools.partial(_inproj_kernel, tm=tm, seq=seq),
        grid=(n_tok // tm,),
        in_specs=[
            pl.BlockSpec((tm, D_MODEL), lambda i: (i, 0)),
            pl.BlockSpec((None, SUBLANES, D_MODEL), lambda i: ((i * tm) // seq, 0, 0)),
            pl.BlockSpec((1, D_MODEL), lambda i: (0, 0)),
            pl.BlockSpec((D_MODEL, wtot), lambda i: (0, 0)),
            pl.BlockSpec((CONV_K, N_CONV), lambda i: (0, 0)),
            pl.BlockSpec((1, N_CONV), lambda i: (0, 0)),
        ],
        out_specs=[
            pl.BlockSpec((tm, PG_W), lambda i: (i, 0)),
            pl.BlockSpec((tm, PS_W), lambda i: (i, 0)),
            pl.BlockSpec((tm, LANES), lambda i: (i, 0)),
        ],
        out_shape=[
            jax.ShapeDtypeStruct((n_tok, PG_W), BF16),
            jax.ShapeDtypeStruct((n_tok, PS_W), BF16),
            jax.ShapeDtypeStruct((n_tok, LANES), F32),
        ],
        scratch_shapes=[
            pltpu.VMEM((2, SUBLANES, N_CONV), F32),
            pltpu.VMEM((2, 2, SUBLANES + tm, CONV_SLAB), F32),
        ],
        compiler_params=pltpu.CompilerParams(
            dimension_semantics=("arbitrary",), vmem_limit_bytes=VMEM_LIMIT),
        name="norm_inproj",
    )(x2, mod_l, g, w_perm, conv_w, conv_b)


def _tri_masks():
    row = lax.broadcasted_iota(jnp.int32, (CHUNK, CHUNK), 0)
    col = lax.broadcasted_iota(jnp.int32, (CHUNK, CHUNK), 1)
    return row, col


def _block_diag2(m, lo_half):
    zero = jnp.zeros_like(m)
    return jnp.concatenate([jnp.where(lo_half, m, zero), jnp.where(lo_half, zero, m)], axis=0)


def _unit_lower_inverses(nmats, row, col, lo_half):
    eye = (row == col).astype(F32)
    pair = (row >> 1) == (col >> 1)
    xs = [eye - jnp.where(pair, n, 0.0) for n in nmats]
    for lg in range(2, CHUNK.bit_length()):
        mask = ((row >> lg) == (col >> lg)) & ((row >> (lg - 1)) != (col >> (lg - 1)))
        xb = [x.astype(BF16) for x in xs]
        ys = [_dot(jnp.where(mask, n, 0.0).astype(BF16), _block_diag2(b, lo_half)) for n, b in zip(nmats, xb)]
        yield
        xs = [x - _dot(b, _block_diag2(y.astype(BF16), lo_half)) for x, b, y in zip(xs, xb, ys)]
        yield
    return xs


def _interleave(*stage_generators):
    live = list(stage_generators)
    while live:
        for gen in list(live):
            try:
                next(gen)
            except StopIteration:
                live.remove(gen)


def _gdn_program(pg_ref, pm_ref, nega_ref, dtb_ref, ng_ref, o_ref, s_ref, gq_scr, b_scr, o0_scr, el_scr):
    heads = range(GDN_HEADS)

    def init():
        s_ref[...] = jnp.zeros_like(s_ref)

    row, col = _tri_masks()
    tril = (row >= col).astype(F32)
    row2 = lax.broadcasted_iota(jnp.int32, (CHUNK, LANES), 0)
    lane2 = lax.broadcasted_iota(jnp.int32, (CHUNK, LANES), 1)
    col2 = lane2 & (CHUNK - 1)
    lo_half = lane2 < CHUNK
    lo_half1 = lax.broadcasted_iota(jnp.int32, (1, LANES), 1) < CHUNK
    incl2 = row2 >= col2
    strict2 = row2 > col2
    zero_k = jnp.zeros((CHUNK, GDN_DK), BF16)
    zero_r = jnp.zeros((CHUNK, 2 * GDN_DV), BF16)
    nega = nega_ref[...]
    dtb = dtb_ref[...]
    ng = ng_ref[...]

    def wide(arr, l0):
        n = arr.shape[0]
        return jnp.concatenate([jnp.broadcast_to(arr[:, l0:l0 + 1], (n, LANES)),
                                jnp.broadcast_to(arr[:, l0 + 1:l0 + 2], (n, LANES))], axis=1)

    def halves(w, sel):
        return jnp.where(sel, w[:, 0:LANES], w[:, LANES:])

    def prepare(ci):
        units = []
        for j in range(MIX_CHUNKS):
            c = ci * MIX_CHUNKS + j
            rows = pl.ds(c * CHUNK, CHUNK)
            pmv = pm_ref[rows, :]
            beta_all = _sigmoid(pmv)
            g_all = nega * _softplus(pmv + dtb)
            gam = jnp.dot(tril, g_all, precision=HIGHEST, preferred_element_type=F32)
            gam_t = gam.T
            gam_last = gam[CHUNK - 1:CHUNK, :]
            el_scr[pl.ds(c * SUBLANES, SUBLANES), :] = jnp.broadcast_to(
                jnp.exp2(gam_last), (SUBLANES, LANES))
            for h0 in range(0, GDN_HEADS, 2):
                la = LANE_ALPHA + h0
                gam_w = wide(gam, la)
                r0_t = gam_t[la:la + 1, :]
                r1_t = gam_t[la + 1:la + 2, :]
                units.append(dict(
                    slot=c * GDN_HEADS + h0, rows=rows, h0=h0,
                    beta_w=wide(beta_all, LANE_BETA + h0), gam_w=gam_w, gl_w=wide(gam_last, la),
                    grow=jnp.where(lo_half1, jnp.concatenate([r0_t, r0_t], axis=1),
                                   jnp.concatenate([r1_t, r1_t], axis=1))))

        def cols(u, base):
            return pg_ref[u["rows"], base + u["h0"] * GDN_DK:base + (u["h0"] + 2) * GDN_DK]

        yield
        q2s = [cols(u, 0) for u in units]
        k2s = [cols(u, GDN_QK) for u in units]
        kbd = [jnp.concatenate([jnp.concatenate([k2[:, 0:GDN_DK], zero_k], axis=1),
                                jnp.concatenate([zero_k, k2[:, GDN_DK:]], axis=1)], axis=0) for k2 in k2s]
        qkks = [_dot_nt(jnp.concatenate([q2, k2], axis=0), r) for q2, k2, r in zip(q2s, k2s, kbd)]
        yield
        decays = [jnp.exp2(jnp.where(incl2, halves(u["gam_w"], lo_half) - u["grow"], -jnp.inf)) for u in units]
        nmats = [jnp.where(strict2, halves(u["beta_w"], lo_half) * qkk[CHUNK:2 * CHUNK, :] * d, 0.0)
                 for u, qkk, d in zip(units, qkks, decays)]
        amats = [(qkk[0:CHUNK, :] * d).astype(BF16) for qkk, d in zip(qkks, decays)]
        yield
        egws = [jnp.exp2(u["gam_w"]) for u in units]
        k2f = [k2.astype(F32) for k2 in k2s]
        kds = [(kf * jnp.exp2(u["gl_w"] - u["gam_w"])).astype(BF16) for kf, u in zip(k2f, units)]
        yield
        tinvs = yield from _unit_lower_inverses(nmats, row2, col2, lo_half)
        vbs = [(cols(u, 2 * GDN_QK).astype(F32) * u["beta_w"]).astype(BF16) for u in units]
        kbs = [(kf * (u["beta_w"] * e)).astype(BF16) for kf, u, e in zip(k2f, units, egws)]
        rhss = [jnp.concatenate(
            [jnp.concatenate([vb[:, 0:GDN_DV], kb[:, 0:GDN_DK], zero_r], axis=1),
             jnp.concatenate([zero_r, vb[:, GDN_DV:], kb[:, GDN_DK:]], axis=1)], axis=0)
            for vb, kb in zip(vbs, kbs)]
        yield
        sols = [_dot(x.astype(BF16), r).astype(BF16) for x, r in zip(tinvs, rhss)]
        yield
        sbd = [jnp.concatenate([jnp.concatenate([s[:, 0:2 * GDN_DV], zero_r], axis=1),
                                jnp.concatenate([zero_r, s[:, 2 * GDN_DV:]], axis=1)], axis=0) for s in sols]
        a_uw = [_dot(a, r) for a, r in zip(amats, sbd)]
        yield
        for hh in range(2):
            k_uw = [_dot_tn(kd[:, hh * GDN_DK:(hh + 1) * GDN_DK], s[:, hh * 2 * GDN_DV:(hh + 1) * 2 * GDN_DV])
                    for kd, s in zip(kds, sols)]
            yield
            for u, q2, e, au, ku in zip(units, q2s, egws, a_uw, k_uw):
                qe = (q2[:, hh * GDN_DK:(hh + 1) * GDN_DK].astype(F32) * e[:, hh * LANES:(hh + 1) * LANES])
                base = hh * 2 * GDN_DV
                o0_scr[u["slot"] + hh] = au[:, base:base + GDN_DV]
                b_scr[u["slot"] + hh] = ku[:, 0:GDN_DV]
                gq_scr[u["slot"] + hh] = jnp.concatenate(
                    [ku[:, GDN_DV:], qe - au[:, base + GDN_DV:base + 2 * GDN_DV]], axis=0).astype(BF16)
            yield

    def scan(ci):
        for c in range(ci * MIX_CHUNKS, (ci + 1) * MIX_CHUNKS):
            rows = pl.ds(c * CHUNK, CHUNK)
            e_last = el_scr[pl.ds(c * SUBLANES, SUBLANES), :][0:1, :]
            states = [s_ref[h] for h in heads]
            rs = [_dot(gq_scr[c * GDN_HEADS + h], states[h].astype(BF16)) for h in heads]
            yield
            for h in heads:
                s_ref[h] = (e_last[:, LANE_ALPHA + h:LANE_ALPHA + h + 1] * states[h]
                            - rs[h][0:GDN_DK, :] + b_scr[c * GDN_HEADS + h])
            yield
            for h in heads:
                o = rs[h][GDN_DK:, :] + o0_scr[c * GDN_HEADS + h]
                z = pg_ref[rows, GDN_CONV_CH + h * GDN_DV:GDN_CONV_CH + (h + 1) * GDN_DV].astype(F32)
                on = o * lax.rsqrt(jnp.mean(o * o, axis=-1, keepdims=True) + EPS) * ng
                o_ref[rows, h * GDN_DV:(h + 1) * GDN_DV] = (on * _silu(z)).astype(BF16)
                if h % 4 == 3:
                    yield

    return init, prepare, scan


def _ssd_program(ps_ref, pm_ref, nega_ref, dtb_ref, dsk_ref, ng_ref, o_ref, h_ref):
    def init():
        h_ref[...] = jnp.zeros_like(h_ref)

    row, col = _tri_masks()
    incl = row >= col
    tril = incl.astype(F32)
    nega = nega_ref[...]
    dtb = dtb_ref[...]
    dsk = dsk_ref[...]
    lane = lax.broadcasted_iota(jnp.int32, (CHUNK, LANES), 1)
    lo_half = lane < SSD_P
    incl2 = lax.broadcasted_iota(jnp.int32, (CHUNK, LANES), 0) >= (lane & (SSD_P - 1))
    lane1 = lax.broadcasted_iota(jnp.int32, (1, LANES), 1)
    lo_half1 = lane1 < SSD_P
    heads_per_group = SSD_HEADS // SSD_G
    gw = SSD_INNER // SSD_G

    def pair_cols(arr, l0):
        sel = lo_half if arr.shape[0] == CHUNK else lo_half1
        return jnp.where(sel, arr[:, l0:l0 + 1], arr[:, l0 + 1:l0 + 2])

    pairs_per_group = heads_per_group // 2

    def chunk_group(ci):
        groups = []
        for j in range(MIX_CHUNKS):
            c = ci * MIX_CHUNKS + j
            rows = pl.ds(c * CHUNK, CHUNK)
            pmv = pm_ref[rows, :]
            dt_all = _softplus(pmv + dtb)
            acum = jnp.dot(tril, nega * dt_all, precision=HIGHEST, preferred_element_type=F32)
            info = dict(rows=rows, dt=dt_all, acum=acum, acum_t=acum.T, a_last=acum[CHUNK - 1:CHUNK, :])
            for g in range(SSD_G):
                groups.append(dict(info, g=g))
        yield
        for gr in groups:
            g = gr["g"]
            gr["bg"] = ps_ref[gr["rows"], PS_B + g * SSD_N:PS_B + (g + 1) * SSD_N]
            gr["cg"] = ps_ref[gr["rows"], PS_C + g * SSD_N:PS_C + (g + 1) * SSD_N]
        for gr in groups:
            gr["cb2"] = _dot_nt(gr["cg"], jnp.concatenate([gr["bg"], gr["bg"]], axis=0))
        yield

        units = [dict(gr=gr, p=p, l0=LANE_DT + gr["g"] * heads_per_group + 2 * p)
                 for gr in groups for p in range(pairs_per_group)]
        batches = [units[i:i + len(units) // SSD_STAGE_SPLIT] for i in range(0, len(units), len(units) // SSD_STAGE_SPLIT)]
        for batch in batches:
            for u in batch:
                gr, l0 = u["gr"], u["l0"]
                head0 = l0 - LANE_DT
                u["x"] = ps_ref[gr["rows"], PS_X + head0 * SSD_P:PS_X + (head0 + 2) * SSD_P].astype(F32)
                u["ac"] = pair_cols(gr["acum"], l0)
                u["al"] = pair_cols(gr["a_last"], l0)
                r0_t = gr["acum_t"][l0:l0 + 1, :]
                r1_t = gr["acum_t"][l0 + 1:l0 + 2, :]
                u["arow"] = jnp.where(lo_half1, jnp.concatenate([r0_t, r0_t], axis=1),
                                      jnp.concatenate([r1_t, r1_t], axis=1))
                u["xdt"] = u["x"] * pair_cols(gr["dt"], l0)
            yield
        for batch in batches:
            for u in batch:
                decay = jnp.exp2(jnp.where(incl2, u["ac"] - u["arow"], -jnp.inf))
                u["m"] = (u["gr"]["cb2"] * decay).astype(BF16)
                u["rhs"] = _block_diag2(u["xdt"], lo_half).astype(BF16)
            yield
        for batch in batches:
            for u in batch:
                u["y_diag"] = _dot(u["m"], u["rhs"])
                u["xw"] = (u["xdt"] * jnp.exp2(u["al"] - u["ac"])).astype(BF16)
            yield
        for gr in groups:
            mine = [u for u in units if u["gr"] is gr]
            gr["upd"] = _dot_tn(gr["bg"], jnp.concatenate([u["xw"] for u in mine], axis=1))
            gr["scale"] = jnp.concatenate([jnp.exp2(u["al"]) for u in mine], axis=1)
        yield

        states = [h_ref[g] for g in range(SSD_G)]
        for gr in groups:
            gr["y_off"] = _dot(gr["cg"], states[gr["g"]].astype(BF16))
            states[gr["g"]] = gr["scale"] * states[gr["g"]] + gr["upd"]
        for g in range(SSD_G):
            h_ref[g] = states[g]
        yield

        for gr in groups:
            g = gr["g"]
            mine = [u for u in units if u["gr"] is gr]
            y = jnp.concatenate(
                [u["y_diag"] + jnp.exp2(u["ac"]) * gr["y_off"][:, u["p"] * LANES:(u["p"] + 1) * LANES]
                 + pair_cols(dsk, u["l0"]) * u["x"] for u in mine], axis=1)
            z = ps_ref[gr["rows"], g * gw:(g + 1) * gw].astype(F32)
            yz = y * _silu(z)
            yn = yz * lax.rsqrt(jnp.mean(yz * yz, axis=-1, keepdims=True) + EPS)
            o_ref[gr["rows"], g * gw:(g + 1) * gw] = (yn * ng_ref[:, g * gw:(g + 1) * gw]).astype(BF16)
            yield

    return init, chunk_group


def _mixer_kernel(pg_ref, ps_ref, pm_ref, g_nega, g_dtb, g_ng, s_nega, s_dtb, s_dsk, s_ng, go_ref, so_ref,
                  s_ref, gq_scr, b_scr, o0_scr, el_scr, h_ref, *, tt):
    g_init, g_prepare, g_scan = _gdn_program(pg_ref, pm_ref, g_nega, g_dtb, g_ng, go_ref, s_ref,
                                             gq_scr, b_scr, o0_scr, el_scr)
    s_init, s_chunks = _ssd_program(ps_ref, pm_ref, s_nega, s_dtb, s_dsk, s_ng, so_ref, h_ref)

    @pl.when(pl.program_id(1) == 0)
    def _():
        g_init()
        s_init()

    n_groups = tt // CHUNK // MIX_CHUNKS
    for ci in range(n_groups):
        stages = [g_prepare(ci), s_chunks(ci)]
        if ci > 0:
            stages.append(g_scan(ci - 1))
        _interleave(*stages)
    _interleave(g_scan(n_groups - 1))


def _mixer(pg, ps, pm, gdn_a_log, gdn_dt_bias, gdn_norm_g, ssd_a_log, ssd_dt_bias, ssd_d, ssd_norm_g, bsz, seq):
    tt = min(TT_SCAN, seq)
    nt = seq // tt
    units = tt // CHUNK * GDN_HEADS

    def lanes(v, lane0):
        return jnp.zeros((1, LANES), F32).at[0, lane0:lane0 + v.shape[0]].set(v)

    def tile(width):
        return pl.BlockSpec((tt, width), lambda b, t: (b * nt + t, 0))

    def const(width):
        return pl.BlockSpec((1, width), lambda b, t: (0, 0))

    return pl.pallas_call(
        functools.partial(_mixer_kernel, tt=tt),
        grid=(bsz, nt),
        in_specs=[tile(PG_W), tile(PS_W), tile(LANES), const(LANES), const(LANES), const(GDN_DV),
                  const(LANES), const(LANES), const(LANES), const(SSD_INNER)],
        out_specs=[tile(GDN_V), tile(SSD_INNER)],
        out_shape=[jax.ShapeDtypeStruct((bsz * seq, GDN_V), BF16),
                   jax.ShapeDtypeStruct((bsz * seq, SSD_INNER), BF16)],
        scratch_shapes=[
            pltpu.VMEM((GDN_HEADS, GDN_DK, GDN_DV), F32),
            pltpu.VMEM((units, GDN_DK + CHUNK, GDN_DK), BF16),
            pltpu.VMEM((units, GDN_DK, GDN_DV), F32),
            pltpu.VMEM((units, CHUNK, GDN_DV), F32),
            pltpu.VMEM((tt // CHUNK * SUBLANES, LANES), F32),
            pltpu.VMEM((SSD_G, SSD_N, SSD_INNER // SSD_G), F32),
        ],
        compiler_params=pltpu.CompilerParams(
            dimension_semantics=("parallel", "arbitrary"), vmem_limit_bytes=VMEM_LIMIT),
        name="mixer_scan",
    )(pg, ps, pm, lanes(-LOG2_E * jnp.exp(gdn_a_log), LANE_ALPHA), lanes(gdn_dt_bias, LANE_ALPHA),
      gdn_norm_g.reshape(1, GDN_DV), lanes(-LOG2_E * jnp.exp(ssd_a_log), LANE_DT), lanes(ssd_dt_bias, LANE_DT),
      lanes(ssd_d, LANE_DT), ssd_norm_g.reshape(1, SSD_INNER))


RT_IDX = 0
RT_GATE = TOP_K
RT_POS = 2 * TOP_K


def _outproj_router_kernel(go_ref, so_ref, x_ref, mod_ref, g_ref, wo_ref, rw_ref, rb_ref,
                           xn_ref, h_ref, rt_ref, cnt_ref, run_ref, *, tm):
    i = pl.program_id(0)

    @pl.when(i == 0)
    def _():
        run_ref[...] = jnp.zeros_like(run_ref)

    sub = tm // ROUTER_SPLIT
    lane = lax.broadcasted_iota(jnp.int32, (sub, LANES), 1).astype(F32)
    trow = lax.broadcasted_iota(jnp.int32, (sub, sub), 0)
    tcol = lax.broadcasted_iota(jnp.int32, (sub, sub), 1)
    earlier = (trow > tcol).astype(BF16)
    onehots = [None] * ROUTER_SPLIT

    def part(j):
        rows = slice(j * sub, (j + 1) * sub)
        mix = _dot(go_ref[rows, :], wo_ref[0:GDN_V, :]) + _dot(so_ref[rows, :], wo_ref[GDN_V:, :])
        yield
        xn = x_ref[rows, :] + mod_ref[2:3, :] * mix
        xn_ref[rows, :] = xn
        h = _modulated_norm(xn, g_ref[...], mod_ref[3:4, :], mod_ref[4:5, :])
        h_ref[rows, :] = _pack_rows(h)
        yield
        h_hi = h.astype(BF16)
        h_lo = (h - h_hi.astype(F32)).astype(BF16)
        hw = _dot(h_hi, rw_ref[...])
        work = hw[:, 0:LANES] + hw[:, LANES:] + _dot(h_lo, rw_ref[:, 0:LANES]) + rb_ref[...]
        yield
        tops = []
        idxs = []
        for _ in range(TOP_K):
            m = jnp.max(work, axis=-1, keepdims=True)
            idx = jnp.min(jnp.where(work == m, lane, float(LANES)), axis=-1, keepdims=True)
            work = jnp.where(lane == idx, -jnp.inf, work)
            tops.append(m)
            idxs.append(idx)
            yield
        exps = [jnp.exp(m - tops[0]) for m in tops]
        denom = exps[0] + exps[1] + exps[2] + exps[3]
        onehot = jnp.zeros((sub, LANES), F32)
        for idx in idxs:
            onehot = onehot + (lane == idx).astype(F32)
        onehots[j] = onehot
        yield
        prior = run_ref[0:1, :]
        for jj in range(j):
            prior = prior + jnp.sum(onehots[jj], axis=0, keepdims=True)
        before = _dot(earlier, onehot.astype(BF16)) + prior
        yield
        rt = jnp.zeros((sub, LANES), F32)
        for k in range(TOP_K):
            pos = jnp.sum(jnp.where(lane == idxs[k], before, 0.0), axis=-1, keepdims=True)
            rt = jnp.where(lane == RT_IDX + k, idxs[k], rt)
            rt = jnp.where(lane == RT_GATE + k, exps[k] / denom, rt)
            rt = jnp.where(lane == RT_POS + k, pos, rt)
        rt_ref[rows, :] = rt

    _interleave(*[part(j) for j in range(ROUTER_SPLIT)])
    run = run_ref[0:1, :]
    for onehot in onehots:
        run = run + jnp.sum(onehot, axis=0, keepdims=True)
    run_ref[...] = jnp.broadcast_to(run, run_ref.shape)
    cnt_ref[...] = jnp.broadcast_to(run, cnt_ref.shape)


def _outproj_router(go, so, x2, mod_l, g, w_out, rw, rb, seq):
    n_tok = x2.shape[0]
    tm = min(TM_ROUTER, seq)
    return pl.pallas_call(
        functools.partial(_outproj_router_kernel, tm=tm),
        grid=(n_tok // tm,),
        in_specs=[
            pl.BlockSpec((tm, GDN_V), lambda i: (i, 0)),
            pl.BlockSpec((tm, SSD_INNER), lambda i: (i, 0)),
            pl.BlockSpec((tm, D_MODEL), lambda i: (i, 0)),
            pl.BlockSpec((None, SUBLANES, D_MODEL), lambda i: ((i * tm) // seq, 0, 0)),
            pl.BlockSpec((1, D_MODEL), lambda i: (0, 0)),
            pl.BlockSpec((GDN_V + SSD_INNER, D_MODEL), lambda i: (0, 0)),
            pl.BlockSpec((D_MODEL, 2 * LANES), lambda i: (0, 0)),
            pl.BlockSpec((1, LANES), lambda i: (0, 0)),
        ],
        out_specs=[
            pl.BlockSpec((tm, D_MODEL), lambda i: (i, 0)),
            pl.BlockSpec((tm, ROW_WORDS), lambda i: (i, 0)),
            pl.BlockSpec((tm, LANES), lambda i: (i, 0)),
            pl.BlockSpec((SUBLANES, LANES), lambda i: (0, 0)),
        ],
        out_shape=[
            jax.ShapeDtypeStruct((n_tok, D_MODEL), F32),
            jax.ShapeDtypeStruct((n_tok, ROW_WORDS), U32),
            jax.ShapeDtypeStruct((n_tok, LANES), F32),
            jax.ShapeDtypeStruct((SUBLANES, LANES), F32),
        ],
        scratch_shapes=[pltpu.VMEM((SUBLANES, LANES), F32)],
        compiler_params=pltpu.CompilerParams(
            dimension_semantics=("arbitrary",), vmem_limit_bytes=VMEM_LIMIT),
        name="outproj_router",
    )(go, so, x2, mod_l, g, w_out, rw, rb)


def _sc_workers():
    info = plsc.get_sparse_core_info()
    return info.num_cores, info.num_subcores


def _scatter_rows(src, dest_t, n_rows):
    n_tok, width = src.shape
    n_k = dest_t.shape[0]
    nc, ns = _sc_workers()
    per_w = n_tok // (nc * ns)
    win = min(SC_WINDOW, per_w)
    mesh = plsc.VectorSubcoreMesh(core_axis_name="c", subcore_axis_name="s")

    @functools.partial(
        pl.kernel, mesh=mesh,
        out_type=jax.ShapeDtypeStruct((n_rows, width), src.dtype),
        scratch_types=[pltpu.VMEM((n_k, per_w // win, win), jnp.int32), pltpu.VMEM((win, width), src.dtype)],
    )
    def scatter_kernel(src_hbm, idx_hbm, out_hbm, idx_v, rows_v):
        wid = lax.axis_index("s") * nc + lax.axis_index("c")
        base = wid * per_w
        for k in range(n_k):
            pltpu.sync_copy(idx_hbm.at[k, wid], idx_v.at[k])

        @pl.loop(0, per_w // win)
        def _(j):
            pltpu.sync_copy(src_hbm.at[pl.ds(base + j * win, win)], rows_v)
            for k in range(n_k):
                pltpu.sync_copy(rows_v, out_hbm.at[idx_v.at[k, j]])

    return scatter_kernel(src, dest_t.reshape(n_k, nc * ns, per_w // win, win))


def _gather_rows(table, idx):
    n_idx = idx.shape[0]
    width = table.shape[1]
    nc, ns = _sc_workers()
    per_w = n_idx // (nc * ns)
    win = min(SC_WINDOW, per_w)
    mesh = plsc.VectorSubcoreMesh(core_axis_name="c", subcore_axis_name="s")

    @functools.partial(
        pl.kernel, mesh=mesh,
        out_type=jax.ShapeDtypeStruct((n_idx, width), table.dtype),
        scratch_types=[pltpu.VMEM((per_w,), jnp.int32), pltpu.VMEM((win, width), table.dtype)],
    )
    def gather_kernel(table_hbm, idx_hbm, out_hbm, idx_v, rows_v):
        wid = lax.axis_index("s") * nc + lax.axis_index("c")
        base = wid * per_w
        pltpu.sync_copy(idx_hbm.at[pl.ds(base, per_w)], idx_v)

        @pl.loop(0, per_w // win)
        def _(j):
            pltpu.sync_copy(table_hbm.at[idx_v.at[pl.ds(j * win, win)]], rows_v)
            pltpu.sync_copy(rows_v, out_hbm.at[pl.ds(base + j * win, win)])

    return gather_kernel(table, idx)


def _expert_kernel(be_ref, bv_ref, x_ref, wgu_ref, bgu_ref, wd_ref, bd_ref, y_ref, wgu_b, wd_b, *, tr):
    i = pl.program_id(0)
    e = be_ref[i]
    prev = be_ref[jnp.maximum(i - 1, 0)]

    @pl.when((i == 0) | (e != prev))
    def _():
        wgu_b[...] = wgu_ref[...].astype(BF16)
        wd_b[...] = wd_ref[...].astype(BF16)

    valid = bv_ref[i]

    @pl.when(valid == 0)
    def _():
        y_ref[...] = jnp.zeros_like(y_ref)

    @pl.when(valid > 0)
    def _():
        rows = lax.broadcasted_iota(jnp.int32, (tr, D_MODEL), 0)
        x = jnp.where(rows < valid, _unpack_rows(x_ref[...]), 0.0).astype(BF16)
        acc = jnp.zeros((tr, D_MODEL), F32)
        for f in range(0, D_FF, FF_CHUNK):
            gate = _dot(x, wgu_b[:, f:f + FF_CHUNK]) + bgu_ref[:, f:f + FF_CHUNK]
            up = _dot(x, wgu_b[:, D_FF + f:D_FF + f + FF_CHUNK]) + bgu_ref[:, D_FF + f:D_FF + f + FF_CHUNK]
            gate = jnp.minimum(gate, SWIGLU_LIMIT)
            up = jnp.clip(up, -SWIGLU_LIMIT, SWIGLU_LIMIT)
            act = (up + 1.0) * (gate * _sigmoid(gate * SWIGLU_ALPHA))
            acc = acc + _dot(act.astype(BF16), wd_b[f:f + FF_CHUNK, :])
        y_ref[...] = _pack_rows(acc + bd_ref[...])


def _experts(x_rows, blk_expert, blk_valid, w_gu, b_gu, w_down, b_down, layer):
    n_rows, row_w = x_rows.shape
    depth = w_gu.shape[0]
    tr = ROW_BLOCK
    nb = n_rows // tr
    grid_spec = pltpu.PrefetchScalarGridSpec(
        num_scalar_prefetch=2,
        grid=(nb,),
        in_specs=[
            pl.BlockSpec((tr, row_w), lambda i, be, bv: (i, 0)),
            pl.BlockSpec((None, None, D_MODEL, 2 * D_FF), lambda i, be, bv: (layer, be[i], 0, 0)),
            pl.BlockSpec((None, None, 1, 2 * D_FF), lambda i, be, bv: (layer, be[i], 0, 0)),
            pl.BlockSpec((None, None, D_FF, D_MODEL), lambda i, be, bv: (layer, be[i], 0, 0)),
            pl.BlockSpec((None, None, 1, D_MODEL), lambda i, be, bv: (layer, be[i], 0, 0)),
        ],
        out_specs=pl.BlockSpec((tr, row_w), lambda i, be, bv: (i, 0)),
        scratch_shapes=[pltpu.VMEM((D_MODEL, 2 * D_FF), BF16), pltpu.VMEM((D_FF, D_MODEL), BF16)],
    )
    return pl.pallas_call(
        functools.partial(_expert_kernel, tr=tr),
        grid_spec=grid_spec,
        out_shape=jax.ShapeDtypeStruct((n_rows, row_w), x_rows.dtype),
        compiler_params=pltpu.CompilerParams(
            dimension_semantics=("arbitrary",), vmem_limit_bytes=VMEM_LIMIT),
        name="moe_experts",
    )(blk_expert, blk_valid, x_rows, w_gu, b_gu.reshape(depth, N_EXPERTS, 1, 2 * D_FF), w_down,
      b_down.reshape(depth, N_EXPERTS, 1, D_MODEL))


def _combine_kernel(xn_ref, y4_ref, rt_ref, mod_ref, g_ref, o_ref, *, final):
    rt = rt_ref[...]
    acc = rt[:, RT_GATE:RT_GATE + 1] * _unpack_rows(y4_ref[0])
    for k in range(1, TOP_K):
        acc = acc + rt[:, RT_GATE + k:RT_GATE + k + 1] * _unpack_rows(y4_ref[k])
    x = xn_ref[...] + mod_ref[5:6, :] * acc
    if final:
        ms = jnp.mean(x * x, axis=-1, keepdims=True)
        x = x * lax.rsqrt(ms + EPS) * g_ref[...]
    o_ref[...] = x


def _combine(xn, y4, rt, mod_l, final_g, seq, final):
    n_tok = xn.shape[0]
    row_w = y4.shape[2]
    tm = min(TM_PROJ, seq)
    return pl.pallas_call(
        functools.partial(_combine_kernel, final=final),
        grid=(n_tok // tm,),
        in_specs=[
            pl.BlockSpec((tm, D_MODEL), lambda i: (i, 0)),
            pl.BlockSpec((TOP_K, tm, row_w), lambda i: (0, i, 0)),
            pl.BlockSpec((tm, LANES), lambda i: (i, 0)),
            pl.BlockSpec((None, SUBLANES, D_MODEL), lambda i: ((i * tm) // seq, 0, 0)),
            pl.BlockSpec((1, D_MODEL), lambda i: (0, 0)),
        ],
        out_specs=pl.BlockSpec((tm, D_MODEL), lambda i: (i, 0)),
        out_shape=jax.ShapeDtypeStruct((n_tok, D_MODEL), F32),
        compiler_params=pltpu.CompilerParams(
            dimension_semantics=("parallel",), vmem_limit_bytes=VMEM_LIMIT),
        name="moe_combine",
    )(xn, y4, rt, mod_l, final_g)


def _permute_w_in(w_in):
    off = 0
    gdn_qkvz = w_in[:, off:off + PG_W]; off += PG_W
    small_ba = w_in[:, off:off + 2 * GDN_HEADS]; off += 2 * GDN_HEADS
    ssd_zxbc = w_in[:, off:off + PS_W]; off += PS_W
    ssd_dt = w_in[:, off:off + SSD_HEADS]
    pad = jnp.zeros((w_in.shape[0], LANES - 2 * GDN_HEADS - SSD_HEADS), w_in.dtype)
    return jnp.concatenate([gdn_qkvz, ssd_zxbc, small_ba, ssd_dt, pad], axis=1).astype(BF16)


def _moe_schedule(rt, counts_row, n_rows):
    idx = rt[:, RT_IDX:RT_IDX + TOP_K].astype(jnp.int32)
    pos = rt[:, RT_POS:RT_POS + TOP_K].astype(jnp.int32)
    counts = counts_row[0, :N_EXPERTS].astype(jnp.int32)
    padded = (counts + ROW_BLOCK - 1) // ROW_BLOCK * ROW_BLOCK
    pad_end = jnp.cumsum(padded)
    pad_start = pad_end - padded
    experts = jnp.arange(N_EXPERTS, dtype=jnp.int32)
    dest = pos + jnp.sum(jnp.where(idx[..., None] == experts, pad_start, 0), axis=-1)
    blk_start = jnp.arange(n_rows // ROW_BLOCK, dtype=jnp.int32) * ROW_BLOCK
    blk_expert = jnp.minimum(jnp.sum(blk_start[:, None] >= pad_end[None, :], axis=1), N_EXPERTS - 1)
    blk_onehot = blk_expert[:, None] == experts
    blk_valid = (jnp.sum(jnp.where(blk_onehot, counts + pad_start, 0), axis=1) - blk_start)
    blk_valid = jnp.where(blk_start < pad_end[-1], jnp.clip(blk_valid, 0, ROW_BLOCK), 0)
    return dest, blk_expert.astype(jnp.int32), blk_valid.astype(jnp.int32)


def _forward(x, c, ada_w, ada_b, norm1_g, norm2_g, w_in, gdn_conv_w, gdn_a_log, gdn_dt_bias, gdn_norm_g,
             ssd_conv_w, ssd_conv_b, ssd_a_log, ssd_dt_bias, ssd_d, ssd_norm_g, w_out, router_w, router_b,
             moe_w_gu, moe_b_gu, moe_w_down, moe_b_down, final_g):
    bsz, seq, _ = x.shape
    depth = ada_w.shape[0]
    n_tok = bsz * seq
    n_rows = n_tok * TOP_K + N_EXPERTS * ROW_BLOCK
    mod = _ada_mod(c, ada_w, ada_b)
    x2 = x.reshape(n_tok, D_MODEL)
    fg = final_g.reshape(1, D_MODEL)

    def prepare_weights(l, anchor):
        conv_w = jnp.concatenate([gdn_conv_w[l], ssd_conv_w[l]], axis=1) + anchor
        conv_b = jnp.concatenate([jnp.zeros((1, GDN_CONV_CH), F32), ssd_conv_b[l].reshape(1, -1)], axis=1)
        rw = jnp.zeros((D_MODEL, LANES), F32).at[:, :N_EXPERTS].set(router_w[l]) + anchor
        rw_hi = rw.astype(BF16)
        rw = jnp.concatenate([rw_hi, (rw - rw_hi.astype(F32)).astype(BF16)], axis=1)
        rb = jnp.full((1, LANES), -1e30, F32).at[0, :N_EXPERTS].set(router_b[l])
        return (_permute_w_in(w_in[l] + anchor), conv_w, conv_b, rw, rb, (w_out[l] + anchor).astype(BF16))

    prepared = prepare_weights(0, jnp.zeros((), F32))
    for l in range(depth):
        w_perm, conv_w, conv_b, rw, rb, w_out_b = prepared
        pg, ps, pm = _inproj(x2, mod[l], norm1_g[l].reshape(1, D_MODEL), w_perm, conv_w, conv_b, seq)
        go, so = _mixer(pg, ps, pm, gdn_a_log[l], gdn_dt_bias[l], gdn_norm_g[l],
                        ssd_a_log[l], ssd_dt_bias[l], ssd_d[l], ssd_norm_g[l], bsz, seq)
        xn, h2, rt, cnt = _outproj_router(go, so, x2, mod[l], norm2_g[l].reshape(1, D_MODEL),
                                          w_out_b, rw, rb, seq)
        dest, blk_expert, blk_valid = _moe_schedule(rt, cnt, n_rows)
        dest_t = dest.T
        if l + 1 < depth:
            dest_t, anchor = lax.optimization_barrier((dest_t, jnp.zeros((), F32)))
            prepared = prepare_weights(l + 1, anchor)
        x_rows = _scatter_rows(h2, dest_t, n_rows)
        y_rows = _experts(x_rows, blk_expert, blk_valid, moe_w_gu, moe_b_gu, moe_w_down, moe_b_down, l)
        y4 = _gather_rows(y_rows, dest_t.reshape(TOP_K * n_tok)).reshape(TOP_K, n_tok, y_rows.shape[1])
        x2 = _combine(xn, y4, rt, mod[l], fg, seq, final=(l == depth - 1))
    return x2.reshape(bsz, seq, D_MODEL)


def kernel(x, c, ada_w, ada_b, norm1_g, norm2_g, w_in, gdn_conv_w, gdn_a_log, gdn_dt_bias, gdn_norm_g, ssd_conv_w, ssd_conv_b, ssd_a_log, ssd_dt_bias, ssd_d, ssd_norm_g, w_out, router_w, router_b, moe_w_gu, moe_b_gu, moe_w_down, moe_b_down, final_g):
    return _forward(x, c, ada_w, ada_b, norm1_g, norm2_g, w_in, gdn_conv_w, gdn_a_log, gdn_dt_bias, gdn_norm_g,
                    ssd_conv_w, ssd_conv_b, ssd_a_log, ssd_dt_bias, ssd_d, ssd_norm_g, w_out, router_w, router_b,
                    moe_w_gu, moe_b_gu, moe_w_down, moe_b_down, final_g)
```

```python
import functools

import jax
import jax.numpy as jnp
from jax import lax
from jax.experimental import pallas as pl
from jax.experimental.pallas import tpu as pltpu
from jax.experimental.pallas import tpu_sc as plsc

F32 = jnp.float32
BF16 = jnp.bfloat16
U32 = jnp.uint32
HIGHEST = lax.Precision.HIGHEST

D_MODEL = 1024
CHUNK = 64
CONV_K = 4
GDN_HEADS = 8
GDN_DK = 128
GDN_DV = 128
GDN_QK = GDN_HEADS * GDN_DK
GDN_V = GDN_HEADS * GDN_DV
GDN_CONV_CH = 2 * GDN_QK + GDN_V
SSD_P = 64
SSD_HEADS = 16
SSD_G = 2
SSD_N = 128
SSD_INNER = SSD_HEADS * SSD_P
SSD_BC = SSD_G * SSD_N
SSD_CONV_CH = SSD_INNER + 2 * SSD_BC
N_EXPERTS = 32
TOP_K = 4
D_FF = D_MODEL
SWIGLU_ALPHA = 1.702
SWIGLU_LIMIT = 7.0
EPS = 1e-6
LOG2_E = 1.4426950408889634

LANES = 128
SUBLANES = 8
PG_W = GDN_CONV_CH + GDN_V
PS_W = SSD_INNER + SSD_CONV_CH
PS_X = SSD_INNER
PS_B = 2 * SSD_INNER
PS_C = 2 * SSD_INNER + SSD_BC
N_CONV = GDN_CONV_CH + SSD_CONV_CH
LANE_BETA = 0
LANE_ALPHA = GDN_HEADS
LANE_DT = 2 * GDN_HEADS
VMEM_LIMIT = 56 * 1024 * 1024

TM_PROJ = 512
TM_ROUTER = 512
ROUTER_SPLIT = 2
TT_SCAN = 512
MIX_CHUNKS = 4
SSD_STAGE_SPLIT = 4
ROW_BLOCK = 1024
SUB_BLOCK = 512
FF_CHUNK = 512
CONV_SLAB = 512
SC_WINDOW = 128
ROW_WORDS = D_MODEL // 2


def _dot(a, b):
    return jnp.dot(a, b, preferred_element_type=F32)


def _dot_nt(a, b):
    return lax.dot_general(a, b, (((1,), (1,)), ((), ())), preferred_element_type=F32)


def _dot_tn(a, b):
    return lax.dot_general(a, b, (((0,), (0,)), ((), ())), preferred_element_type=F32)


def _sigmoid(x):
    return 1.0 / (1.0 + jnp.exp(-x))


def _silu(x):
    h = 0.5 * x
    return h + h * jnp.tanh(h)


def _softplus(x):
    return jnp.maximum(x, 0.0) + jnp.log(1.0 + jnp.exp(-jnp.abs(x)))


def _pack_rows(h):
    w = h.shape[1] // 2
    hi = pltpu.bitcast(h[:, :w].astype(BF16).astype(F32), U32)
    lo = pltpu.bitcast(h[:, w:].astype(BF16).astype(F32), U32)
    return hi | (lo >> 16)


def _unpack_rows(p):
    hi = pltpu.bitcast(p & jnp.uint32(0xFFFF0000), F32)
    lo = pltpu.bitcast(p << 16, F32)
    return jnp.concatenate([hi, lo], axis=1)


def _ada_kernel(c_ref, w_ref, b_ref, o_ref):
    c = c_ref[...]
    o_ref[0] = jnp.dot(_silu(c), w_ref[0], precision=HIGHEST, preferred_element_type=F32) + b_ref[0]


def _ada_mod(c, ada_w, ada_b):
    depth = ada_w.shape[0]
    bsz = c.shape[0]
    c8 = jnp.zeros((SUBLANES, D_MODEL), F32).at[:bsz].set(c)
    out = pl.pallas_call(
        _ada_kernel,
        grid=(depth, 6),
        in_specs=[
            pl.BlockSpec((SUBLANES, D_MODEL), lambda l, j: (0, 0)),
            pl.BlockSpec((1, D_MODEL, D_MODEL), lambda l, j: (l, 0, j)),
            pl.BlockSpec((1, 1, D_MODEL), lambda l, j: (l, 0, j)),
        ],
        out_specs=pl.BlockSpec((1, SUBLANES, D_MODEL), lambda l, j: (l, 0, j)),
        out_shape=jax.ShapeDtypeStruct((depth, SUBLANES, 6 * D_MODEL), F32),
        name="ada_mod",
    )(c8, ada_w, ada_b.reshape(depth, 1, 6 * D_MODEL))
    mod = out[:, :bsz].reshape(depth, bsz, 6, D_MODEL)
    return jnp.concatenate([mod, jnp.zeros((depth, bsz, 2, D_MODEL), F32)], axis=2)


def _modulated_norm(x, g, shift, scale):
    ms = jnp.mean(x * x, axis=-1, keepdims=True)
    return (x * lax.rsqrt(ms + EPS) * g) * (1.0 + scale) + shift


def _inproj_kernel(x_ref, mod_ref, g_ref, w_ref, cw_ref, cb_ref, og_ref, os_ref, om_ref,
                   tail_ref, cbuf_ref, *, tm, seq):
    i = pl.program_id(0)

    @pl.when((i * tm) % seq == 0)
    def _():
        tail_ref[...] = jnp.zeros_like(tail_ref)

    h = _modulated_norm(x_ref[...], g_ref[...], mod_ref[0:1, :], mod_ref[1:2, :]).astype(BF16)

    def conv_slab(n, wcol, ccol, bias):
        cs = slice(ccol, ccol + CONV_SLAB)
        pbuf = cbuf_ref.at[n % 2, 0]
        bbuf = cbuf_ref.at[n % 2, 1]
        p = _dot(h, w_ref[:, wcol:wcol + CONV_SLAB])
        pbuf[0:SUBLANES, :] = tail_ref[0, :, cs]
        pbuf[SUBLANES:SUBLANES + tm, :] = p
        tail_ref[0, :, cs] = p[tm - SUBLANES:tm, :]
        p1 = pbuf[SUBLANES - 1:SUBLANES - 1 + tm, :]
        b = p * cw_ref[1:2, cs] + p1 * cw_ref[0:1, cs]
        bbuf[0:SUBLANES, :] = tail_ref[1, :, cs]
        bbuf[SUBLANES:SUBLANES + tm, :] = b
        tail_ref[1, :, cs] = b[tm - SUBLANES:tm, :]
        y = p * cw_ref[3:4, cs] + p1 * cw_ref[2:3, cs] + bbuf[SUBLANES - 2:SUBLANES - 2 + tm, :]
        if bias:
            y = y + cb_ref[:, cs]
        return _silu(y)

    n = 0
    for part, scale in ((0, GDN_DK ** -0.5), (1, 1.0)):
        for j in range(GDN_QK // CONV_SLAB):
            col = part * GDN_QK + j * CONV_SLAB
            y = conv_slab(n, col, col, False)
            n += 1
            for hh in range(CONV_SLAB // GDN_DK):
                yh = y[:, hh * GDN_DK:(hh + 1) * GDN_DK]
                inv = lax.rsqrt(jnp.sum(yh * yh, axis=-1, keepdims=True) + EPS) * scale
                og_ref[:, col + hh * GDN_DK:col + (hh + 1) * GDN_DK] = (yh * inv).astype(BF16)
    for j in range(GDN_V // CONV_SLAB):
        col = 2 * GDN_QK + j * CONV_SLAB
        og_ref[:, col:col + CONV_SLAB] = conv_slab(n, col, col, False).astype(BF16)
        n += 1
    og_ref[:, GDN_CONV_CH:PG_W] = _dot(h, w_ref[:, GDN_CONV_CH:PG_W]).astype(BF16)
    os_ref[:, 0:SSD_INNER] = _dot(h, w_ref[:, PG_W:PG_W + SSD_INNER]).astype(BF16)
    for j in range(SSD_CONV_CH // CONV_SLAB):
        col = j * CONV_SLAB
        os_ref[:, SSD_INNER + col:SSD_INNER + col + CONV_SLAB] = conv_slab(
            n, PG_W + SSD_INNER + col, GDN_CONV_CH + col, True).astype(BF16)
        n += 1
    om_ref[...] = _dot(h, w_ref[:, PG_W + PS_W:])


def _inproj(x2, mod_l, g, w_perm, conv_w, conv_b, seq):
    n_tok = x2.shape[0]
    tm = min(TM_PROJ, seq)
    wtot = w_perm.shape[1]
    return pl.pallas_call(
        functools.partial(_inproj_kernel, tm=tm, seq=seq),
        grid=(n_tok // tm,),
        in_specs=[
            pl.BlockSpec((tm, D_MODEL), lambda i: (i, 0)),
            pl.BlockSpec((None, SUBLANES, D_MODEL), lambda i: ((i * tm) // seq, 0, 0)),
            pl.BlockSpec((1, D_MODEL), lambda i: (0, 0)),
            pl.BlockSpec((D_MODEL, wtot), lambda i: (0, 0)),
            pl.BlockSpec((CONV_K, N_CONV), lambda i: (0, 0)),
            pl.BlockSpec((1, N_CONV), lambda i: (0, 0)),
        ],
        out_specs=[
            pl.BlockSpec((tm, PG_W), lambda i: (i, 0)),
            pl.BlockSpec((tm, PS_W), lambda i: (i, 0)),
            pl.BlockSpec((tm, LANES), lambda i: (i, 0)),
        ],
        out_shape=[
            jax.ShapeDtypeStruct((n_tok, PG_W), BF16),
            jax.ShapeDtypeStruct((n_tok, PS_W), BF16),
            jax.ShapeDtypeStruct((n_tok, LANES), F32),
        ],
        scratch_shapes=[
            pltpu.VMEM((2, SUBLANES, N_CONV), F32),
            pltpu.VMEM((2, 2, SUBLANES + tm, CONV_SLAB), F32),
        ],
        compiler_params=pltpu.CompilerParams(
            dimension_semantics=("arbitrary",), vmem_limit_bytes=VMEM_LIMIT),
        name="norm_inproj",
    )(x2, mod_l, g, w_perm, conv_w, conv_b)


def _tri_masks():
    row = lax.broadcasted_iota(jnp.int32, (CHUNK, CHUNK), 0)
    col = lax.broadcasted_iota(jnp.int32, (CHUNK, CHUNK), 1)
    return row, col


def _block_diag2(m, lo_half):
    zero = jnp.zeros_like(m)
    return jnp.concatenate([jnp.where(lo_half, m, zero), jnp.where(lo_half, zero, m)], axis=0)


def _unit_lower_inverses(nmats, row, col, lo_half):
    eye = (row == col).astype(F32)
    pair = (row >> 1) == (col >> 1)
    xs = [eye - jnp.where(pair, n, 0.0) for n in nmats]
    for lg in range(2, CHUNK.bit_length()):
        mask = ((row >> lg) == (col >> lg)) & ((row >> (lg - 1)) != (col >> (lg - 1)))
        xb = [x.astype(BF16) for x in xs]
        ys = [_dot(jnp.where(mask, n, 0.0).astype(BF16), _block_diag2(b, lo_half)) for n, b in zip(nmats, xb)]
        yield
        xs = [x - _dot(b, _block_diag2(y.astype(BF16), lo_half)) for x, b, y in zip(xs, xb, ys)]
        yield
    return xs


def _interleave(*stage_generators):
    live = list(stage_generators)
    while live:
        for gen in list(live):
            try:
                next(gen)
            except StopIteration:
                live.remove(gen)


def _gdn_program(pg_ref, pm_ref, nega_ref, dtb_ref, ng_ref, o_ref, s_ref, gq_scr, b_scr, o0_scr, el_scr):
    heads = range(GDN_HEADS)

    def init():
        s_ref[...] = jnp.zeros_like(s_ref)

    row, col = _tri_masks()
    tril = (row >= col).astype(F32)
    row2 = lax.broadcasted_iota(jnp.int32, (CHUNK, LANES), 0)
    lane2 = lax.broadcasted_iota(jnp.int32, (CHUNK, LANES), 1)
    col2 = lane2 & (CHUNK - 1)
    lo_half = lane2 < CHUNK
    lo_half1 = lax.broadcasted_iota(jnp.int32, (1, LANES), 1) < CHUNK
    incl2 = row2 >= col2
    strict2 = row2 > col2
    zero_k = jnp.zeros((CHUNK, GDN_DK), BF16)
    zero_r = jnp.zeros((CHUNK, 2 * GDN_DV), BF16)
    nega = nega_ref[...]
    dtb = dtb_ref[...]
    ng = ng_ref[...]

    def wide(arr, l0):
        n = arr.shape[0]
        return jnp.concatenate([jnp.broadcast_to(arr[:, l0:l0 + 1], (n, LANES)),
                                jnp.broadcast_to(arr[:, l0 + 1:l0 + 2], (n, LANES))], axis=1)

    def halves(w, sel):
        return jnp.where(sel, w[:, 0:LANES], w[:, LANES:])

    def prepare(ci):
        units = []
        for j in range(MIX_CHUNKS):
            c = ci * MIX_CHUNKS + j
            rows = pl.ds(c * CHUNK, CHUNK)
            pmv = pm_ref[rows, :]
            beta_all = _sigmoid(pmv)
            g_all = nega * _softplus(pmv + dtb)
            gam = jnp.dot(tril, g_all, precision=HIGHEST, preferred_element_type=F32)
            gam_t = gam.T
            gam_last = gam[CHUNK - 1:CHUNK, :]
            el_scr[pl.ds(c * SUBLANES, SUBLANES), :] = jnp.broadcast_to(
                jnp.exp2(gam_last), (SUBLANES, LANES))
            for h0 in range(0, GDN_HEADS, 2):
                la = LANE_ALPHA + h0
                gam_w = wide(gam, la)
                r0_t = gam_t[la:la + 1, :]
                r1_t = gam_t[la + 1:la + 2, :]
                units.append(dict(
                    slot=c * GDN_HEADS + h0, rows=rows, h0=h0,
                    beta_w=wide(beta_all, LANE_BETA + h0), gam_w=gam_w, gl_w=wide(gam_last, la),
                    grow=jnp.where(lo_half1, jnp.concatenate([r0_t, r0_t], axis=1),
                                   jnp.concatenate([r1_t, r1_t], axis=1))))

        def cols(u, base):
            return pg_ref[u["rows"], base + u["h0"] * GDN_DK:base + (u["h0"] + 2) * GDN_DK]

        yield
        q2s = [cols(u, 0) for u in units]
        k2s = [cols(u, GDN_QK) for u in units]
        kbd = [jnp.concatenate([jnp.concatenate([k2[:, 0:GDN_DK], zero_k], axis=1),
                                jnp.concatenate([zero_k, k2[:, GDN_DK:]], axis=1)], axis=0) for k2 in k2s]
        qkks = [_dot_nt(jnp.concatenate([q2, k2], axis=0), r) for q2, k2, r in zip(q2s, k2s, kbd)]
        yield
        decays = [jnp.exp2(jnp.where(incl2, halves(u["gam_w"], lo_half) - u["grow"], -jnp.inf)) for u in units]
        nmats = [jnp.where(strict2, halves(u["beta_w"], lo_half) * qkk[CHUNK:2 * CHUNK, :] * d, 0.0)
                 for u, qkk, d in zip(units, qkks, decays)]
        amats = [(qkk[0:CHUNK, :] * d).astype(BF16) for qkk, d in zip(qkks, decays)]
        yield
        egws = [jnp.exp2(u["gam_w"]) for u in units]
        k2f = [k2.astype(F32) for k2 in k2s]
        kds = [(kf * jnp.exp2(u["gl_w"] - u["gam_w"])).astype(BF16) for kf, u in zip(k2f, units)]
        yield
        tinvs = yield from _unit_lower_inverses(nmats, row2, col2, lo_half)
        vbs = [(cols(u, 2 * GDN_QK).astype(F32) * u["beta_w"]).astype(BF16) for u in units]
        kbs = [(kf * (u["beta_w"] * e)).astype(BF16) for kf, u, e in zip(k2f, units, egws)]
        rhss = [jnp.concatenate(
            [jnp.concatenate([vb[:, 0:GDN_DV], kb[:, 0:GDN_DK], zero_r], axis=1),
             jnp.concatenate([zero_r, vb[:, GDN_DV:], kb[:, GDN_DK:]], axis=1)], axis=0)
            for vb, kb in zip(vbs, kbs)]
        yield
        sols = [_dot(x.astype(BF16), r).astype(BF16) for x, r in zip(tinvs, rhss)]
        yield
        sbd = [jnp.concatenate([jnp.concatenate([s[:, 0:2 * GDN_DV], zero_r], axis=1),
                                jnp.concatenate([zero_r, s[:, 2 * GDN_DV:]], axis=1)], axis=0) for s in sols]
        a_uw = [_dot(a, r) for a, r in zip(amats, sbd)]
        yield
        for hh in range(2):
            k_uw = [_dot_tn(kd[:, hh * GDN_DK:(hh + 1) * GDN_DK], s[:, hh * 2 * GDN_DV:(hh + 1) * 2 * GDN_DV])
                    for kd, s in zip(kds, sols)]
            yield
            for u, q2, e, au, ku in zip(units, q2s, egws, a_uw, k_uw):
                qe = (q2[:, hh * GDN_DK:(hh + 1) * GDN_DK].astype(F32) * e[:, hh * LANES:(hh + 1) * LANES])
                base = hh * 2 * GDN_DV
                o0_scr[u["slot"] + hh] = au[:, base:base + GDN_DV]
                b_scr[u["slot"] + hh] = ku[:, 0:GDN_DV]
                gq_scr[u["slot"] + hh] = jnp.concatenate(
                    [ku[:, GDN_DV:], qe - au[:, base + GDN_DV:base + 2 * GDN_DV]], axis=0).astype(BF16)
            yield

    def scan(ci):
        for c in range(ci * MIX_CHUNKS, (ci + 1) * MIX_CHUNKS):
            rows = pl.ds(c * CHUNK, CHUNK)
            e_last = el_scr[pl.ds(c * SUBLANES, SUBLANES), :][0:1, :]
            states = [s_ref[h] for h in heads]
            rs = [_dot(gq_scr[c * GDN_HEADS + h], states[h].astype(BF16)) for h in heads]
            yield
            for h in heads:
                s_ref[h] = (e_last[:, LANE_ALPHA + h:LANE_ALPHA + h + 1] * states[h]
                            - rs[h][0:GDN_DK, :] + b_scr[c * GDN_HEADS + h])
            yield
            for h in heads:
                o = rs[h][GDN_DK:, :] + o0_scr[c * GDN_HEADS + h]
                z = pg_ref[rows, GDN_CONV_CH + h * GDN_DV:GDN_CONV_CH + (h + 1) * GDN_DV].astype(F32)
                on = o * lax.rsqrt(jnp.mean(o * o, axis=-1, keepdims=True) + EPS) * ng
                o_ref[rows, h * GDN_DV:(h + 1) * GDN_DV] = (on * _silu(z)).astype(BF16)
                if h % 4 == 3:
                    yield

    return init, prepare, scan


def _ssd_program(ps_ref, pm_ref, nega_ref, dtb_ref, dsk_ref, ng_ref, o_ref, h_ref):
    def init():
        h_ref[...] = jnp.zeros_like(h_ref)

    row, col = _tri_masks()
    incl = row >= col
    tril = incl.astype(F32)
    nega = nega_ref[...]
    dtb = dtb_ref[...]
    dsk = dsk_ref[...]
    lane = lax.broadcasted_iota(jnp.int32, (CHUNK, LANES), 1)
    lo_half = lane < SSD_P
    incl2 = lax.broadcasted_iota(jnp.int32, (CHUNK, LANES), 0) >= (lane & (SSD_P - 1))
    lane1 = lax.broadcasted_iota(jnp.int32, (1, LANES), 1)
    lo_half1 = lane1 < SSD_P
    heads_per_group = SSD_HEADS // SSD_G
    gw = SSD_INNER // SSD_G

    def pair_cols(arr, l0):
        sel = lo_half if arr.shape[0] == CHUNK else lo_half1
        return jnp.where(sel, arr[:, l0:l0 + 1], arr[:, l0 + 1:l0 + 2])

    pairs_per_group = heads_per_group // 2

    def chunk_group(ci):
        groups = []
        for j in range(MIX_CHUNKS):
            c = ci * MIX_CHUNKS + j
            rows = pl.ds(c * CHUNK, CHUNK)
            pmv = pm_ref[rows, :]
            dt_all = _softplus(pmv + dtb)
            acum = jnp.dot(tril, nega * dt_all, precision=HIGHEST, preferred_element_type=F32)
            info = dict(rows=rows, dt=dt_all, acum=acum, acum_t=acum.T, a_last=acum[CHUNK - 1:CHUNK, :])
            for g in range(SSD_G):
                groups.append(dict(info, g=g))
        yield
        for gr in groups:
            g = gr["g"]
            gr["bg"] = ps_ref[gr["rows"], PS_B + g * SSD_N:PS_B + (g + 1) * SSD_N]
            gr["cg"] = ps_ref[gr["rows"], PS_C + g * SSD_N:PS_C + (g + 1) * SSD_N]
        for gr in groups:
            gr["cb2"] = _dot_nt(gr["cg"], jnp.concatenate([gr["bg"], gr["bg"]], axis=0))
        yield

        units = [dict(gr=gr, p=p, l0=LANE_DT + gr["g"] * heads_per_group + 2 * p)
                 for gr in groups for p in range(pairs_per_group)]
        batches = [units[i:i + len(units) // SSD_STAGE_SPLIT] for i in range(0, len(units), len(units) // SSD_STAGE_SPLIT)]
        for batch in batches:
            for u in batch:
                gr, l0 = u["gr"], u["l0"]
                head0 = l0 - LANE_DT
                u["x"] = ps_ref[gr["rows"], PS_X + head0 * SSD_P:PS_X + (head0 + 2) * SSD_P].astype(F32)
                u["ac"] = pair_cols(gr["acum"], l0)
                u["al"] = pair_cols(gr["a_last"], l0)
                r0_t = gr["acum_t"][l0:l0 + 1, :]
                r1_t = gr["acum_t"][l0 + 1:l0 + 2, :]
                u["arow"] = jnp.where(lo_half1, jnp.concatenate([r0_t, r0_t], axis=1),
                                      jnp.concatenate([r1_t, r1_t], axis=1))
                u["xdt"] = u["x"] * pair_cols(gr["dt"], l0)
            yield
        for batch in batches:
            for u in batch:
                decay = jnp.exp2(jnp.where(incl2, u["ac"] - u["arow"], -jnp.inf))
                u["m"] = (u["gr"]["cb2"] * decay).astype(BF16)
                u["rhs"] = _block_diag2(u["xdt"], lo_half).astype(BF16)
            yield
        for batch in batches:
            for u in batch:
                u["y_diag"] = _dot(u["m"], u["rhs"])
                u["xw"] = (u["xdt"] * jnp.exp2(u["al"] - u["ac"])).astype(BF16)
            yield
        for gr in groups:
            mine = [u for u in units if u["gr"] is gr]
            gr["upd"] = _dot_tn(gr["bg"], jnp.concatenate([u["xw"] for u in mine], axis=1))
            gr["scale"] = jnp.concatenate([jnp.exp2(u["al"]) for u in mine], axis=1)
        yield

        states = [h_ref[g] for g in range(SSD_G)]
        for gr in groups:
            gr["y_off"] = _dot(gr["cg"], states[gr["g"]].astype(BF16))
            states[gr["g"]] = gr["scale"] * states[gr["g"]] + gr["upd"]
        for g in range(SSD_G):
            h_ref[g] = states[g]
        yield

        for gr in groups:
            g = gr["g"]
            mine = [u for u in units if u["gr"] is gr]
            y = jnp.concatenate(
                [u["y_diag"] + jnp.exp2(u["ac"]) * gr["y_off"][:, u["p"] * LANES:(u["p"] + 1) * LANES]
                 + pair_cols(dsk, u["l0"]) * u["x"] for u in mine], axis=1)
            z = ps_ref[gr["rows"], g * gw:(g + 1) * gw].astype(F32)
            yz = y * _silu(z)
            yn = yz * lax.rsqrt(jnp.mean(yz * yz, axis=-1, keepdims=True) + EPS)
            o_ref[gr["rows"], g * gw:(g + 1) * gw] = (yn * ng_ref[:, g * gw:(g + 1) * gw]).astype(BF16)
            yield

    return init, chunk_group


def _mixer_kernel(pg_ref, ps_ref, pm_ref, g_nega, g_dtb, g_ng, s_nega, s_dtb, s_dsk, s_ng, go_ref, so_ref,
                  s_ref, gq_scr, b_scr, o0_scr, el_scr, h_ref, *, tt):
    g_init, g_prepare, g_scan = _gdn_program(pg_ref, pm_ref, g_nega, g_dtb, g_ng, go_ref, s_ref,
                                             gq_scr, b_scr, o0_scr, el_scr)
    s_init, s_chunks = _ssd_program(ps_ref, pm_ref, s_nega, s_dtb, s_dsk, s_ng, so_ref, h_ref)

    @pl.when(pl.program_id(1) == 0)
    def _():
        g_init()
        s_init()

    n_groups = tt // CHUNK // MIX_CHUNKS
    for ci in range(n_groups):
        stages = [g_prepare(ci), s_chunks(ci)]
        if ci > 0:
            stages.append(g_scan(ci - 1))
        _interleave(*stages)
    _interleave(g_scan(n_groups - 1))


def _mixer(pg, ps, pm, gdn_a_log, gdn_dt_bias, gdn_norm_g, ssd_a_log, ssd_dt_bias, ssd_d, ssd_norm_g, bsz, seq):
    tt = min(TT_SCAN, seq)
    nt = seq // tt
    units = tt // CHUNK * GDN_HEADS

    def lanes(v, lane0):
        return jnp.zeros((1, LANES), F32).at[0, lane0:lane0 + v.shape[0]].set(v)

    def tile(width):
        return pl.BlockSpec((tt, width), lambda b, t: (b * nt + t, 0))

    def const(width):
        return pl.BlockSpec((1, width), lambda b, t: (0, 0))

    return pl.pallas_call(
        functools.partial(_mixer_kernel, tt=tt),
        grid=(bsz, nt),
        in_specs=[tile(PG_W), tile(PS_W), tile(LANES), const(LANES), const(LANES), const(GDN_DV),
                  const(LANES), const(LANES), const(LANES), const(SSD_INNER)],
        out_specs=[tile(GDN_V), tile(SSD_INNER)],
        out_shape=[jax.ShapeDtypeStruct((bsz * seq, GDN_V), BF16),
                   jax.ShapeDtypeStruct((bsz * seq, SSD_INNER), BF16)],
        scratch_shapes=[
            pltpu.VMEM((GDN_HEADS, GDN_DK, GDN_DV), F32),
            pltpu.VMEM((units, GDN_DK + CHUNK, GDN_DK), BF16),
            pltpu.VMEM((units, GDN_DK, GDN_DV), F32),
            pltpu.VMEM((units, CHUNK, GDN_DV), F32),
            pltpu.VMEM((tt // CHUNK * SUBLANES, LANES), F32),
            pltpu.VMEM((SSD_G, SSD_N, SSD_INNER // SSD_G), F32),
        ],
        compiler_params=pltpu.CompilerParams(
            dimension_semantics=("parallel", "arbitrary"), vmem_limit_bytes=VMEM_LIMIT),
        name="mixer_scan",
    )(pg, ps, pm, lanes(-LOG2_E * jnp.exp(gdn_a_log), LANE_ALPHA), lanes(gdn_dt_bias, LANE_ALPHA),
      gdn_norm_g.reshape(1, GDN_DV), lanes(-LOG2_E * jnp.exp(ssd_a_log), LANE_DT), lanes(ssd_dt_bias, LANE_DT),
      lanes(ssd_d, LANE_DT), ssd_norm_g.reshape(1, SSD_INNER))


RT_IDX = 0
RT_GATE = TOP_K
RT_POS = 2 * TOP_K


def _outproj_router_kernel(go_ref, so_ref, x_ref, mod_ref, g_ref, wo_ref, rw_ref, rb_ref,
                           xn_ref, h_ref, rt_ref, cnt_ref, run_ref, *, tm):
    i = pl.program_id(0)

    @pl.when(i == 0)
    def _():
        run_ref[...] = jnp.zeros_like(run_ref)

    sub = tm // ROUTER_SPLIT
    lane = lax.broadcasted_iota(jnp.int32, (sub, LANES), 1).astype(F32)
    trow = lax.broadcasted_iota(jnp.int32, (sub, sub), 0)
    tcol = lax.broadcasted_iota(jnp.int32, (sub, sub), 1)
    earlier = (trow > tcol).astype(BF16)
    onehots = [None] * ROUTER_SPLIT

    def part(j):
        rows = slice(j * sub, (j + 1) * sub)
        mix = _dot(go_ref[rows, :], wo_ref[0:GDN_V, :]) + _dot(so_ref[rows, :], wo_ref[GDN_V:, :])
        yield
        xn = x_ref[rows, :] + mod_ref[2:3, :] * mix
        xn_ref[rows, :] = xn
        h = _modulated_norm(xn, g_ref[...], mod_ref[3:4, :], mod_ref[4:5, :])
        h_ref[rows, :] = _pack_rows(h)
        yield
        h_hi = h.astype(BF16)
        h_lo = (h - h_hi.astype(F32)).astype(BF16)
        hw = _dot(h_hi, rw_ref[...])
        work = hw[:, 0:LANES] + hw[:, LANES:] + _dot(h_lo, rw_ref[:, 0:LANES]) + rb_ref[...]
        yield
        tops = []
        idxs = []
        for _ in range(TOP_K):
            m = jnp.max(work, axis=-1, keepdims=True)
            idx = jnp.min(jnp.where(work == m, lane, float(LANES)), axis=-1, keepdims=True)
            work = jnp.where(lane == idx, -jnp.inf, work)
            tops.append(m)
            idxs.append(idx)
            yield
        exps = [jnp.exp(m - tops[0]) for m in tops]
        denom = exps[0] + exps[1] + exps[2] + exps[3]
        onehot = jnp.zeros((sub, LANES), F32)
        for idx in idxs:
            onehot = onehot + (lane == idx).astype(F32)
        onehots[j] = onehot
        yield
        prior = run_ref[0:1, :]
        for jj in range(j):
            prior = prior + jnp.sum(onehots[jj], axis=0, keepdims=True)
        before = _dot(earlier, onehot.astype(BF16)) + prior
        yield
        rt = jnp.zeros((sub, LANES), F32)
        for k in range(TOP_K):
            pos = jnp.sum(jnp.where(lane == idxs[k], before, 0.0), axis=-1, keepdims=True)
            rt = jnp.where(lane == RT_IDX + k, idxs[k], rt)
            rt = jnp.where(lane == RT_GATE + k, exps[k] / denom, rt)
            rt = jnp.where(lane == RT_POS + k, pos, rt)
        rt_ref[rows, :] = rt

    _interleave(*[part(j) for j in range(ROUTER_SPLIT)])
    run = run_ref[0:1, :]
    for onehot in onehots:
        run = run + jnp.sum(onehot, axis=0, keepdims=True)
    run_ref[...] = jnp.broadcast_to(run, run_ref.shape)
    cnt_ref[...] = jnp.broadcast_to(run, cnt_ref.shape)


def _outproj_router(go, so, x2, mod_l, g, w_out, rw, rb, seq):
    n_tok = x2.shape[0]
    tm = min(TM_ROUTER, seq)
    return pl.pallas_call(
        functools.partial(_outproj_router_kernel, tm=tm),
        grid=(n_tok // tm,),
        in_specs=[
            pl.BlockSpec((tm, GDN_V), lambda i: (i, 0)),
            pl.BlockSpec((tm, SSD_INNER), lambda i: (i, 0)),
            pl.BlockSpec((tm, D_MODEL), lambda i: (i, 0)),
            pl.BlockSpec((None, SUBLANES, D_MODEL), lambda i: ((i * tm) // seq, 0, 0)),
            pl.BlockSpec((1, D_MODEL), lambda i: (0, 0)),
            pl.BlockSpec((GDN_V + SSD_INNER, D_MODEL), lambda i: (0, 0)),
            pl.BlockSpec((D_MODEL, 2 * LANES), lambda i: (0, 0)),
            pl.BlockSpec((1, LANES), lambda i: (0, 0)),
        ],
        out_specs=[
            pl.BlockSpec((tm, D_MODEL), lambda i: (i, 0)),
            pl.BlockSpec((tm, ROW_WORDS), lambda i: (i, 0)),
            pl.BlockSpec((tm, LANES), lambda i: (i, 0)),
            pl.BlockSpec((SUBLANES, LANES), lambda i: (0, 0)),
        ],
        out_shape=[
            jax.ShapeDtypeStruct((n_tok, D_MODEL), F32),
            jax.ShapeDtypeStruct((n_tok, ROW_WORDS), U32),
            jax.ShapeDtypeStruct((n_tok, LANES), F32),
            jax.ShapeDtypeStruct((SUBLANES, LANES), F32),
        ],
        scratch_shapes=[pltpu.VMEM((SUBLANES, LANES), F32)],
        compiler_params=pltpu.CompilerParams(
            dimension_semantics=("arbitrary",), vmem_limit_bytes=VMEM_LIMIT),
        name="outproj_router",
    )(go, so, x2, mod_l, g, w_out, rw, rb)


def _sc_workers():
    info = plsc.get_sparse_core_info()
    return info.num_cores, info.num_subcores


def _scatter_rows(src, dest_t, n_rows):
    n_tok, width = src.shape
    n_k = dest_t.shape[0]
    nc, ns = _sc_workers()
    per_w = n_tok // (nc * ns)
    win = min(SC_WINDOW, per_w)
    mesh = plsc.VectorSubcoreMesh(core_axis_name="c", subcore_axis_name="s")

    @functools.partial(
        pl.kernel, mesh=mesh,
        out_type=jax.ShapeDtypeStruct((n_rows, width), src.dtype),
        scratch_types=[pltpu.VMEM((win,), jnp.int32), pltpu.VMEM((win, width), src.dtype)],
    )
    def scatter_kernel(src_hbm, idx_hbm, out_hbm, idx_v, rows_v):
        wid = lax.axis_index("s") * nc + lax.axis_index("c")
        base = wid * per_w

        @pl.loop(0, per_w // win)
        def _(j):
            off = base + j * win
            pltpu.sync_copy(src_hbm.at[pl.ds(off, win)], rows_v)
            for k in range(n_k):
                pltpu.sync_copy(idx_hbm.at[k, pl.ds(off, win)], idx_v)
                pltpu.sync_copy(rows_v, out_hbm.at[idx_v])

    return scatter_kernel(src, dest_t)


def _gather_rows(table, idx):
    n_idx = idx.shape[0]
    width = table.shape[1]
    nc, ns = _sc_workers()
    per_w = n_idx // (nc * ns)
    win = min(SC_WINDOW, per_w)
    mesh = plsc.VectorSubcoreMesh(core_axis_name="c", subcore_axis_name="s")

    @functools.partial(
        pl.kernel, mesh=mesh,
        out_type=jax.ShapeDtypeStruct((n_idx, width), table.dtype),
        scratch_types=[pltpu.VMEM((per_w,), jnp.int32), pltpu.VMEM((win, width), table.dtype)],
    )
    def gather_kernel(table_hbm, idx_hbm, out_hbm, idx_v, rows_v):
        wid = lax.axis_index("s") * nc + lax.axis_index("c")
        base = wid * per_w
        pltpu.sync_copy(idx_hbm.at[pl.ds(base, per_w)], idx_v)

        @pl.loop(0, per_w // win)
        def _(j):
            pltpu.sync_copy(table_hbm.at[idx_v.at[pl.ds(j * win, win)]], rows_v)
            pltpu.sync_copy(rows_v, out_hbm.at[pl.ds(base + j * win, win)])

    return gather_kernel(table, idx)


def _expert_kernel(be_ref, bv_ref, x_ref, wgu_ref, bgu_ref, wd_ref, bd_ref, y_ref, wgu_b, wd_b, *, tr):
    i = pl.program_id(0)
    e = be_ref[i]
    prev = be_ref[jnp.maximum(i - 1, 0)]

    @pl.when((i == 0) | (e != prev))
    def _():
        wgu_b[...] = wgu_ref[...].astype(BF16)
        wd_b[...] = wd_ref[...].astype(BF16)

    valid = bv_ref[i]
    for r0 in range(0, tr, SUB_BLOCK):
        part = pl.ds(r0, SUB_BLOCK)

        @pl.when(valid <= r0)
        def _():
            y_ref[part, :] = jnp.zeros((SUB_BLOCK, y_ref.shape[1]), y_ref.dtype)

        @pl.when(valid > r0)
        def _():
            rows = lax.broadcasted_iota(jnp.int32, (SUB_BLOCK, D_MODEL), 0) + r0
            x = jnp.where(rows < valid, _unpack_rows(x_ref[part, :]), 0.0).astype(BF16)
            acc = jnp.zeros((SUB_BLOCK, D_MODEL), F32)
            for f in range(0, D_FF, FF_CHUNK):
                gate = _dot(x, wgu_b[:, f:f + FF_CHUNK]) + bgu_ref[:, f:f + FF_CHUNK]
                up = _dot(x, wgu_b[:, D_FF + f:D_FF + f + FF_CHUNK]) + bgu_ref[:, D_FF + f:D_FF + f + FF_CHUNK]
                gate = jnp.minimum(gate, SWIGLU_LIMIT)
                up = jnp.clip(up, -SWIGLU_LIMIT, SWIGLU_LIMIT)
                act = (up + 1.0) * (gate * _sigmoid(gate * SWIGLU_ALPHA))
                acc = acc + _dot(act.astype(BF16), wd_b[f:f + FF_CHUNK, :])
            y_ref[part, :] = _pack_rows(acc + bd_ref[...])


def _experts(x_rows, blk_expert, blk_valid, w_gu, b_gu, w_down, b_down, layer):
    n_rows, row_w = x_rows.shape
    depth = w_gu.shape[0]
    tr = ROW_BLOCK
    nb = n_rows // tr
    grid_spec = pltpu.PrefetchScalarGridSpec(
        num_scalar_prefetch=2,
        grid=(nb,),
        in_specs=[
            pl.BlockSpec((tr, row_w), lambda i, be, bv: (i, 0)),
            pl.BlockSpec((None, None, D_MODEL, 2 * D_FF), lambda i, be, bv: (layer, be[i], 0, 0)),
            pl.BlockSpec((None, None, 1, 2 * D_FF), lambda i, be, bv: (layer, be[i], 0, 0)),
            pl.BlockSpec((None, None, D_FF, D_MODEL), lambda i, be, bv: (layer, be[i], 0, 0)),
            pl.BlockSpec((None, None, 1, D_MODEL), lambda i, be, bv: (layer, be[i], 0, 0)),
        ],
        out_specs=pl.BlockSpec((tr, row_w), lambda i, be, bv: (i, 0)),
        scratch_shapes=[pltpu.VMEM((D_MODEL, 2 * D_FF), BF16), pltpu.VMEM((D_FF, D_MODEL), BF16)],
    )
    return pl.pallas_call(
        functools.partial(_expert_kernel, tr=tr),
        grid_spec=grid_spec,
        out_shape=jax.ShapeDtypeStruct((n_rows, row_w), x_rows.dtype),
        compiler_params=pltpu.CompilerParams(
            dimension_semantics=("arbitrary",), vmem_limit_bytes=VMEM_LIMIT),
        name="moe_experts",
    )(blk_expert, blk_valid, x_rows, w_gu, b_gu.reshape(depth, N_EXPERTS, 1, 2 * D_FF), w_down,
      b_down.reshape(depth, N_EXPERTS, 1, D_MODEL))


def _combine_kernel(xn_ref, y4_ref, rt_ref, mod_ref, g_ref, o_ref, *, final):
    rt = rt_ref[...]
    acc = rt[:, RT_GATE:RT_GATE + 1] * _unpack_rows(y4_ref[0])
    for k in range(1, TOP_K):
        acc = acc + rt[:, RT_GATE + k:RT_GATE + k + 1] * _unpack_rows(y4_ref[k])
    x = xn_ref[...] + mod_ref[5:6, :] * acc
    if final:
        ms = jnp.mean(x * x, axis=-1, keepdims=True)
        x = x * lax.rsqrt(ms + EPS) * g_ref[...]
    o_ref[...] = x


def _combine(xn, y4, rt, mod_l, final_g, seq, final):
    n_tok = xn.shape[0]
    row_w = y4.shape[2]
    tm = min(TM_PROJ, seq)
    return pl.pallas_call(
        functools.partial(_combine_kernel, final=final),
        grid=(n_tok // tm,),
        in_specs=[
            pl.BlockSpec((tm, D_MODEL), lambda i: (i, 0)),
            pl.BlockSpec((TOP_K, tm, row_w), lambda i: (0, i, 0)),
            pl.BlockSpec((tm, LANES), lambda i: (i, 0)),
            pl.BlockSpec((None, SUBLANES, D_MODEL), lambda i: ((i * tm) // seq, 0, 0)),
            pl.BlockSpec((1, D_MODEL), lambda i: (0, 0)),
        ],
        out_specs=pl.BlockSpec((tm, D_MODEL), lambda i: (i, 0)),
        out_shape=jax.ShapeDtypeStruct((n_tok, D_MODEL), F32),
        compiler_params=pltpu.CompilerParams(
            dimension_semantics=("parallel",), vmem_limit_bytes=VMEM_LIMIT),
        name="moe_combine",
    )(xn, y4, rt, mod_l, final_g)


def _permute_w_in(w_in):
    off = 0
    gdn_qkvz = w_in[:, off:off + PG_W]; off += PG_W
    small_ba = w_in[:, off:off + 2 * GDN_HEADS]; off += 2 * GDN_HEADS
    ssd_zxbc = w_in[:, off:off + PS_W]; off += PS_W
    ssd_dt = w_in[:, off:off + SSD_HEADS]
    pad = jnp.zeros((w_in.shape[0], LANES - 2 * GDN_HEADS - SSD_HEADS), w_in.dtype)
    return jnp.concatenate([gdn_qkvz, ssd_zxbc, small_ba, ssd_dt, pad], axis=1).astype(BF16)


def _moe_schedule(rt, counts_row, n_rows):
    idx = rt[:, RT_IDX:RT_IDX + TOP_K].astype(jnp.int32)
    pos = rt[:, RT_POS:RT_POS + TOP_K].astype(jnp.int32)
    counts = counts_row[0, :N_EXPERTS].astype(jnp.int32)
    padded = (counts + ROW_BLOCK - 1) // ROW_BLOCK * ROW_BLOCK
    pad_end = jnp.cumsum(padded)
    pad_start = pad_end - padded
    experts = jnp.arange(N_EXPERTS, dtype=jnp.int32)
    dest = pos + jnp.sum(jnp.where(idx[..., None] == experts, pad_start, 0), axis=-1)
    blk_start = jnp.arange(n_rows // ROW_BLOCK, dtype=jnp.int32) * ROW_BLOCK
    blk_expert = jnp.minimum(jnp.sum(blk_start[:, None] >= pad_end[None, :], axis=1), N_EXPERTS - 1)
    blk_onehot = blk_expert[:, None] == experts
    blk_valid = (jnp.sum(jnp.where(blk_onehot, counts + pad_start, 0), axis=1) - blk_start)
    blk_valid = jnp.where(blk_start < pad_end[-1], jnp.clip(blk_valid, 0, ROW_BLOCK), 0)
    return dest, blk_expert.astype(jnp.int32), blk_valid.astype(jnp.int32)


def _forward(x, c, ada_w, ada_b, norm1_g, norm2_g, w_in, gdn_conv_w, gdn_a_log, gdn_dt_bias, gdn_norm_g,
             ssd_conv_w, ssd_conv_b, ssd_a_log, ssd_dt_bias, ssd_d, ssd_norm_g, w_out, router_w, router_b,
             moe_w_gu, moe_b_gu, moe_w_down, moe_b_down, final_g):
    bsz, seq, _ = x.shape
    depth = ada_w.shape[0]
    n_tok = bsz * seq
    n_rows = n_tok * TOP_K + N_EXPERTS * ROW_BLOCK
    mod = _ada_mod(c, ada_w, ada_b)
    x2 = x.reshape(n_tok, D_MODEL)
    fg = final_g.reshape(1, D_MODEL)

    def prepare_weights(l, anchor):
        conv_w = jnp.concatenate([gdn_conv_w[l], ssd_conv_w[l]], axis=1) + anchor
        conv_b = jnp.concatenate([jnp.zeros((1, GDN_CONV_CH), F32), ssd_conv_b[l].reshape(1, -1)], axis=1)
        rw = jnp.zeros((D_MODEL, LANES), F32).at[:, :N_EXPERTS].set(router_w[l]) + anchor
        rw_hi = rw.astype(BF16)
        rw = jnp.concatenate([rw_hi, (rw - rw_hi.astype(F32)).astype(BF16)], axis=1)
        rb = jnp.full((1, LANES), -1e30, F32).at[0, :N_EXPERTS].set(router_b[l])
        return (_permute_w_in(w_in[l] + anchor), conv_w, conv_b, rw, rb, (w_out[l] + anchor).astype(BF16))

    prepared = prepare_weights(0, jnp.zeros((), F32))
    for l in range(depth):
        w_perm, conv_w, conv_b, rw, rb, w_out_b = prepared
        pg, ps, pm = _inproj(x2, mod[l], norm1_g[l].reshape(1, D_MODEL), w_perm, conv_w, conv_b, seq)
        go, so = _mixer(pg, ps, pm, gdn_a_log[l], gdn_dt_bias[l], gdn_norm_g[l],
                        ssd_a_log[l], ssd_dt_bias[l], ssd_d[l], ssd_norm_g[l], bsz, seq)
        xn, h2, rt, cnt = _outproj_router(go, so, x2, mod[l], norm2_g[l].reshape(1, D_MODEL),
                                          w_out_b, rw, rb, seq)
        dest, blk_expert, blk_valid = _moe_schedule(rt, cnt, n_rows)
        dest_t = dest.T
        if l + 1 < depth:
            dest_t, anchor = lax.optimization_barrier((dest_t, jnp.zeros((), F32)))
            prepared = prepare_weights(l + 1, anchor)
        x_rows = _scatter_rows(h2, dest_t, n_rows)
        y_rows = _experts(x_rows, blk_expert, blk_valid, moe_w_gu, moe_b_gu, moe_w_down, moe_b_down, l)
        y4 = _gather_rows(y_rows, dest_t.reshape(TOP_K * n_tok)).reshape(TOP_K, n_tok, y_rows.shape[1])
        x2 = _combine(xn, y4, rt, mod[l], fg, seq, final=(l == depth - 1))
    return x2.reshape(bsz, seq, D_MODEL)


def kernel(x, c, ada_w, ada_b, norm1_g, norm2_g, w_in, gdn_conv_w, gdn_a_log, gdn_dt_bias, gdn_norm_g, ssd_conv_w, ssd_conv_b, ssd_a_log, ssd_dt_bias, ssd_d, ssd_norm_g, w_out, router_w, router_b, moe_w_gu, moe_b_gu, moe_w_down, moe_b_down, final_g):
    return _forward(x, c, ada_w, ada_b, norm1_g, norm2_g, w_in, gdn_conv_w, gdn_a_log, gdn_dt_bias, gdn_norm_g,
                    ssd_conv_w, ssd_conv_b, ssd_a_log, ssd_dt_bias, ssd_d, ssd_norm_g, w_out, router_w, router_b,
                    moe_w_gu, moe_b_gu, moe_w_down, moe_b_down, final_g)
```

```python
import functools

import jax
import jax.numpy as jnp
from jax import lax
from jax.experimental import pallas as pl
from jax.experimental.pallas import tpu as pltpu
from jax.experimental.pallas import tpu_sc as plsc

F32 = jnp.float32
BF16 = jnp.bfloat16
U32 = jnp.uint32
HIGHEST = lax.Precision.HIGHEST

D_MODEL = 1024
CHUNK = 64
CONV_K = 4
GDN_HEADS = 8
GDN_DK = 128
GDN_DV = 128
GDN_QK = GDN_HEADS * GDN_DK
GDN_V = GDN_HEADS * GDN_DV
GDN_CONV_CH = 2 * GDN_QK + GDN_V
SSD_P = 64
SSD_HEADS = 16
SSD_G = 2
SSD_N = 128
SSD_INNER = SSD_HEADS * SSD_P
SSD_BC = SSD_G * SSD_N
SSD_CONV_CH = SSD_INNER + 2 * SSD_BC
N_EXPERTS = 32
TOP_K = 4
D_FF = D_MODEL
SWIGLU_ALPHA = 1.702
SWIGLU_LIMIT = 7.0
EPS = 1e-6
LOG2_E = 1.4426950408889634

LANES = 128
SUBLANES = 8
PG_W = GDN_CONV_CH + GDN_V
PS_W = SSD_INNER + SSD_CONV_CH
PS_X = SSD_INNER
PS_B = 2 * SSD_INNER
PS_C = 2 * SSD_INNER + SSD_BC
N_CONV = GDN_CONV_CH + SSD_CONV_CH
LANE_BETA = 0
LANE_ALPHA = GDN_HEADS
LANE_DT = 2 * GDN_HEADS
VMEM_LIMIT = 56 * 1024 * 1024

TM_PROJ = 512
TM_ROUTER = 1024
ROUTER_SPLIT = 4
TT_SCAN = 512
MIX_CHUNKS = 4
SSD_STAGE_SPLIT = 4
ROW_BLOCK = 1024
SUB_BLOCK = 512
FF_CHUNK = 512
CONV_SLAB = 512
SC_WINDOW = 128
ROW_WORDS = D_MODEL // 2


def _dot(a, b):
    return jnp.dot(a, b, preferred_element_type=F32)


def _dot_nt(a, b):
    return lax.dot_general(a, b, (((1,), (1,)), ((), ())), preferred_element_type=F32)


def _dot_tn(a, b):
    return lax.dot_general(a, b, (((0,), (0,)), ((), ())), preferred_element_type=F32)


def _sigmoid(x):
    return 1.0 / (1.0 + jnp.exp(-x))


def _silu(x):
    h = 0.5 * x
    return h + h * jnp.tanh(h)


def _softplus(x):
    return jnp.maximum(x, 0.0) + jnp.log(1.0 + jnp.exp(-jnp.abs(x)))


def _pack_rows(h):
    w = h.shape[1] // 2
    hi = pltpu.bitcast(h[:, :w].astype(BF16).astype(F32), U32)
    lo = pltpu.bitcast(h[:, w:].astype(BF16).astype(F32), U32)
    return hi | (lo >> 16)


def _unpack_rows(p):
    hi = pltpu.bitcast(p & jnp.uint32(0xFFFF0000), F32)
    lo = pltpu.bitcast(p << 16, F32)
    return jnp.concatenate([hi, lo], axis=1)


def _ada_kernel(c_ref, w_ref, b_ref, o_ref):
    c = c_ref[...]
    o_ref[0] = jnp.dot(_silu(c), w_ref[0], precision=HIGHEST, preferred_element_type=F32) + b_ref[0]


def _ada_mod(c, ada_w, ada_b):
    depth = ada_w.shape[0]
    bsz = c.shape[0]
    c8 = jnp.zeros((SUBLANES, D_MODEL), F32).at[:bsz].set(c)
    out = pl.pallas_call(
        _ada_kernel,
        grid=(depth, 6),
        in_specs=[
            pl.BlockSpec((SUBLANES, D_MODEL), lambda l, j: (0, 0)),
            pl.BlockSpec((1, D_MODEL, D_MODEL), lambda l, j: (l, 0, j)),
            pl.BlockSpec((1, 1, D_MODEL), lambda l, j: (l, 0, j)),
        ],
        out_specs=pl.BlockSpec((1, SUBLANES, D_MODEL), lambda l, j: (l, 0, j)),
        out_shape=jax.ShapeDtypeStruct((depth, SUBLANES, 6 * D_MODEL), F32),
        name="ada_mod",
    )(c8, ada_w, ada_b.reshape(depth, 1, 6 * D_MODEL))
    mod = out[:, :bsz].reshape(depth, bsz, 6, D_MODEL)
    return jnp.concatenate([mod, jnp.zeros((depth, bsz, 2, D_MODEL), F32)], axis=2)


def _modulated_norm(x, g, shift, scale):
    ms = jnp.mean(x * x, axis=-1, keepdims=True)
    return (x * lax.rsqrt(ms + EPS) * g) * (1.0 + scale) + shift


def _inproj_kernel(x_ref, mod_ref, g_ref, w_ref, cw_ref, cb_ref, og_ref, os_ref, om_ref,
                   tail_ref, cbuf_ref, *, tm, seq):
    i = pl.program_id(0)

    @pl.when((i * tm) % seq == 0)
    def _():
        tail_ref[...] = jnp.zeros_like(tail_ref)

    h = _modulated_norm(x_ref[...], g_ref[...], mod_ref[0:1, :], mod_ref[1:2, :]).astype(BF16)

    def conv_slab(n, wcol, ccol, bias):
        cs = slice(ccol, ccol + CONV_SLAB)
        pbuf = cbuf_ref.at[n % 2, 0]
        bbuf = cbuf_ref.at[n % 2, 1]
        p = _dot(h, w_ref[:, wcol:wcol + CONV_SLAB])
        pbuf[0:SUBLANES, :] = tail_ref[0, :, cs]
        pbuf[SUBLANES:SUBLANES + tm, :] = p
        tail_ref[0, :, cs] = p[tm - SUBLANES:tm, :]
        p1 = pbuf[SUBLANES - 1:SUBLANES - 1 + tm, :]
        b = p * cw_ref[1:2, cs] + p1 * cw_ref[0:1, cs]
        bbuf[0:SUBLANES, :] = tail_ref[1, :, cs]
        bbuf[SUBLANES:SUBLANES + tm, :] = b
        tail_ref[1, :, cs] = b[tm - SUBLANES:tm, :]
        y = p * cw_ref[3:4, cs] + p1 * cw_ref[2:3, cs] + bbuf[SUBLANES - 2:SUBLANES - 2 + tm, :]
        if bias:
            y = y + cb_ref[:, cs]
        return _silu(y)

    n = 0
    for part, scale in ((0, GDN_DK ** -0.5), (1, 1.0)):
        for j in range(GDN_QK // CONV_SLAB):
            col = part * GDN_QK + j * CONV_SLAB
            y = conv_slab(n, col, col, False)
            n += 1
            for hh in range(CONV_SLAB // GDN_DK):
                yh = y[:, hh * GDN_DK:(hh + 1) * GDN_DK]
                inv = lax.rsqrt(jnp.sum(yh * yh, axis=-1, keepdims=True) + EPS) * scale
                og_ref[:, col + hh * GDN_DK:col + (hh + 1) * GDN_DK] = (yh * inv).astype(BF16)
    for j in range(GDN_V // CONV_SLAB):
        col = 2 * GDN_QK + j * CONV_SLAB
        og_ref[:, col:col + CONV_SLAB] = conv_slab(n, col, col, False).astype(BF16)
        n += 1
    og_ref[:, GDN_CONV_CH:PG_W] = _dot(h, w_ref[:, GDN_CONV_CH:PG_W]).astype(BF16)
    os_ref[:, 0:SSD_INNER] = _dot(h, w_ref[:, PG_W:PG_W + SSD_INNER]).astype(BF16)
    for j in range(SSD_CONV_CH // CONV_SLAB):
        col = j * CONV_SLAB
        os_ref[:, SSD_INNER + col:SSD_INNER + col + CONV_SLAB] = conv_slab(
            n, PG_W + SSD_INNER + col, GDN_CONV_CH + col, True).astype(BF16)
        n += 1
    om_ref[...] = _dot(h, w_ref[:, PG_W + PS_W:])


def _inproj(x2, mod_l, g, w_perm, conv_w, conv_b, seq):
    n_tok = x2.shape[0]
    tm = min(TM_PROJ, seq)
    wtot = w_perm.shape[1]
    return pl.pallas_call(
        functools.partial(_inproj_kernel, tm=tm, seq=seq),
        grid=(n_tok // tm,),
        in_specs=[
            pl.BlockSpec((tm, D_MODEL), lambda i: (i, 0)),
            pl.BlockSpec((None, SUBLANES, D_MODEL), lambda i: ((i * tm) // seq, 0, 0)),
            pl.BlockSpec((1, D_MODEL), lambda i: (0, 0)),
            pl.BlockSpec((D_MODEL, wtot), lambda i: (0, 0)),
            pl.BlockSpec((CONV_K, N_CONV), lambda i: (0, 0)),
            pl.BlockSpec((1, N_CONV), lambda i: (0, 0)),
        ],
        out_specs=[
            pl.BlockSpec((tm, PG_W), lambda i: (i, 0)),
            pl.BlockSpec((tm, PS_W), lambda i: (i, 0)),
            pl.BlockSpec((tm, LANES), lambda i: (i, 0)),
        ],
        out_shape=[
            jax.ShapeDtypeStruct((n_tok, PG_W), BF16),
            jax.ShapeDtypeStruct((n_tok, PS_W), BF16),
            jax.ShapeDtypeStruct((n_tok, LANES), F32),
        ],
        scratch_shapes=[
            pltpu.VMEM((2, SUBLANES, N_CONV), F32),
            pltpu.VMEM((2, 2, SUBLANES + tm, CONV_SLAB), F32),
        ],
        compiler_params=pltpu.CompilerParams(
            dimension_semantics=("arbitrary",), vmem_limit_bytes=VMEM_LIMIT),
        name="norm_inproj",
    )(x2, mod_l, g, w_perm, conv_w, conv_b)


def _tri_masks():
    row = lax.broadcasted_iota(jnp.int32, (CHUNK, CHUNK), 0)
    col = lax.broadcasted_iota(jnp.int32, (CHUNK, CHUNK), 1)
    return row, col


def _block_diag2(m, lo_half):
    zero = jnp.zeros_like(m)
    return jnp.concatenate([jnp.where(lo_half, m, zero), jnp.where(lo_half, zero, m)], axis=0)


def _unit_lower_inverses(nmats, row, col, lo_half):
    eye = (row == col).astype(F32)
    pair = (row >> 1) == (col >> 1)
    xs = [eye - jnp.where(pair, n, 0.0) for n in nmats]
    for lg in range(2, CHUNK.bit_length()):
        mask = ((row >> lg) == (col >> lg)) & ((row >> (lg - 1)) != (col >> (lg - 1)))
        xb = [x.astype(BF16) for x in xs]
        ys = [_dot(jnp.where(mask, n, 0.0).astype(BF16), _block_diag2(b, lo_half)) for n, b in zip(nmats, xb)]
        yield
        xs = [x - _dot(b, _block_diag2(y.astype(BF16), lo_half)) for x, b, y in zip(xs, xb, ys)]
        yield
    return xs


def _interleave(*stage_generators):
    live = list(stage_generators)
    while live:
        for gen in list(live):
            try:
                next(gen)
            except StopIteration:
                live.remove(gen)


def _gdn_program(pg_ref, pm_ref, nega_ref, dtb_ref, ng_ref, o_ref, s_ref, gq_scr, b_scr, o0_scr, el_scr):
    heads = range(GDN_HEADS)

    def init():
        s_ref[...] = jnp.zeros_like(s_ref)

    row, col = _tri_masks()
    tril = (row >= col).astype(F32)
    row2 = lax.broadcasted_iota(jnp.int32, (CHUNK, LANES), 0)
    lane2 = lax.broadcasted_iota(jnp.int32, (CHUNK, LANES), 1)
    col2 = lane2 & (CHUNK - 1)
    lo_half = lane2 < CHUNK
    lo_half1 = lax.broadcasted_iota(jnp.int32, (1, LANES), 1) < CHUNK
    incl2 = row2 >= col2
    strict2 = row2 > col2
    zero_k = jnp.zeros((CHUNK, GDN_DK), BF16)
    zero_r = jnp.zeros((CHUNK, 2 * GDN_DV), BF16)
    nega = nega_ref[...]
    dtb = dtb_ref[...]
    ng = ng_ref[...]

    def wide(arr, l0):
        n = arr.shape[0]
        return jnp.concatenate([jnp.broadcast_to(arr[:, l0:l0 + 1], (n, LANES)),
                                jnp.broadcast_to(arr[:, l0 + 1:l0 + 2], (n, LANES))], axis=1)

    def halves(w, sel):
        return jnp.where(sel, w[:, 0:LANES], w[:, LANES:])

    def prepare(ci):
        units = []
        for j in range(MIX_CHUNKS):
            c = ci * MIX_CHUNKS + j
            rows = pl.ds(c * CHUNK, CHUNK)
            pmv = pm_ref[rows, :]
            beta_all = _sigmoid(pmv)
            g_all = nega * _softplus(pmv + dtb)
            gam = jnp.dot(tril, g_all, precision=HIGHEST, preferred_element_type=F32)
            gam_t = gam.T
            gam_last = gam[CHUNK - 1:CHUNK, :]
            el_scr[pl.ds(c * SUBLANES, SUBLANES), :] = jnp.broadcast_to(
                jnp.exp2(gam_last), (SUBLANES, LANES))
            for h0 in range(0, GDN_HEADS, 2):
                la = LANE_ALPHA + h0
                gam_w = wide(gam, la)
                r0_t = gam_t[la:la + 1, :]
                r1_t = gam_t[la + 1:la + 2, :]
                units.append(dict(
                    slot=c * GDN_HEADS + h0, rows=rows, h0=h0,
                    beta_w=wide(beta_all, LANE_BETA + h0), gam_w=gam_w, gl_w=wide(gam_last, la),
                    grow=jnp.where(lo_half1, jnp.concatenate([r0_t, r0_t], axis=1),
                                   jnp.concatenate([r1_t, r1_t], axis=1))))

        def cols(u, base):
            return pg_ref[u["rows"], base + u["h0"] * GDN_DK:base + (u["h0"] + 2) * GDN_DK]

        yield
        q2s = [cols(u, 0) for u in units]
        k2s = [cols(u, GDN_QK) for u in units]
        kbd = [jnp.concatenate([jnp.concatenate([k2[:, 0:GDN_DK], zero_k], axis=1),
                                jnp.concatenate([zero_k, k2[:, GDN_DK:]], axis=1)], axis=0) for k2 in k2s]
        qkks = [_dot_nt(jnp.concatenate([q2, k2], axis=0), r) for q2, k2, r in zip(q2s, k2s, kbd)]
        yield
        decays = [jnp.exp2(jnp.where(incl2, halves(u["gam_w"], lo_half) - u["grow"], -jnp.inf)) for u in units]
        nmats = [jnp.where(strict2, halves(u["beta_w"], lo_half) * qkk[CHUNK:2 * CHUNK, :] * d, 0.0)
                 for u, qkk, d in zip(units, qkks, decays)]
        amats = [(qkk[0:CHUNK, :] * d).astype(BF16) for qkk, d in zip(qkks, decays)]
        yield
        egws = [jnp.exp2(u["gam_w"]) for u in units]
        k2f = [k2.astype(F32) for k2 in k2s]
        kds = [(kf * jnp.exp2(u["gl_w"] - u["gam_w"])).astype(BF16) for kf, u in zip(k2f, units)]
        yield
        tinvs = yield from _unit_lower_inverses(nmats, row2, col2, lo_half)
        vbs = [(cols(u, 2 * GDN_QK).astype(F32) * u["beta_w"]).astype(BF16) for u in units]
        kbs = [(kf * (u["beta_w"] * e)).astype(BF16) for kf, u, e in zip(k2f, units, egws)]
        rhss = [jnp.concatenate(
            [jnp.concatenate([vb[:, 0:GDN_DV], kb[:, 0:GDN_DK], zero_r], axis=1),
             jnp.concatenate([zero_r, vb[:, GDN_DV:], kb[:, GDN_DK:]], axis=1)], axis=0)
            for vb, kb in zip(vbs, kbs)]
        yield
        sols = [_dot(x.astype(BF16), r).astype(BF16) for x, r in zip(tinvs, rhss)]
        yield
        sbd = [jnp.concatenate([jnp.concatenate([s[:, 0:2 * GDN_DV], zero_r], axis=1),
                                jnp.concatenate([zero_r, s[:, 2 * GDN_DV:]], axis=1)], axis=0) for s in sols]
        a_uw = [_dot(a, r) for a, r in zip(amats, sbd)]
        yield
        for hh in range(2):
            k_uw = [_dot_tn(kd[:, hh * GDN_DK:(hh + 1) * GDN_DK], s[:, hh * 2 * GDN_DV:(hh + 1) * 2 * GDN_DV])
                    for kd, s in zip(kds, sols)]
            yield
            for u, q2, e, au, ku in zip(units, q2s, egws, a_uw, k_uw):
                qe = (q2[:, hh * GDN_DK:(hh + 1) * GDN_DK].astype(F32) * e[:, hh * LANES:(hh + 1) * LANES])
                base = hh * 2 * GDN_DV
                o0_scr[u["slot"] + hh] = au[:, base:base + GDN_DV]
                b_scr[u["slot"] + hh] = ku[:, 0:GDN_DV]
                gq_scr[u["slot"] + hh] = jnp.concatenate(
                    [ku[:, GDN_DV:], qe - au[:, base + GDN_DV:base + 2 * GDN_DV]], axis=0).astype(BF16)
            yield

    def scan(ci):
        for c in range(ci * MIX_CHUNKS, (ci + 1) * MIX_CHUNKS):
            rows = pl.ds(c * CHUNK, CHUNK)
            e_last = el_scr[pl.ds(c * SUBLANES, SUBLANES), :][0:1, :]
            states = [s_ref[h] for h in heads]
            rs = [_dot(gq_scr[c * GDN_HEADS + h], states[h].astype(BF16)) for h in heads]
            yield
            for h in heads:
                s_ref[h] = (e_last[:, LANE_ALPHA + h:LANE_ALPHA + h + 1] * states[h]
                            - rs[h][0:GDN_DK, :] + b_scr[c * GDN_HEADS + h])
            yield
            for h in heads:
                o = rs[h][GDN_DK:, :] + o0_scr[c * GDN_HEADS + h]
                z = pg_ref[rows, GDN_CONV_CH + h * GDN_DV:GDN_CONV_CH + (h + 1) * GDN_DV].astype(F32)
                on = o * lax.rsqrt(jnp.mean(o * o, axis=-1, keepdims=True) + EPS) * ng
                o_ref[rows, h * GDN_DV:(h + 1) * GDN_DV] = (on * _silu(z)).astype(BF16)
                if h % 4 == 3:
                    yield

    return init, prepare, scan


def _ssd_program(ps_ref, pm_ref, nega_ref, dtb_ref, dsk_ref, ng_ref, o_ref, h_ref):
    def init():
        h_ref[...] = jnp.zeros_like(h_ref)

    row, col = _tri_masks()
    incl = row >= col
    tril = incl.astype(F32)
    nega = nega_ref[...]
    dtb = dtb_ref[...]
    dsk = dsk_ref[...]
    lane = lax.broadcasted_iota(jnp.int32, (CHUNK, LANES), 1)
    lo_half = lane < SSD_P
    incl2 = lax.broadcasted_iota(jnp.int32, (CHUNK, LANES), 0) >= (lane & (SSD_P - 1))
    lane1 = lax.broadcasted_iota(jnp.int32, (1, LANES), 1)
    lo_half1 = lane1 < SSD_P
    heads_per_group = SSD_HEADS // SSD_G
    gw = SSD_INNER // SSD_G

    def pair_cols(arr, l0):
        sel = lo_half if arr.shape[0] == CHUNK else lo_half1
        return jnp.where(sel, arr[:, l0:l0 + 1], arr[:, l0 + 1:l0 + 2])

    pairs_per_group = heads_per_group // 2

    def chunk_group(ci):
        groups = []
        for j in range(MIX_CHUNKS):
            c = ci * MIX_CHUNKS + j
            rows = pl.ds(c * CHUNK, CHUNK)
            pmv = pm_ref[rows, :]
            dt_all = _softplus(pmv + dtb)
            acum = jnp.dot(tril, nega * dt_all, precision=HIGHEST, preferred_element_type=F32)
            info = dict(rows=rows, dt=dt_all, acum=acum, acum_t=acum.T, a_last=acum[CHUNK - 1:CHUNK, :])
            for g in range(SSD_G):
                groups.append(dict(info, g=g))
        yield
        for gr in groups:
            g = gr["g"]
            gr["bg"] = ps_ref[gr["rows"], PS_B + g * SSD_N:PS_B + (g + 1) * SSD_N]
            gr["cg"] = ps_ref[gr["rows"], PS_C + g * SSD_N:PS_C + (g + 1) * SSD_N]
        for gr in groups:
            gr["cb2"] = _dot_nt(gr["cg"], jnp.concatenate([gr["bg"], gr["bg"]], axis=0))
        yield

        units = [dict(gr=gr, p=p, l0=LANE_DT + gr["g"] * heads_per_group + 2 * p)
                 for gr in groups for p in range(pairs_per_group)]
        batches = [units[i:i + len(units) // SSD_STAGE_SPLIT] for i in range(0, len(units), len(units) // SSD_STAGE_SPLIT)]
        for batch in batches:
            for u in batch:
                gr, l0 = u["gr"], u["l0"]
                head0 = l0 - LANE_DT
                u["x"] = ps_ref[gr["rows"], PS_X + head0 * SSD_P:PS_X + (head0 + 2) * SSD_P].astype(F32)
                u["ac"] = pair_cols(gr["acum"], l0)
                u["al"] = pair_cols(gr["a_last"], l0)
                r0_t = gr["acum_t"][l0:l0 + 1, :]
                r1_t = gr["acum_t"][l0 + 1:l0 + 2, :]
                u["arow"] = jnp.where(lo_half1, jnp.concatenate([r0_t, r0_t], axis=1),
                                      jnp.concatenate([r1_t, r1_t], axis=1))
                u["xdt"] = u["x"] * pair_cols(gr["dt"], l0)
            yield
        for batch in batches:
            for u in batch:
                decay = jnp.exp2(jnp.where(incl2, u["ac"] - u["arow"], -jnp.inf))
                u["m"] = (u["gr"]["cb2"] * decay).astype(BF16)
                u["rhs"] = _block_diag2(u["xdt"], lo_half).astype(BF16)
            yield
        for batch in batches:
            for u in batch:
                u["y_diag"] = _dot(u["m"], u["rhs"])
                u["xw"] = (u["xdt"] * jnp.exp2(u["al"] - u["ac"])).astype(BF16)
            yield
        for gr in groups:
            mine = [u for u in units if u["gr"] is gr]
            gr["upd"] = _dot_tn(gr["bg"], jnp.concatenate([u["xw"] for u in mine], axis=1))
            gr["scale"] = jnp.concatenate([jnp.exp2(u["al"]) for u in mine], axis=1)
        yield

        states = [h_ref[g] for g in range(SSD_G)]
        for gr in groups:
            gr["y_off"] = _dot(gr["cg"], states[gr["g"]].astype(BF16))
            states[gr["g"]] = gr["scale"] * states[gr["g"]] + gr["upd"]
        for g in range(SSD_G):
            h_ref[g] = states[g]
        yield

        for gr in groups:
            g = gr["g"]
            mine = [u for u in units if u["gr"] is gr]
            y = jnp.concatenate(
                [u["y_diag"] + jnp.exp2(u["ac"]) * gr["y_off"][:, u["p"] * LANES:(u["p"] + 1) * LANES]
                 + pair_cols(dsk, u["l0"]) * u["x"] for u in mine], axis=1)
            z = ps_ref[gr["rows"], g * gw:(g + 1) * gw].astype(F32)
            yz = y * _silu(z)
            yn = yz * lax.rsqrt(jnp.mean(yz * yz, axis=-1, keepdims=True) + EPS)
            o_ref[gr["rows"], g * gw:(g + 1) * gw] = (yn * ng_ref[:, g * gw:(g + 1) * gw]).astype(BF16)
            yield

    return init, chunk_group


def _mixer_kernel(pg_ref, ps_ref, pm_ref, g_nega, g_dtb, g_ng, s_nega, s_dtb, s_dsk, s_ng, go_ref, so_ref,
                  s_ref, gq_scr, b_scr, o0_scr, el_scr, h_ref, *, tt):
    g_init, g_prepare, g_scan = _gdn_program(pg_ref, pm_ref, g_nega, g_dtb, g_ng, go_ref, s_ref,
                                             gq_scr, b_scr, o0_scr, el_scr)
    s_init, s_chunks = _ssd_program(ps_ref, pm_ref, s_nega, s_dtb, s_dsk, s_ng, so_ref, h_ref)

    @pl.when(pl.program_id(1) == 0)
    def _():
        g_init()
        s_init()

    n_groups = tt // CHUNK // MIX_CHUNKS
    for ci in range(n_groups):
        stages = [g_prepare(ci), s_chunks(ci)]
        if ci > 0:
            stages.append(g_scan(ci - 1))
        _interleave(*stages)
    _interleave(g_scan(n_groups - 1))


def _mixer(pg, ps, pm, gdn_a_log, gdn_dt_bias, gdn_norm_g, ssd_a_log, ssd_dt_bias, ssd_d, ssd_norm_g, bsz, seq):
    tt = min(TT_SCAN, seq)
    nt = seq // tt
    units = tt // CHUNK * GDN_HEADS

    def lanes(v, lane0):
        return jnp.zeros((1, LANES), F32).at[0, lane0:lane0 + v.shape[0]].set(v)

    def tile(width):
        return pl.BlockSpec((tt, width), lambda b, t: (b * nt + t, 0))

    def const(width):
        return pl.BlockSpec((1, width), lambda b, t: (0, 0))

    return pl.pallas_call(
        functools.partial(_mixer_kernel, tt=tt),
        grid=(bsz, nt),
        in_specs=[tile(PG_W), tile(PS_W), tile(LANES), const(LANES), const(LANES), const(GDN_DV),
                  const(LANES), const(LANES), const(LANES), const(SSD_INNER)],
        out_specs=[tile(GDN_V), tile(SSD_INNER)],
        out_shape=[jax.ShapeDtypeStruct((bsz * seq, GDN_V), BF16),
                   jax.ShapeDtypeStruct((bsz * seq, SSD_INNER), BF16)],
        scratch_shapes=[
            pltpu.VMEM((GDN_HEADS, GDN_DK, GDN_DV), F32),
            pltpu.VMEM((units, GDN_DK + CHUNK, GDN_DK), BF16),
            pltpu.VMEM((units, GDN_DK, GDN_DV), F32),
            pltpu.VMEM((units, CHUNK, GDN_DV), F32),
            pltpu.VMEM((tt // CHUNK * SUBLANES, LANES), F32),
            pltpu.VMEM((SSD_G, SSD_N, SSD_INNER // SSD_G), F32),
        ],
        compiler_params=pltpu.CompilerParams(
            dimension_semantics=("parallel", "arbitrary"), vmem_limit_bytes=VMEM_LIMIT),
        name="mixer_scan",
    )(pg, ps, pm, lanes(-LOG2_E * jnp.exp(gdn_a_log), LANE_ALPHA), lanes(gdn_dt_bias, LANE_ALPHA),
      gdn_norm_g.reshape(1, GDN_DV), lanes(-LOG2_E * jnp.exp(ssd_a_log), LANE_DT), lanes(ssd_dt_bias, LANE_DT),
      lanes(ssd_d, LANE_DT), ssd_norm_g.reshape(1, SSD_INNER))


RT_IDX = 0
RT_GATE = TOP_K
RT_POS = 2 * TOP_K


def _outproj_router_kernel(go_ref, so_ref, x_ref, mod_ref, g_ref, wo_ref, rw_ref, rb_ref,
                           xn_ref, h_ref, rt_ref, cnt_ref, run_ref, *, tm):
    i = pl.program_id(0)

    @pl.when(i == 0)
    def _():
        run_ref[...] = jnp.zeros_like(run_ref)

    sub = tm // ROUTER_SPLIT
    lane = lax.broadcasted_iota(jnp.int32, (sub, LANES), 1).astype(F32)
    trow = lax.broadcasted_iota(jnp.int32, (sub, sub), 0)
    tcol = lax.broadcasted_iota(jnp.int32, (sub, sub), 1)
    earlier = (trow > tcol).astype(BF16)
    onehots = [None] * ROUTER_SPLIT

    def part(j):
        rows = slice(j * sub, (j + 1) * sub)
        mix = _dot(go_ref[rows, :], wo_ref[0:GDN_V, :]) + _dot(so_ref[rows, :], wo_ref[GDN_V:, :])
        yield
        xn = x_ref[rows, :] + mod_ref[2:3, :] * mix
        xn_ref[rows, :] = xn
        h = _modulated_norm(xn, g_ref[...], mod_ref[3:4, :], mod_ref[4:5, :])
        h_ref[rows, :] = _pack_rows(h)
        yield
        h_hi = h.astype(BF16)
        h_lo = (h - h_hi.astype(F32)).astype(BF16)
        hw = _dot(h_hi, rw_ref[...])
        work = hw[:, 0:LANES] + hw[:, LANES:] + _dot(h_lo, rw_ref[:, 0:LANES]) + rb_ref[...]
        yield
        tops = []
        idxs = []
        for _ in range(TOP_K):
            m = jnp.max(work, axis=-1, keepdims=True)
            idx = jnp.min(jnp.where(work == m, lane, float(LANES)), axis=-1, keepdims=True)
            work = jnp.where(lane == idx, -jnp.inf, work)
            tops.append(m)
            idxs.append(idx)
            yield
        exps = [jnp.exp(m - tops[0]) for m in tops]
        denom = exps[0] + exps[1] + exps[2] + exps[3]
        onehot = jnp.zeros((sub, LANES), F32)
        for idx in idxs:
            onehot = onehot + (lane == idx).astype(F32)
        onehots[j] = onehot
        yield
        prior = run_ref[0:1, :]
        for jj in range(j):
            prior = prior + jnp.sum(onehots[jj], axis=0, keepdims=True)
        before = _dot(earlier, onehot.astype(BF16)) + prior
        yield
        rt = jnp.zeros((sub, LANES), F32)
        for k in range(TOP_K):
            pos = jnp.sum(jnp.where(lane == idxs[k], before, 0.0), axis=-1, keepdims=True)
            rt = jnp.where(lane == RT_IDX + k, idxs[k], rt)
            rt = jnp.where(lane == RT_GATE + k, exps[k] / denom, rt)
            rt = jnp.where(lane == RT_POS + k, pos, rt)
        rt_ref[rows, :] = rt

    _interleave(*[part(j) for j in range(ROUTER_SPLIT)])
    run = run_ref[0:1, :]
    for onehot in onehots:
        run = run + jnp.sum(onehot, axis=0, keepdims=True)
    run_ref[...] = jnp.broadcast_to(run, run_ref.shape)
    cnt_ref[...] = jnp.broadcast_to(run, cnt_ref.shape)


def _outproj_router(go, so, x2, mod_l, g, w_out, rw, rb, seq):
    n_tok = x2.shape[0]
    tm = min(TM_ROUTER, seq)
    return pl.pallas_call(
        functools.partial(_outproj_router_kernel, tm=tm),
        grid=(n_tok // tm,),
        in_specs=[
            pl.BlockSpec((tm, GDN_V), lambda i: (i, 0)),
            pl.BlockSpec((tm, SSD_INNER), lambda i: (i, 0)),
            pl.BlockSpec((tm, D_MODEL), lambda i: (i, 0)),
            pl.BlockSpec((None, SUBLANES, D_MODEL), lambda i: ((i * tm) // seq, 0, 0)),
            pl.BlockSpec((1, D_MODEL), lambda i: (0, 0)),
            pl.BlockSpec((GDN_V + SSD_INNER, D_MODEL), lambda i: (0, 0)),
            pl.BlockSpec((D_MODEL, 2 * LANES), lambda i: (0, 0)),
            pl.BlockSpec((1, LANES), lambda i: (0, 0)),
        ],
        out_specs=[
            pl.BlockSpec((tm, D_MODEL), lambda i: (i, 0)),
            pl.BlockSpec((tm, ROW_WORDS), lambda i: (i, 0)),
            pl.BlockSpec((tm, LANES), lambda i: (i, 0)),
            pl.BlockSpec((SUBLANES, LANES), lambda i: (0, 0)),
        ],
        out_shape=[
            jax.ShapeDtypeStruct((n_tok, D_MODEL), F32),
            jax.ShapeDtypeStruct((n_tok, ROW_WORDS), U32),
            jax.ShapeDtypeStruct((n_tok, LANES), F32),
            jax.ShapeDtypeStruct((SUBLANES, LANES), F32),
        ],
        scratch_shapes=[pltpu.VMEM((SUBLANES, LANES), F32)],
        compiler_params=pltpu.CompilerParams(
            dimension_semantics=("arbitrary",), vmem_limit_bytes=VMEM_LIMIT),
        name="outproj_router",
    )(go, so, x2, mod_l, g, w_out, rw, rb)


def _sc_workers():
    info = plsc.get_sparse_core_info()
    return info.num_cores, info.num_subcores


def _scatter_rows(src, dest_t, n_rows):
    n_tok, width = src.shape
    n_k = dest_t.shape[0]
    nc, ns = _sc_workers()
    per_w = n_tok // (nc * ns)
    win = min(SC_WINDOW, per_w)
    mesh = plsc.VectorSubcoreMesh(core_axis_name="c", subcore_axis_name="s")

    @functools.partial(
        pl.kernel, mesh=mesh,
        out_type=jax.ShapeDtypeStruct((n_rows, width), src.dtype),
        scratch_types=[pltpu.VMEM((win,), jnp.int32), pltpu.VMEM((win, width), src.dtype)],
    )
    def scatter_kernel(src_hbm, idx_hbm, out_hbm, idx_v, rows_v):
        wid = lax.axis_index("s") * nc + lax.axis_index("c")
        base = wid * per_w

        @pl.loop(0, per_w // win)
        def _(j):
            off = base + j * win
            pltpu.sync_copy(src_hbm.at[pl.ds(off, win)], rows_v)
            for k in range(n_k):
                pltpu.sync_copy(idx_hbm.at[k, pl.ds(off, win)], idx_v)
                pltpu.sync_copy(rows_v, out_hbm.at[idx_v])

    return scatter_kernel(src, dest_t)


def _gather_rows(table, idx):
    n_idx = idx.shape[0]
    width = table.shape[1]
    nc, ns = _sc_workers()
    per_w = n_idx // (nc * ns)
    win = min(SC_WINDOW, per_w)
    mesh = plsc.VectorSubcoreMesh(core_axis_name="c", subcore_axis_name="s")

    @functools.partial(
        pl.kernel, mesh=mesh,
        out_type=jax.ShapeDtypeStruct((n_idx, width), table.dtype),
        scratch_types=[pltpu.VMEM((per_w,), jnp.int32), pltpu.VMEM((win, width), table.dtype)],
    )
    def gather_kernel(table_hbm, idx_hbm, out_hbm, idx_v, rows_v):
        wid = lax.axis_index("s") * nc + lax.axis_index("c")
        base = wid * per_w
        pltpu.sync_copy(idx_hbm.at[pl.ds(base, per_w)], idx_v)

        @pl.loop(0, per_w // win)
        def _(j):
            pltpu.sync_copy(table_hbm.at[idx_v.at[pl.ds(j * win, win)]], rows_v)
            pltpu.sync_copy(rows_v, out_hbm.at[pl.ds(base + j * win, win)])

    return gather_kernel(table, idx)


def _expert_kernel(be_ref, bv_ref, x_ref, wgu_ref, bgu_ref, wd_ref, bd_ref, y_ref, wgu_b, wd_b, *, tr):
    i = pl.program_id(0)
    e = be_ref[i]
    prev = be_ref[jnp.maximum(i - 1, 0)]

    @pl.when((i == 0) | (e != prev))
    def _():
        wgu_b[...] = wgu_ref[...].astype(BF16)
        wd_b[...] = wd_ref[...].astype(BF16)

    valid = bv_ref[i]
    for r0 in range(0, tr, SUB_BLOCK):
        part = pl.ds(r0, SUB_BLOCK)

        @pl.when(valid <= r0)
        def _():
            y_ref[part, :] = jnp.zeros((SUB_BLOCK, y_ref.shape[1]), y_ref.dtype)

        @pl.when(valid > r0)
        def _():
            rows = lax.broadcasted_iota(jnp.int32, (SUB_BLOCK, D_MODEL), 0) + r0
            x = jnp.where(rows < valid, _unpack_rows(x_ref[part, :]), 0.0).astype(BF16)
            acc = jnp.zeros((SUB_BLOCK, D_MODEL), F32)
            for f in range(0, D_FF, FF_CHUNK):
                gate = _dot(x, wgu_b[:, f:f + FF_CHUNK]) + bgu_ref[:, f:f + FF_CHUNK]
                up = _dot(x, wgu_b[:, D_FF + f:D_FF + f + FF_CHUNK]) + bgu_ref[:, D_FF + f:D_FF + f + FF_CHUNK]
                gate = jnp.minimum(gate, SWIGLU_LIMIT)
                up = jnp.clip(up, -SWIGLU_LIMIT, SWIGLU_LIMIT)
                act = (up + 1.0) * (gate * _sigmoid(gate * SWIGLU_ALPHA))
                acc = acc + _dot(act.astype(BF16), wd_b[f:f + FF_CHUNK, :])
            y_ref[part, :] = _pack_rows(acc + bd_ref[...])


def _experts(x_rows, blk_expert, blk_valid, w_gu, b_gu, w_down, b_down, layer):
    n_rows, row_w = x_rows.shape
    depth = w_gu.shape[0]
    tr = ROW_BLOCK
    nb = n_rows // tr
    grid_spec = pltpu.PrefetchScalarGridSpec(
        num_scalar_prefetch=2,
        grid=(nb,),
        in_specs=[
            pl.BlockSpec((tr, row_w), lambda i, be, bv: (i, 0)),
            pl.BlockSpec((None, None, D_MODEL, 2 * D_FF), lambda i, be, bv: (layer, be[i], 0, 0)),
            pl.BlockSpec((None, None, 1, 2 * D_FF), lambda i, be, bv: (layer, be[i], 0, 0)),
            pl.BlockSpec((None, None, D_FF, D_MODEL), lambda i, be, bv: (layer, be[i], 0, 0)),
            pl.BlockSpec((None, None, 1, D_MODEL), lambda i, be, bv: (layer, be[i], 0, 0)),
        ],
        out_specs=pl.BlockSpec((tr, row_w), lambda i, be, bv: (i, 0)),
        scratch_shapes=[pltpu.VMEM((D_MODEL, 2 * D_FF), BF16), pltpu.VMEM((D_FF, D_MODEL), BF16)],
    )
    return pl.pallas_call(
        functools.partial(_expert_kernel, tr=tr),
        grid_spec=grid_spec,
        out_shape=jax.ShapeDtypeStruct((n_rows, row_w), x_rows.dtype),
        compiler_params=pltpu.CompilerParams(
            dimension_semantics=("arbitrary",), vmem_limit_bytes=VMEM_LIMIT),
        name="moe_experts",
    )(blk_expert, blk_valid, x_rows, w_gu, b_gu.reshape(depth, N_EXPERTS, 1, 2 * D_FF), w_down,
      b_down.reshape(depth, N_EXPERTS, 1, D_MODEL))


def _combine_kernel(xn_ref, y4_ref, rt_ref, mod_ref, g_ref, o_ref, *, final):
    rt = rt_ref[...]
    acc = rt[:, RT_GATE:RT_GATE + 1] * _unpack_rows(y4_ref[0])
    for k in range(1, TOP_K):
        acc = acc + rt[:, RT_GATE + k:RT_GATE + k + 1] * _unpack_rows(y4_ref[k])
    x = xn_ref[...] + mod_ref[5:6, :] * acc
    if final:
        ms = jnp.mean(x * x, axis=-1, keepdims=True)
        x = x * lax.rsqrt(ms + EPS) * g_ref[...]
    o_ref[...] = x


def _combine(xn, y4, rt, mod_l, final_g, seq, final):
    n_tok = xn.shape[0]
    row_w = y4.shape[2]
    tm = min(TM_PROJ, seq)
    return pl.pallas_call(
        functools.partial(_combine_kernel, final=final),
        grid=(n_tok // tm,),
        in_specs=[
            pl.BlockSpec((tm, D_MODEL), lambda i: (i, 0)),
            pl.BlockSpec((TOP_K, tm, row_w), lambda i: (0, i, 0)),
            pl.BlockSpec((tm, LANES), lambda i: (i, 0)),
            pl.BlockSpec((None, SUBLANES, D_MODEL), lambda i: ((i * tm) // seq, 0, 0)),
            pl.BlockSpec((1, D_MODEL), lambda i: (0, 0)),
        ],
        out_specs=pl.BlockSpec((tm, D_MODEL), lambda i: (i, 0)),
        out_shape=jax.ShapeDtypeStruct((n_tok, D_MODEL), F32),
        compiler_params=pltpu.CompilerParams(
            dimension_semantics=("parallel",), vmem_limit_bytes=VMEM_LIMIT),
        name="moe_combine",
    )(xn, y4, rt, mod_l, final_g)


def _permute_w_in(w_in):
    off = 0
    gdn_qkvz = w_in[:, off:off + PG_W]; off += PG_W
    small_ba = w_in[:, off:off + 2 * GDN_HEADS]; off += 2 * GDN_HEADS
    ssd_zxbc = w_in[:, off:off + PS_W]; off += PS_W
    ssd_dt = w_in[:, off:off + SSD_HEADS]
    pad = jnp.zeros((w_in.shape[0], LANES - 2 * GDN_HEADS - SSD_HEADS), w_in.dtype)
    return jnp.concatenate([gdn_qkvz, ssd_zxbc, small_ba, ssd_dt, pad], axis=1).astype(BF16)


def _moe_schedule(rt, counts_row, n_rows):
    idx = rt[:, RT_IDX:RT_IDX + TOP_K].astype(jnp.int32)
    pos = rt[:, RT_POS:RT_POS + TOP_K].astype(jnp.int32)
    counts = counts_row[0, :N_EXPERTS].astype(jnp.int32)
    padded = (counts + ROW_BLOCK - 1) // ROW_BLOCK * ROW_BLOCK
    pad_end = jnp.cumsum(padded)
    pad_start = pad_end - padded
    experts = jnp.arange(N_EXPERTS, dtype=jnp.int32)
    dest = pos + jnp.sum(jnp.where(idx[..., None] == experts, pad_start, 0), axis=-1)
    blk_start = jnp.arange(n_rows // ROW_BLOCK, dtype=jnp.int32) * ROW_BLOCK
    blk_expert = jnp.minimum(jnp.sum(blk_start[:, None] >= pad_end[None, :], axis=1), N_EXPERTS - 1)
    blk_onehot = blk_expert[:, None] == experts
    blk_valid = (jnp.sum(jnp.where(blk_onehot, counts + pad_start, 0), axis=1) - blk_start)
    blk_valid = jnp.where(blk_start < pad_end[-1], jnp.clip(blk_valid, 0, ROW_BLOCK), 0)
    return dest, blk_expert.astype(jnp.int32), blk_valid.astype(jnp.int32)


def _forward(x, c, ada_w, ada_b, norm1_g, norm2_g, w_in, gdn_conv_w, gdn_a_log, gdn_dt_bias, gdn_norm_g,
             ssd_conv_w, ssd_conv_b, ssd_a_log, ssd_dt_bias, ssd_d, ssd_norm_g, w_out, router_w, router_b,
             moe_w_gu, moe_b_gu, moe_w_down, moe_b_down, final_g):
    bsz, seq, _ = x.shape
    depth = ada_w.shape[0]
    n_tok = bsz * seq
    n_rows = n_tok * TOP_K + N_EXPERTS * ROW_BLOCK
    mod = _ada_mod(c, ada_w, ada_b)
    x2 = x.reshape(n_tok, D_MODEL)
    fg = final_g.reshape(1, D_MODEL)

    def prepare_weights(l, anchor):
        conv_w = jnp.concatenate([gdn_conv_w[l], ssd_conv_w[l]], axis=1) + anchor
        conv_b = jnp.concatenate([jnp.zeros((1, GDN_CONV_CH), F32), ssd_conv_b[l].reshape(1, -1)], axis=1)
        rw = jnp.zeros((D_MODEL, LANES), F32).at[:, :N_EXPERTS].set(router_w[l]) + anchor
        rw_hi = rw.astype(BF16)
        rw = jnp.concatenate([rw_hi, (rw - rw_hi.astype(F32)).astype(BF16)], axis=1)
        rb = jnp.full((1, LANES), -1e30, F32).at[0, :N_EXPERTS].set(router_b[l])
        return (_permute_w_in(w_in[l] + anchor), conv_w, conv_b, rw, rb, (w_out[l] + anchor).astype(BF16))

    prepared = prepare_weights(0, jnp.zeros((), F32))
    for l in range(depth):
        w_perm, conv_w, conv_b, rw, rb, w_out_b = prepared
        pg, ps, pm = _inproj(x2, mod[l], norm1_g[l].reshape(1, D_MODEL), w_perm, conv_w, conv_b, seq)
        go, so = _mixer(pg, ps, pm, gdn_a_log[l], gdn_dt_bias[l], gdn_norm_g[l],
                        ssd_a_log[l], ssd_dt_bias[l], ssd_d[l], ssd_norm_g[l], bsz, seq)
        xn, h2, rt, cnt = _outproj_router(go, so, x2, mod[l], norm2_g[l].reshape(1, D_MODEL),
                                          w_out_b, rw, rb, seq)
        dest, blk_expert, blk_valid = _moe_schedule(rt, cnt, n_rows)
        dest_t = dest.T
        if l + 1 < depth:
            dest_t, anchor = lax.optimization_barrier((dest_t, jnp.zeros((), F32)))
            prepared = prepare_weights(l + 1, anchor)
        x_rows = _scatter_rows(h2, dest_t, n_rows)
        y_rows = _experts(x_rows, blk_expert, blk_valid, moe_w_gu, moe_b_gu, moe_w_down, moe_b_down, l)
        y4 = _gather_rows(y_rows, dest_t.reshape(TOP_K * n_tok)).reshape(TOP_K, n_tok, y_rows.shape[1])
        x2 = _combine(xn, y4, rt, mod[l], fg, seq, final=(l == depth - 1))
    return x2.reshape(bsz, seq, D_MODEL)


def kernel(x, c, ada_w, ada_b, norm1_g, norm2_g, w_in, gdn_conv_w, gdn_a_log, gdn_dt_bias, gdn_norm_g, ssd_conv_w, ssd_conv_b, ssd_a_log, ssd_dt_bias, ssd_d, ssd_norm_g, w_out, router_w, router_b, moe_w_gu, moe_b_gu, moe_w_down, moe_b_down, final_g):
    return _forward(x, c, ada_w, ada_b, norm1_g, norm2_g, w_in, gdn_conv_w, gdn_a_log, gdn_dt_bias, gdn_norm_g,
                    ssd_conv_w, ssd_conv_b, ssd_a_log, ssd_dt_bias, ssd_d, ssd_norm_g, w_out, router_w, router_b,
                    moe_w_gu, moe_b_gu, moe_w_down, moe_b_down, final_g)
```
